```python
import math
import jax, jax.numpy as jnp
from jax import lax
import numpy as np

D_MODEL = 1024
BATCH = 2
SEQ = 8192
DEPTH = 4
DEC_BATCH = 128
DEC_SEQ = 4
PAST_LEN = 8192
PAGE_SIZE = 128

N_META = 16
N_MIXERS = 4
EPS = 1e-6

M_HEADS = 4
M_DK = 128
M_DV = D_MODEL // M_HEADS
M_CHUNK = 128

POOL_WINDOWS = (2, 4, 8, 16)
POOL_GROUPS = 4
POOL_GDIM = D_MODEL // POOL_GROUPS
POOL_MAX = 16
POOL_HIST = POOL_MAX - 1

MLA_HEADS = 16
MLA_NOPE = 64
MLA_ROPE = 32
MLA_QK = MLA_NOPE + MLA_ROPE
MLA_V = 64
MLA_Q_LORA = 512
MLA_KV_LORA = 256
MLA_CACHE_DIM = MLA_KV_LORA + MLA_ROPE
ROPE_BASE = 10000.0
Q_BLOCK = 128

CONV_WIDTH = 31
CONV_HIST = CONV_WIDTH - 1

D_FF = 2816
FFN_CONV_WIDTH = 3
FFN_HIST = FFN_CONV_WIDTH - 1

kernel_name = "hybrid_mlstm_pool_mla_conformer_decoder_step"


def rmsnorm(x, g):
    xf = x.astype(jnp.float32)
    y = xf * lax.rsqrt(jnp.mean(xf * xf, axis=-1, keepdims=True) + EPS)
    return (y * g.astype(jnp.float32)).astype(x.dtype)


def layernorm(x, g, b):
    xf = x.astype(jnp.float32)
    mu = jnp.mean(xf, axis=-1, keepdims=True)
    var = jnp.mean(jnp.square(xf - mu), axis=-1, keepdims=True)
    y = (xf - mu) * lax.rsqrt(var + EPS)
    return (y * g.astype(jnp.float32) + b.astype(jnp.float32)).astype(x.dtype)


def causal_dwconv(x_ext, w, b):
    out = lax.conv_general_dilated(
        x_ext, w[:, None, :].astype(x_ext.dtype), window_strides=(1,), padding="VALID",
        dimension_numbers=("NWC", "WIO", "NWC"), feature_group_count=x_ext.shape[-1])
    return out + b.astype(x_ext.dtype)


def rope_tail(x, pos):
    half = MLA_ROPE // 2
    inv = ROPE_BASE ** (-jnp.arange(half, dtype=jnp.float32) / half)
    ang = pos.astype(jnp.float32)[:, None] * inv[None, :]
    cos = jnp.cos(ang)[:, None, :]
    sin = jnp.sin(ang)[:, None, :]
    xn = x[..., :MLA_NOPE]
    x1 = x[..., MLA_NOPE:MLA_NOPE + half].astype(jnp.float32)
    x2 = x[..., MLA_NOPE + half:].astype(jnp.float32)
    rot = jnp.concatenate([x1 * cos - x2 * sin, x2 * cos + x1 * sin], axis=-1).astype(x.dtype)
    return jnp.concatenate([xn, rot], axis=-1)


def mlstm_inputs(h, m_w_in, m_b_gates):
    n, t, _ = h.shape
    z = h @ m_w_in
    i0 = M_HEADS * M_DK
    i1 = 2 * i0
    i2 = i1 + M_HEADS * M_DV
    i3 = i2 + M_HEADS * M_DV

    def heads(a, d):
        return a.reshape(n, t, M_HEADS, d).transpose(0, 2, 1, 3).astype(jnp.float32)

    q = heads(z[..., :i0], M_DK) * (M_DK ** -0.5)
    k = heads(z[..., i0:i1], M_DK)
    v = heads(z[..., i1:i2], M_DV)
    o = jax.nn.sigmoid(z[..., i2:i3])
    g = (z[..., i3:] + m_b_gates).astype(jnp.float32)
    ig = g[..., :M_HEADS].transpose(0, 2, 1)
    lf = jax.nn.log_sigmoid(g[..., M_HEADS:]).transpose(0, 2, 1)
    return q, k, v, ig, lf, o


def mlstm_chunk(state, q, k, v, ig, lf):
    c0, n0, m0 = state
    t = q.shape[2]
    b = jnp.cumsum(lf, axis=-1)
    causal = jnp.tril(jnp.ones((t, t), dtype=bool))
    dmat = jnp.where(causal, b[..., :, None] - b[..., None, :] + ig[..., None, :], -jnp.inf)
    inter = b + m0[..., None]
    m_t = jnp.maximum(inter, jnp.max(dmat, axis=-1))
    w_intra = jnp.exp(dmat - m_t[..., None])
    w_inter = jnp.exp(inter - m_t)
    s = jnp.einsum("nhtk,nhsk->nhts", q, k) * w_intra
    num = jnp.einsum("nhts,nhsv->nhtv", s, v) + w_inter[..., None] * jnp.einsum("nhtk,nhkv->nhtv", q, c0)
    den = jnp.sum(s, axis=-1) + w_inter * jnp.einsum("nhtk,nhk->nht", q, n0)
    h = num / jnp.maximum(jnp.abs(den), jnp.exp(-m_t))[..., None]
    b_last = b[..., -1]
    m_new = m_t[..., -1]
    wk = jnp.exp(b_last[..., None] - b + ig - m_new[..., None])
    decay = jnp.exp(b_last + m0 - m_new)
    c_new = decay[..., None, None] * c0 + jnp.einsum("nhs,nhsk,nhsv->nhkv", wk, k, v)
    n_new = decay[..., None] * n0 + jnp.einsum("nhs,nhsk->nhk", wk, k)
    return (c_new, n_new, m_new), h


def mixer_mlstm(h, p, st):
    q, k, v, ig, lf, o = mlstm_inputs(h, p["m_w_in"], p["m_b_gates"])
    n = h.shape[0]
    if st is None:
        state = (jnp.zeros((n, M_HEADS, M_DK, M_DV), jnp.float32),
                 jnp.zeros((n, M_HEADS, M_DK), jnp.float32),
                 jnp.zeros((n, M_HEADS), jnp.float32))
        state, h_meta = mlstm_chunk(state, *tuple(a[:, :, :N_META] for a in (q, k, v, ig, lf)))

        def to_chunks(a):
            a = a[:, :, N_META:]
            nc = a.shape[2] // M_CHUNK
            a = a.reshape(a.shape[:2] + (nc, M_CHUNK) + a.shape[3:])
            return jnp.moveaxis(a, 2, 0)

        xs = tuple(to_chunks(a) for a in (q, k, v, ig, lf))
        state, hc = lax.scan(lambda s, c: mlstm_chunk(s, *c), state, xs)
        h_real = jnp.moveaxis(hc, 0, 2)
        h_real = h_real.reshape(h_real.shape[:2] + (-1, M_DV))
        hh = jnp.concatenate([h_meta, h_real], axis=2)
    else:
        state = tuple(a.astype(jnp.float32) for a in (st["mlstm_C"], st["mlstm_n"], st["mlstm_m"]))
        state, hh = mlstm_chunk(state, q, k, v, ig, lf)
    t = hh.shape[2]
    hn = rmsnorm(hh, p["m_head_norm_g"][:, None, :])
    hn = hn.transpose(0, 2, 1, 3).reshape(n, t, M_HEADS * M_DV).astype(h.dtype)
    y = (hn * o) @ p["m_w_out"]
    return y, state


def pool_mix(u_ext, pos_out, n_out, pool_w, pool_scale):
    n, t_ext, _ = u_ext.shape
    uf = u_ext.astype(jnp.float32)
    pz = jnp.pad(jnp.cumsum(uf, axis=1), ((0, 0), (POOL_MAX + 1, 0), (0, 0)))
    t0 = t_ext - n_out + POOL_MAX + 1
    u_out = uf[:, t_ext - n_out:]
    parts = []
    for g, w in enumerate(POOL_WINDOWS):
        sl = slice(g * POOL_GDIM, (g + 1) * POOL_GDIM)
        win = pz[:, t0:t0 + n_out, sl] - pz[:, t0 - w:t0 - w + n_out, sl]
        cnt = jnp.minimum(pos_out + 1, w).astype(jnp.float32)[None, :, None]
        parts.append(win / cnt - u_out[..., sl])
    d = jnp.stack(parts, axis=2).astype(u_ext.dtype)
    y = jnp.einsum("ntgc,gcd->ntgd", d, pool_w).reshape(n, n_out, D_MODEL)
    return y * pool_scale


def mixer_pool(h, p, st):
    t = h.shape[1]
    if st is None:
        y = pool_mix(h, jnp.arange(t, dtype=jnp.int32), t, p["pool_w"], p["pool_scale"])
        return y, h[:, -POOL_HIST:]
    ext = jnp.concatenate([st["pool"].astype(h.dtype), h], axis=1)
    pos_out = st["past_len"] + jnp.arange(t, dtype=jnp.int32)
    y = pool_mix(ext, pos_out, t, p["pool_w"], p["pool_scale"])
    return y, ext[:, -POOL_HIST:]


def mla_rows_and_queries(h, p, pos):
    n, t, _ = h.shape
    cq = rmsnorm(h @ p["mla_w_dq"], p["mla_q_lora_g"])
    q = (cq @ p["mla_w_uq"]).reshape(n, t, MLA_HEADS, MLA_QK)
    q = rope_tail(rmsnorm(q, p["mla_q_norm_g"]), pos)
    ckv = h @ p["mla_w_dkv"]
    lat = rmsnorm(ckv[..., :MLA_KV_LORA], p["mla_kv_lora_g"])
    rows = jnp.concatenate([lat, ckv[..., MLA_KV_LORA:]], axis=-1)
    return q, rows


def mla_keys(rows, pos, p):
    lat = rows[..., :MLA_KV_LORA]
    kpe = rows[..., MLA_KV_LORA:]
    kv = (lat @ p["mla_w_ukv"]).reshape(lat.shape[:-1] + (MLA_HEADS, MLA_NOPE + MLA_V))
    k_nope, v = kv[..., :MLA_NOPE], kv[..., MLA_NOPE:]
    kpe_h = jnp.broadcast_to(kpe[..., None, :], k_nope.shape[:-1] + (MLA_ROPE,))
    k = jnp.concatenate([k_nope, kpe_h.astype(k_nope.dtype)], axis=-1)
    k = rope_tail(rmsnorm(k, p["mla_k_norm_g"]), pos)
    return k, v


def mixer_mla(h, p, st, page_table):
    n, t, _ = h.shape
    scale = MLA_QK ** -0.5
    if st is None:
        pos = jnp.arange(t, dtype=jnp.int32)
        q, rows = mla_rows_and_queries(h, p, pos)
        k, v = mla_keys(rows, pos, p)
        nb = -(-t // Q_BLOCK)
        qp = jnp.pad(q, ((0, 0), (0, nb * Q_BLOCK - t), (0, 0), (0, 0)))
        qb = jnp.moveaxis(qp.reshape(n, nb, Q_BLOCK, MLA_HEADS, MLA_QK), 1, 0)

        def block(args):
            qblk, start = args
            qpos = start + jnp.arange(Q_BLOCK, dtype=jnp.int32)
            s = jnp.einsum("nqhd,nkhd->nhqk", qblk, k).astype(jnp.float32) * scale
            s = jnp.where(pos[None, :] <= qpos[:, None], s, -jnp.inf)
            pr = jax.nn.softmax(s, axis=-1).astype(v.dtype)
            return jnp.einsum("nhqk,nkhd->nqhd", pr, v)

        o = lax.map(block, (qb, jnp.arange(nb, dtype=jnp.int32) * Q_BLOCK))
        o = jnp.moveaxis(o, 0, 1).reshape(n, nb * Q_BLOCK, MLA_HEADS * MLA_V)[:, :t]
        return o @ p["mla_w_o"], rows
    cache = st["mla"]
    n_pages = page_table.shape[1]
    past = n_pages * cache.shape[1]
    qpos = past + jnp.arange(t, dtype=jnp.int32)
    q, rows = mla_rows_and_queries(h, p, qpos)
    kpos = jnp.arange(past + t, dtype=jnp.int32)
    mask = kpos[None, :] <= qpos[:, None]

    def one_seq(args):
        pt, r_new, qs = args
        past_rows = cache[pt].reshape(past, MLA_CACHE_DIM)
        all_rows = jnp.concatenate([past_rows, r_new.astype(past_rows.dtype)], axis=0)
        k, v = mla_keys(all_rows, kpos, p)
        s = jnp.einsum("qhd,khd->hqk", qs, k.astype(qs.dtype)).astype(jnp.float32) * scale
        s = jnp.where(mask[None], s, -jnp.inf)
        pr = jax.nn.softmax(s, axis=-1).astype(v.dtype)
        return jnp.einsum("hqk,khd->qhd", pr, v).reshape(t, MLA_HEADS * MLA_V)

    o = lax.map(one_seq, (page_table, rows, q)).astype(h.dtype)
    return o @ p["mla_w_o"], rows


def mixer_conv(h, p, st):
    a = h @ p["conv_w_pw1"] + p["conv_b_pw1"]
    gl = a[..., :D_MODEL] * jax.nn.sigmoid(a[..., D_MODEL:])
    if st is None:
        hist = jnp.zeros((h.shape[0], CONV_HIST, D_MODEL), gl.dtype)
    else:
        hist = st["conv"].astype(gl.dtype)
    ext = jnp.concatenate([hist, gl], axis=1)
    c = causal_dwconv(ext, p["conv_w_dw"], p["conv_b_dw"])
    c = layernorm(c, p["conv_ln_g"], p["conv_ln_b"])
    y = jax.nn.silu(c) @ p["conv_w_pw2"] + p["conv_b_pw2"]
    return y, ext[:, -CONV_HIST:]


def conv_ffn(h, hist, w_up, w_conv, b_conv, w_down):
    up = h @ w_up
    g, u = up[..., :D_FF], up[..., D_FF:]
    ext = jnp.concatenate([hist.astype(g.dtype), g], axis=1)
    gc = causal_dwconv(ext, w_conv, b_conv)
    y = (jax.nn.silu(gc) * u) @ w_down
    return y, ext[:, -FFN_HIST:]


def run_trunk(x, p, st, page_table):
    new = {}
    ffn_hist = []
    for layer in range(DEPTH):
        kind = layer % N_MIXERS
        h = rmsnorm(x, p["norm_mix_g"][layer])
        if kind == 0:
            y, s = mixer_mlstm(h, p, st)
            new["mlstm_C"], new["mlstm_n"], new["mlstm_m"] = s
        elif kind == 1:
            y, new["pool"] = mixer_pool(h, p, st)
        elif kind == 2:
            y, new["mla"] = mixer_mla(h, p, st, page_table)
        else:
            y, new["conv"] = mixer_conv(h, p, st)
        x = x + y.astype(x.dtype)
        h = rmsnorm(x, p["norm_ffn_g"][layer])
        if st is None:
            hist = jnp.zeros((x.shape[0], FFN_HIST, D_FF), x.dtype)
        else:
            hist = st["ffn"][layer]
        y, fh = conv_ffn(h, hist, p["ffn_w_up"][layer], p["ffn_conv_w"][layer],
                         p["ffn_conv_b"][layer], p["ffn_w_down"][layer])
        ffn_hist.append(fh)
        x = x + y.astype(x.dtype)
    new["ffn"] = jnp.stack(ffn_hist, axis=0)
    return x, new


def setup_inputs(seed: int = 0) -> dict:
    key = jax.random.key(seed)
    ks = iter(jax.random.split(key, 64))

    def nrm(shape, scale=1.0):
        return jax.random.normal(next(ks), shape, jnp.float32) * scale

    def wt(fan_in, shape):
        return nrm(shape, fan_in ** -0.5)

    def gain(shape):
        return 1.0 + nrm(shape, 0.1)

    n_pages = PAST_LEN // PAGE_SIZE
    n_used = DEC_BATCH * n_pages
    n_phys = (n_used * 5) // 4
    page_table = jax.random.permutation(next(ks), n_phys)[:n_used].reshape(DEC_BATCH, n_pages).astype(jnp.int32)
    m_in_width = 2 * M_HEADS * M_DK + 2 * M_HEADS * M_DV + 2 * M_HEADS
    m_b_gates = jnp.concatenate([nrm((M_HEADS,), 0.1), 3.0 + nrm((M_HEADS,), 0.5)])
    return {
        "x_prompt": nrm((BATCH, SEQ, D_MODEL)),
        "x_sample": nrm((DEC_BATCH, DEC_SEQ, D_MODEL)),
        "state_mlstm_C": nrm((DEC_BATCH, M_HEADS, M_DK, M_DV), 0.05),
        "state_mlstm_n": nrm((DEC_BATCH, M_HEADS, M_DK), 0.5),
        "state_mlstm_m": nrm((DEC_BATCH, M_HEADS), 0.5),
        "state_pool": nrm((DEC_BATCH, POOL_HIST, D_MODEL)),
        "cache_mla": nrm((n_phys, PAGE_SIZE, MLA_CACHE_DIM)),
        "page_table": page_table,
        "state_conv": nrm((DEC_BATCH, CONV_HIST, D_MODEL), 0.5),
        "state_ffn": nrm((DEPTH, DEC_BATCH, FFN_HIST, D_FF)),
        "meta_tokens": nrm((N_META, D_MODEL)),
        "norm_mix_g": gain((DEPTH, D_MODEL)),
        "norm_ffn_g": gain((DEPTH, D_MODEL)),
        "m_w_in": wt(D_MODEL, (D_MODEL, m_in_width)),
        "m_b_gates": m_b_gates,
        "m_head_norm_g": gain((M_HEADS, M_DV)),
        "m_w_out": wt(M_HEADS * M_DV, (M_HEADS * M_DV, D_MODEL)),
        "pool_w": wt(POOL_GDIM, (POOL_GROUPS, POOL_GDIM, POOL_GDIM)),
        "pool_scale": gain((D_MODEL,)),
        "mla_w_dq": wt(D_MODEL, (D_MODEL, MLA_Q_LORA)),
        "mla_q_lora_g": gain((MLA_Q_LORA,)),
        "mla_w_uq": wt(MLA_Q_LORA, (MLA_Q_LORA, MLA_HEADS * MLA_QK)),
        "mla_w_dkv": wt(D_MODEL, (D_MODEL, MLA_CACHE_DIM)),
        "mla_kv_lora_g": gain((MLA_KV_LORA,)),
        "mla_w_ukv": wt(MLA_KV_LORA, (MLA_KV_LORA, MLA_HEADS * (MLA_NOPE + MLA_V))),
        "mla_q_norm_g": gain((MLA_QK,)),
        "mla_k_norm_g": gain((MLA_QK,)),
        "mla_w_o": wt(MLA_HEADS * MLA_V, (MLA_HEADS * MLA_V, D_MODEL)),
        "conv_w_pw1": wt(D_MODEL, (D_MODEL, 2 * D_MODEL)),
        "conv_b_pw1": nrm((2 * D_MODEL,), 0.02),
        "conv_w_dw": wt(CONV_WIDTH, (CONV_WIDTH, D_MODEL)),
        "conv_b_dw": nrm((D_MODEL,), 0.02),
        "conv_ln_g": gain((D_MODEL,)),
        "conv_ln_b": nrm((D_MODEL,), 0.02),
        "conv_w_pw2": wt(D_MODEL, (D_MODEL, D_MODEL)),
        "conv_b_pw2": nrm((D_MODEL,), 0.02),
        "ffn_w_up": wt(D_MODEL, (DEPTH, D_MODEL, 2 * D_FF)),
        "ffn_conv_w": wt(FFN_CONV_WIDTH, (DEPTH, FFN_CONV_WIDTH, D_FF)),
        "ffn_conv_b": nrm((DEPTH, D_FF), 0.02),
        "ffn_w_down": wt(D_FF, (DEPTH, D_FF, D_MODEL)),
    }


def reference(x_prompt, x_sample, state_mlstm_C, state_mlstm_n, state_mlstm_m, state_pool,
              cache_mla, page_table, state_conv, state_ffn, meta_tokens, norm_mix_g, norm_ffn_g,
              m_w_in, m_b_gates, m_head_norm_g, m_w_out, pool_w, pool_scale,
              mla_w_dq, mla_q_lora_g, mla_w_uq, mla_w_dkv, mla_kv_lora_g, mla_w_ukv,
              mla_q_norm_g, mla_k_norm_g, mla_w_o, conv_w_pw1, conv_b_pw1, conv_w_dw, conv_b_dw,
              conv_ln_g, conv_ln_b, conv_w_pw2, conv_b_pw2, ffn_w_up, ffn_conv_w, ffn_conv_b, ffn_w_down):
    p = dict(norm_mix_g=norm_mix_g, norm_ffn_g=norm_ffn_g, m_w_in=m_w_in, m_b_gates=m_b_gates,
             m_head_norm_g=m_head_norm_g, m_w_out=m_w_out, pool_w=pool_w, pool_scale=pool_scale,
             mla_w_dq=mla_w_dq, mla_q_lora_g=mla_q_lora_g, mla_w_uq=mla_w_uq, mla_w_dkv=mla_w_dkv,
             mla_kv_lora_g=mla_kv_lora_g, mla_w_ukv=mla_w_ukv, mla_q_norm_g=mla_q_norm_g,
             mla_k_norm_g=mla_k_norm_g, mla_w_o=mla_w_o, conv_w_pw1=conv_w_pw1, conv_b_pw1=conv_b_pw1,
             conv_w_dw=conv_w_dw, conv_b_dw=conv_b_dw, conv_ln_g=conv_ln_g, conv_ln_b=conv_ln_b,
             conv_w_pw2=conv_w_pw2, conv_b_pw2=conv_b_pw2, ffn_w_up=ffn_w_up, ffn_conv_w=ffn_conv_w,
             ffn_conv_b=ffn_conv_b, ffn_w_down=ffn_w_down)
    meta = jnp.broadcast_to(meta_tokens[None].astype(x_prompt.dtype), (x_prompt.shape[0], N_META, D_MODEL))
    x_full = jnp.concatenate([meta, x_prompt], axis=1)
    yp, newp = run_trunk(x_full, p, None, None)
    y_prompt = yp[:, N_META:]
    past_len = page_table.shape[1] * cache_mla.shape[1]
    st = dict(mlstm_C=state_mlstm_C, mlstm_n=state_mlstm_n, mlstm_m=state_mlstm_m, pool=state_pool,
              mla=cache_mla, conv=state_conv, ffn=state_ffn, past_len=past_len)
    y_sample, news = run_trunk(x_sample, p, st, page_table)
    return (y_prompt, y_sample,
            newp["mlstm_C"], newp["mlstm_n"], newp["mlstm_m"], newp["pool"], newp["mla"], newp["conv"], newp["ffn"],
            news["mlstm_C"], news["mlstm_n"], news["mlstm_m"], news["pool"], news["mla"], news["conv"], news["ffn"])
```

```python
import functools
import math

import jax
import jax.numpy as jnp
from jax import lax
from jax.experimental import pallas as pl
from jax.experimental.pallas import tpu as pltpu

F32 = jnp.float32
BF16 = jnp.bfloat16

D_MODEL = 1024
N_META = 16
EPS = 1e-6
M_HEADS = 4
M_DK = 128
M_DV = 256
POOL_WINDOWS = (2, 4, 8, 16)
POOL_GDIM = 256
POOL_HIST = 15
MLA_HEADS = 16
MLA_NOPE = 64
MLA_ROPE = 32
MLA_QK = 96
MLA_V = 64
MLA_Q_LORA = 512
MLA_KV_LORA = 256
MLA_CACHE_DIM = 288
ROPE_BASE = 10000.0
CONV_WIDTH = 31
CONV_HIST = 30
D_FF = 2816
FFN_HIST = 2

LANE = 128
FF_CHUNK = 256
N_FF_CHUNK = D_FF // FF_CHUNK
NEG = -1e30
LOG2E = 1.4426950408889634
VMEM_LIMIT = 56 * 1024 * 1024


def _dot(a, b):
    return jnp.dot(a, b, preferred_element_type=F32)


def _dot_nt(a, b):
    return lax.dot_general(a, b, (((1,), (1,)), ((), ())), preferred_element_type=F32)


def _dot_tn(a, b):
    return lax.dot_general(a, b, (((0,), (0,)), ((), ())), preferred_element_type=F32)


def _rms(xf, g):
    return xf * lax.rsqrt(jnp.mean(xf * xf, axis=-1, keepdims=True) + EPS) * g


def _sigmoid(x):
    return 1.0 / (1.0 + jnp.exp(-x))


def _valid_rows(i, tm, tps, pad):
    r = (i % tps) * tm + lax.broadcasted_iota(jnp.int32, (tm, 1), 0)
    return r >= pad


def _params(n_axes):
    return pltpu.CompilerParams(dimension_semantics=("arbitrary",) * n_axes,
                                vmem_limit_bytes=VMEM_LIMIT)


def _row_spec(tm, c):
    return pl.BlockSpec((tm, c), lambda i: (i, 0))


def _full_spec(shape):
    zeros = (0,) * len(shape)
    return pl.BlockSpec(shape, lambda *_: zeros)


def _sds(shape, dtype):
    return jax.ShapeDtypeStruct(shape, dtype)


def _m_in_kernel(x_ref, g_ref, wq, wk, wv, wo, wgh, wgl, bg,
                 q_o, k_o, v_o, o_o, gt_o, *, tps, pad):
    tm = x_ref.shape[0]
    hf = _rms(x_ref[...], g_ref[...])
    h = hf.astype(BF16)
    q_o[...] = (_dot(h, wq[...]) * (M_DK ** -0.5)).astype(BF16)
    k_o[...] = _dot(h, wk[...]).astype(BF16)
    v_o[...] = _dot(h, wv[...]).astype(BF16)
    o_o[...] = _sigmoid(_dot(h, wo[...])).astype(BF16)
    hl = (hf - h.astype(F32)).astype(BF16)
    gts = _dot(h, wgh[...]) + _dot(hl, wgh[...]) + _dot(h, wgl[...]) + bg[...]
    lane = lax.broadcasted_iota(jnp.int32, gts.shape, 1)
    lf = jnp.minimum(gts, 0.0) - jnp.log(1.0 + jnp.exp(-jnp.abs(gts)))
    out = jnp.where(lane < M_HEADS, gts, lf)
    if pad:
        valid = _valid_rows(pl.program_id(0), tm, tps, pad)
        out = jnp.where(valid, out, jnp.where(lane < M_HEADS, NEG, 0.0))
    gt_o[...] = out


def _m_in(x, g, w, *, tm, tps, pad):
    r = x.shape[0]
    kern = functools.partial(_m_in_kernel, tps=tps, pad=pad)
    return pl.pallas_call(
        kern, grid=(r // tm,),
        in_specs=[_row_spec(tm, D_MODEL), _full_spec((1, D_MODEL)),
                  _full_spec(w["wq"].shape), _full_spec(w["wk"].shape), _full_spec(w["wv"].shape),
                  _full_spec(w["wo"].shape), _full_spec(w["wgh"].shape), _full_spec(w["wgl"].shape),
                  _full_spec((1, LANE))],
        out_specs=[_row_spec(tm, 512), _row_spec(tm, 512), _row_spec(tm, 1024),
                   _row_spec(tm, 1024), _row_spec(tm, LANE)],
        out_shape=[_sds((r, 512), BF16), _sds((r, 512), BF16), _sds((r, 1024), BF16),
                   _sds((r, 1024), BF16), _sds((r, LANE), F32)],
        compiler_params=_params(1), name="mlstm_in",
    )(x, g, w["wq"], w["wk"], w["wv"], w["wo"], w["wgh"], w["wgl"], w["bg"])


def _split3(a):
    a1 = a.astype(BF16)
    r1 = a - a1.astype(F32)
    a2 = r1.astype(BF16)
    a3 = (r1 - a2.astype(F32)).astype(BF16)
    return a1, a2, a3


def _m_chunk_kernel(q_ref, k_ref, v_ref, gc_ref, gr_ref, c0_ref, n0_ref, m0_ref,
                    h_ref, c_o, n_o, m_o, c_s, n_s, m_s):
    c = pl.program_id(1)
    ln = q_ref.shape[0]

    @pl.when(c == 0)
    def _():
        c_s[...] = c0_ref[0]
        n_s[...] = n0_ref[0]
        m_s[...] = m0_ref[0]

    row = lax.broadcasted_iota(jnp.int32, (ln, ln), 0)
    col = lax.broadcasted_iota(jnp.int32, (ln, ln), 1)
    causal = col <= row
    tril = jnp.where(causal, 1.0, 0.0).astype(BF16)
    triu = jnp.where(row <= col, 1.0, 0.0).astype(BF16)
    gc = gc_ref[...]
    gr = gr_ref[0]
    c1, c2, c3 = _split3(gc)
    b_col = _dot(tril, c1) + _dot(tril, c2) + _dot(tril, c3)
    r1, r2, r3 = _split3(gr)
    b_row = _dot(r1, triu) + _dot(r2, triu) + _dot(r3, triu)

    for hd in range(M_HEADS):
        ig_c = gc[:, hd:hd + 1]
        b_c = b_col[:, M_HEADS + hd:M_HEADS + hd + 1]
        ig_r = gr[hd:hd + 1, :]
        b_r = b_row[M_HEADS + hd:M_HEADS + hd + 1, :]
        m0 = m_s[hd]
        c0 = c_s[hd]
        n0 = n_s[hd]
        dmat = jnp.where(causal, b_c - b_r + ig_r, NEG)
        inter = b_c + m0
        m_t = jnp.maximum(inter, jnp.max(dmat, axis=1, keepdims=True))
        w_intra = jnp.exp(dmat - m_t)
        w_inter = jnp.exp(inter - m_t)
        qh = q_ref[:, hd * M_DK:(hd + 1) * M_DK]
        kh = k_ref[:, hd * M_DK:(hd + 1) * M_DK]
        vh = v_ref[:, hd * M_DV:(hd + 1) * M_DV]
        s = _dot_nt(qh, kh) * w_intra
        num = _dot(s.astype(BF16), vh) + w_inter * _dot(qh, c0.astype(BF16))
        den = (jnp.sum(s, axis=1, keepdims=True)
               + w_inter * jnp.sum(qh.astype(F32) * n0, axis=1, keepdims=True))
        h_ref[:, hd * M_DV:(hd + 1) * M_DV] = num / jnp.maximum(jnp.abs(den), jnp.exp(-m_t))
        b_last = b_c[ln - 1:ln, :]
        m_new = m_t[ln - 1:ln, :]
        wk_c = jnp.exp(b_last - b_c + ig_c - m_new)
        decay = jnp.exp(b_last + m0 - m_new)
        vw = (vh.astype(F32) * wk_c).astype(BF16)
        c_s[hd] = decay * c0 + _dot_tn(kh, vw)
        n_s[hd] = decay * n0 + jnp.sum(kh.astype(F32) * wk_c, axis=0, keepdims=True)
        m_s[hd] = m_new

    @pl.when(c == pl.num_programs(1) - 1)
    def _():
        c_o[0] = c_s[...]
        n_o[0] = n_s[...]
        m_o[0] = m_s[...]


def _m_chunk(q, k, v, gcol, grow, c0, n0, m0, *, ln):
    n = c0.shape[0]
    t = q.shape[0] // n
    nc = t // ln
    rows = lambda w: pl.BlockSpec((ln, w), lambda i, c: (i * nc + c, 0))
    st_c = pl.BlockSpec((1, M_HEADS, M_DK, M_DV), lambda i, c: (i, 0, 0, 0))
    st_n = pl.BlockSpec((1, M_HEADS, 1, M_DK), lambda i, c: (i, 0, 0, 0))
    st_m = pl.BlockSpec((1, M_HEADS, 1, 1), lambda i, c: (i, 0, 0, 0))
    return pl.pallas_call(
        _m_chunk_kernel, grid=(n, nc),
        in_specs=[rows(512), rows(512), rows(1024), rows(LANE),
                  pl.BlockSpec((1, 8, ln), lambda i, c: (i, 0, c)), st_c, st_n, st_m],
        out_specs=[rows(1024), st_c, st_n, st_m],
        out_shape=[_sds((n * t, 1024), F32), _sds((n, M_HEADS, M_DK, M_DV), F32),
                   _sds((n, M_HEADS, 1, M_DK), F32), _sds((n, M_HEADS, 1, 1), F32)],
        scratch_shapes=[pltpu.VMEM((M_HEADS, M_DK, M_DV), F32), pltpu.VMEM((M_HEADS, 1, M_DK), F32),
                        pltpu.VMEM((M_HEADS, 1, 1), F32)],
        compiler_params=_params(2), name="mlstm_chunk",
    )(q, k, v, gcol, grow, c0, n0, m0)


def _m_out_kernel(hh_ref, o_ref, x_ref, g_ref, w_ref, y_ref, *, tps, pad):
    tm = x_ref.shape[0]
    parts = []
    for hd in range(M_HEADS):
        sl = slice(hd * M_DV, (hd + 1) * M_DV)
        hn = _rms(hh_ref[:, sl], g_ref[:, sl])
        parts.append((hn * o_ref[:, sl].astype(F32)).astype(BF16))
    y = x_ref[...] + _dot(jnp.concatenate(parts, axis=1), w_ref[...])
    if pad:
        y = jnp.where(_valid_rows(pl.program_id(0), tm, tps, pad), y, 0.0)
    y_ref[...] = y


def _m_out(hh, o, x, g, w, *, tm, tps, pad):
    r = x.shape[0]
    kern = functools.partial(_m_out_kernel, tps=tps, pad=pad)
    return pl.pallas_call(
        kern, grid=(r // tm,),
        in_specs=[_row_spec(tm, 1024), _row_spec(tm, 1024), _row_spec(tm, D_MODEL),
                  _full_spec((1, 1024)), _full_spec((1024, D_MODEL))],
        out_specs=_row_spec(tm, D_MODEL), out_shape=_sds((r, D_MODEL), F32),
        compiler_params=_params(1), name="mlstm_out",
    )(hh, o, x, g, w)


def _ffn_kernel(x_ref, g_ref, wg_ref, wu_ref, cw_ref, cb_ref, wd_ref,
                y_ref, tail_ref, carry_s, acc_s, *, tps, pad):
    i = pl.program_id(0)
    tm = x_ref.shape[0]
    x = x_ref[...]
    h = _rms(x, g_ref[...]).astype(BF16)

    @pl.when(i % tps == 0)
    def _():
        carry_s[...] = jnp.zeros_like(carry_s)

    acc_s[...] = jnp.zeros_like(acc_s)
    rowid = lax.broadcasted_iota(jnp.int32, (tm, 1), 0)

    def chunk(c, carry):
        g = _dot(h, wg_ref[c])
        u = _dot(h, wu_ref[c])
        prev = carry_s[c]
        g1 = jnp.where(rowid == 0, prev[7:8], pltpu.roll(g, 1, 0))
        g2 = jnp.where(rowid == 0, prev[6:7], jnp.where(rowid == 1, prev[7:8], pltpu.roll(g, 2, 0)))
        cw = cw_ref[c]
        gc = cw[0:1] * g2 + cw[1:2] * g1 + cw[2:3] * g + cb_ref[c]
        act = (gc * _sigmoid(gc) * u).astype(BF16)
        acc_s[...] += _dot(act, wd_ref[c])
        last = g[tm - 8:tm]
        carry_s[c] = last
        tail_ref[0, c] = last
        return carry

    lax.fori_loop(0, N_FF_CHUNK, chunk, 0)
    y = x + acc_s[...]
    if pad:
        y = jnp.where(_valid_rows(i, tm, tps, pad), y, 0.0)
    y_ref[...] = y


def _ffn(x, g, w, *, tm, tps, pad):
    r = x.shape[0]
    nt = r // tm
    kern = functools.partial(_ffn_kernel, tps=tps, pad=pad)
    return pl.pallas_call(
        kern, grid=(nt,),
        in_specs=[_row_spec(tm, D_MODEL), _full_spec((1, D_MODEL)),
                  _full_spec(w["wg"].shape), _full_spec(w["wu"].shape), _full_spec(w["cw"].shape),
                  _full_spec(w["cb"].shape), _full_spec(w["wd"].shape)],
        out_specs=[_row_spec(tm, D_MODEL),
                   pl.BlockSpec((1, N_FF_CHUNK, 8, FF_CHUNK), lambda i: (i, 0, 0, 0))],
        out_shape=[_sds((r, D_MODEL), F32), _sds((nt, N_FF_CHUNK, 8, FF_CHUNK), F32)],
        scratch_shapes=[pltpu.VMEM((N_FF_CHUNK, 8, FF_CHUNK), F32), pltpu.VMEM((tm, D_MODEL), F32)],
        compiler_params=_params(1), name="ffn",
    )(x, g, w["wg"], w["wu"], w["cw"], w["cb"], w["wd"])


def _ffn_s_kernel(x_ref, g_ref, hist_ref, wg_ref, wu_ref, cw_ref, cb_ref, wd_ref,
                  y_ref, tail_ref, acc_s, *, nb):
    c = pl.program_id(0)
    x = x_ref[...]
    h = _rms(x, g_ref[...]).astype(BF16)

    @pl.when(c == 0)
    def _():
        acc_s[...] = jnp.zeros_like(acc_s)

    g = _dot(h, wg_ref[0])
    u = _dot(h, wu_ref[0])
    hist = hist_ref[...]
    t = g.shape[0]
    ext0 = jnp.concatenate([hist, g[:t - 2 * nb]], axis=0)
    ext1 = jnp.concatenate([hist[nb:], g[:t - nb]], axis=0)
    cw = cw_ref[0]
    gc = cw[0:1] * ext0 + cw[1:2] * ext1 + cw[2:3] * g + cb_ref[0]
    act = (gc * _sigmoid(gc) * u).astype(BF16)
    acc_s[...] += _dot(act, wd_ref[0])
    tail_ref[...] = g[t - 2 * nb:]

    @pl.when(c == pl.num_programs(0) - 1)
    def _():
        y_ref[...] = x + acc_s[...]


def _ffn_s(x, g, hist, w, *, nb):
    r = x.shape[0]
    kern = functools.partial(_ffn_s_kernel, nb=nb)
    chunk3 = lambda a, b: pl.BlockSpec((1, a, b), lambda c: (c, 0, 0))
    return pl.pallas_call(
        kern, grid=(N_FF_CHUNK,),
        in_specs=[_full_spec((r, D_MODEL)), _full_spec((1, D_MODEL)),
                  pl.BlockSpec((2 * nb, FF_CHUNK), lambda c: (0, c)),
                  chunk3(D_MODEL, FF_CHUNK), chunk3(D_MODEL, FF_CHUNK), chunk3(8, FF_CHUNK),
                  chunk3(1, FF_CHUNK), chunk3(FF_CHUNK, D_MODEL)],
        out_specs=[_full_spec((r, D_MODEL)), pl.BlockSpec((2 * nb, FF_CHUNK), lambda c: (0, c))],
        out_shape=[_sds((r, D_MODEL), F32), _sds((2 * nb, D_FF), F32)],
        scratch_shapes=[pltpu.VMEM((r, D_MODEL), F32)],
        compiler_params=_params(1), name="ffn_sample",
    )(x, g, hist, w["wg"], w["wu"], w["cw"], w["cb"], w["wd"])


def _pool_kernel(x_ref, g_ref, pw_ref, ps_ref, y_ref, tail_ref, ext_s, *, tps, pad):
    i = pl.program_id(0)
    tm = x_ref.shape[0]
    x = x_ref[...]
    u = _rms(x, g_ref[...])
    hist = POOL_HIST + 1

    @pl.when(i % tps == 0)
    def _():
        ext_s[0:hist] = jnp.zeros((hist, D_MODEL), F32)

    @pl.when(i % tps != 0)
    def _():
        ext_s[0:hist] = ext_s[tm:tm + hist]

    ext_s[hist:hist + tm] = u
    tail_ref[0] = u[tm - hist:]
    pos = (i % tps) * tm + lax.broadcasted_iota(jnp.int32, (tm, 1), 0) - pad
    valid = pos >= 0
    for gi, w in enumerate(POOL_WINDOWS):
        sl = slice(gi * POOL_GDIM, (gi + 1) * POOL_GDIM)
        ug = u[:, sl]
        win = ug
        for j in range(1, w):
            win = win + ext_s[hist - j:hist - j + tm, sl]
        cnt = jnp.maximum(jnp.minimum(pos + 1, w), 1).astype(F32)
        d = (win / cnt - ug).astype(BF16)
        y = x[:, sl] + _dot(d, pw_ref[gi]) * ps_ref[:, sl]
        y_ref[:, sl] = jnp.where(valid, y, 0.0)


def _pool(x, g, pw, ps, *, tm, tps, pad):
    r = x.shape[0]
    nt = r // tm
    kern = functools.partial(_pool_kernel, tps=tps, pad=pad)
    return pl.pallas_call(
        kern, grid=(nt,),
        in_specs=[_row_spec(tm, D_MODEL), _full_spec((1, D_MODEL)),
                  _full_spec((4, POOL_GDIM, POOL_GDIM)), _full_spec((1, D_MODEL))],
        out_specs=[_row_spec(tm, D_MODEL), pl.BlockSpec((1, 16, D_MODEL), lambda i: (i, 0, 0))],
        out_shape=[_sds((r, D_MODEL), F32), _sds((nt, 16, D_MODEL), F32)],
        scratch_shapes=[pltpu.VMEM((tm + 16, D_MODEL), F32)],
        compiler_params=_params(1), name="pool",
    )(x, g, pw, ps)


def _pool_s_kernel(x_ref, g_ref, hist_ref, pw_ref, ps_ref, y_ref, u_ref, *, nb, past_len):
    x = x_ref[...]
    u = _rms(x, g_ref[...])
    u_ref[...] = u
    nt = x.shape[0] // nb
    for gi, w in enumerate(POOL_WINDOWS):
        sl = slice(gi * POOL_GDIM, (gi + 1) * POOL_GDIM)
        outs = []
        for t in range(nt):
            win = u[t * nb:(t + 1) * nb, sl]
            for j in range(1, w):
                src = t - j
                if src >= 0:
                    win = win + u[src * nb:(src + 1) * nb, sl]
                else:
                    hrow = POOL_HIST + src
                    win = win + hist_ref[hrow * nb:(hrow + 1) * nb, sl]
            cnt = float(min(past_len + t + 1, w))
            outs.append(win / cnt - u[t * nb:(t + 1) * nb, sl])
        d = jnp.concatenate(outs, axis=0).astype(BF16)
        y_ref[:, sl] = x[:, sl] + _dot(d, pw_ref[gi]) * ps_ref[:, sl]


def _pool_s(x, g, hist, pw, ps, *, nb, past_len):
    r = x.shape[0]
    kern = functools.partial(_pool_s_kernel, nb=nb, past_len=past_len)
    return pl.pallas_call(
        kern, grid=(1,),
        in_specs=[_full_spec((r, D_MODEL)), _full_spec((1, D_MODEL)), _full_spec(hist.shape),
                  _full_spec((4, POOL_GDIM, POOL_GDIM)), _full_spec((1, D_MODEL))],
        out_specs=[_full_spec((r, D_MODEL)), _full_spec((r, D_MODEL))],
        out_shape=[_sds((r, D_MODEL), F32), _sds((r, D_MODEL), F32)],
        compiler_params=_params(1), name="pool_sample",
    )(x, g, hist, pw, ps)


def _rope_norm(z, g, tc, ts1, ts2):
    zn = z * lax.rsqrt(jnp.sum(z * z, axis=-1, keepdims=True) * (1.0 / MLA_QK) + EPS) * g
    return zn * tc + pltpu.roll(zn, 16, 1) * ts1 + pltpu.roll(zn, LANE - 16, 1) * ts2


def _mla_in_kernel(x_ref, g_ref, wdq, qlg, wuq, qng, wlat, wkpe, kvg, wuk, wuv, kng,
                   tc_ref, ts1_ref, ts2_ref, qh_o, kh_o, vp_o, lat_o, kpe_o):
    h = _rms(x_ref[...], g_ref[...]).astype(BF16)
    cqn = _rms(_dot(h, wdq[...]), qlg[...]).astype(BF16)
    latn = _rms(_dot(h, wlat[...]), kvg[...])
    lat_o[...] = latn
    latb = latn.astype(BF16)
    kpe = _dot(h, wkpe[...])
    kpe_o[...] = kpe
    tc, ts1, ts2 = tc_ref[...], ts1_ref[...], ts2_ref[...]
    for hd in range(MLA_HEADS):
        qh_o[hd] = _rope_norm(_dot(cqn, wuq[hd]), qng[...], tc, ts1, ts2).astype(qh_o.dtype)
        kh_o[hd] = _rope_norm(_dot(latb, wuk[hd]) + kpe, kng[...], tc, ts1, ts2).astype(BF16)
    for j in range(MLA_HEADS // 2):
        vp_o[j] = _dot(latb, wuv[j]).astype(BF16)


def _mla_in(x, g, w, tabs, *, tm, qdtype):
    r = x.shape[0]
    heads = lambda n: pl.BlockSpec((n, tm, LANE), lambda i: (0, i, 0))
    return pl.pallas_call(
        _mla_in_kernel, grid=(r // tm,),
        in_specs=[_row_spec(tm, D_MODEL), _full_spec((1, D_MODEL)),
                  _full_spec(w["wdq"].shape), _full_spec((1, MLA_Q_LORA)), _full_spec(w["wuq"].shape),
                  _full_spec((1, LANE)), _full_spec(w["wlat"].shape), _full_spec(w["wkpe"].shape),
                  _full_spec((1, MLA_KV_LORA)), _full_spec(w["wuk"].shape), _full_spec(w["wuv"].shape),
                  _full_spec((1, LANE)), _row_spec(tm, LANE), _row_spec(tm, LANE), _row_spec(tm, LANE)],
        out_specs=[heads(MLA_HEADS), heads(MLA_HEADS), heads(MLA_HEADS // 2),
                   _row_spec(tm, MLA_KV_LORA), _row_spec(tm, LANE)],
        out_shape=[_sds((MLA_HEADS, r, LANE), qdtype), _sds((MLA_HEADS, r, LANE), BF16),
                   _sds((MLA_HEADS // 2, r, LANE), BF16), _sds((r, MLA_KV_LORA), F32),
                   _sds((r, LANE), F32)],
        compiler_params=_params(1), name="mla_in",
    )(x, g, w["wdq"], w["qlg"], w["wuq"], w["qng"], w["wlat"], w["wkpe"], w["kvg"],
      w["wuk"], w["wuv"], w["kng"], *tabs)


def _attn_kernel(q_ref, k_ref, v_ref, o_ref, acc_s, m_s, l_s, *, pad, tk):
    qi = pl.program_id(2)
    tq = q_ref.shape[1]
    for hh in range(2):
        m_s[hh] = jnp.full((tq, 1), NEG, F32)
        l_s[hh] = jnp.zeros((tq, 1), F32)
        acc_s[hh] = jnp.zeros((tq, LANE), F32)

    def block(ki, masked):
        ks = pl.multiple_of(ki * tk, tk)
        vblk = v_ref[0, pl.ds(ks, tk), :]
        for hh in range(2):
            s = _dot_nt(q_ref[hh], k_ref[hh, pl.ds(ks, tk), :])
            if masked:
                qpos = qi * tq + lax.broadcasted_iota(jnp.int32, (tq, tk), 0)
                kpos = ki * tk + lax.broadcasted_iota(jnp.int32, (tq, tk), 1)
                s = jnp.where(kpos <= qpos, jnp.where(kpos >= pad, s, NEG), NEG)
            m_old = m_s[hh]
            m_new = jnp.maximum(m_old, jnp.max(s, axis=1, keepdims=True))
            p = jnp.exp2(s - m_new)
            alpha = jnp.exp2(m_old - m_new)
            l_s[hh] = alpha * l_s[hh] + jnp.sum(p, axis=1, keepdims=True)
            acc_s[hh] = alpha * acc_s[hh] + _dot(p.astype(BF16), vblk)
            m_s[hh] = m_new

    block(0, True)

    def body(ki, carry):
        block(ki, False)
        return carry

    lax.fori_loop(1, qi, body, 0)

    @pl.when(qi > 0)
    def _():
        block(qi, True)

    lane = lax.broadcasted_iota(jnp.int32, (tq, LANE), 1)
    o_ref[...] = jnp.where(lane < MLA_V, acc_s[0] / l_s[0], acc_s[1] / l_s[1]).astype(BF16)


def _attn(qh, kh, vp, *, n, tp, tq, pad):
    nq = tp // tq
    kern = functools.partial(_attn_kernel, pad=pad, tk=tq)
    return pl.pallas_call(
        kern, grid=(n, MLA_HEADS // 2, nq),
        in_specs=[pl.BlockSpec((2, tq, LANE), lambda b, hp, qi: (hp, b * nq + qi, 0)),
                  pl.BlockSpec((2, tp, LANE), lambda b, hp, qi: (hp, b, 0)),
                  pl.BlockSpec((1, tp, LANE), lambda b, hp, qi: (hp, b, 0))],
        out_specs=pl.BlockSpec((tq, LANE), lambda b, hp, qi: (b * nq + qi, hp)),
        out_shape=_sds((n * tp, MLA_HEADS * MLA_V), BF16),
        scratch_shapes=[pltpu.VMEM((2, tq, LANE), F32), pltpu.VMEM((2, tq, 1), F32),
                        pltpu.VMEM((2, tq, 1), F32)],
        compiler_params=_params(3), name="attn",
    )(qh, kh, vp)


def _proj_out_kernel(o_ref, x_ref, w_ref, y_ref, *, tps, pad):
    tm = x_ref.shape[0]
    y = x_ref[...] + _dot(o_ref[...], w_ref[...])
    if pad:
        y = jnp.where(_valid_rows(pl.program_id(0), tm, tps, pad), y, 0.0)
    y_ref[...] = y


def _proj_out(o, x, w, *, tm, tps, pad):
    r = x.shape[0]
    kern = functools.partial(_proj_out_kernel, tps=tps, pad=pad)
    return pl.pallas_call(
        kern, grid=(r // tm,),
        in_specs=[_row_spec(tm, 1024), _row_spec(tm, D_MODEL), _full_spec((1024, D_MODEL))],
        out_specs=_row_spec(tm, D_MODEL), out_shape=_sds((r, D_MODEL), F32),
        compiler_params=_params(1), name="mla_out",
    )(o, x, w)


def _dec_prep_kernel(qh_ref, kng, wukt, qabs_o, ab_o):
    g = kng[...]
    g1 = g[:, MLA_NOPE:MLA_NOPE + 16]
    g2 = g[:, MLA_NOPE + 16:MLA_QK]
    for hd in range(MLA_HEADS):
        qh = qh_ref[hd]
        qabs_o[hd] = _dot((qh * g).astype(BF16), wukt[hd]).astype(BF16)
        q1 = qh[:, MLA_NOPE:MLA_NOPE + 16]
        q2 = qh[:, MLA_NOPE + 16:MLA_QK]
        ab_o[hd] = jnp.concatenate([g1 * q1, g2 * q2, g1 * q2, -(g2 * q1)], axis=1).astype(BF16)


def _dec_prep(qh, kng, wukt):
    r = qh.shape[1]
    return pl.pallas_call(
        _dec_prep_kernel, grid=(1,),
        in_specs=[_full_spec(qh.shape), _full_spec((1, LANE)), _full_spec(wukt.shape)],
        out_specs=[_full_spec((MLA_HEADS, r, MLA_KV_LORA)), _full_spec((MLA_HEADS, r, 64))],
        out_shape=[_sds((MLA_HEADS, r, MLA_KV_LORA), BF16), _sds((MLA_HEADS, r, 64), BF16)],
        compiler_params=_params(1), name="decode_prep",
    )(qh, kng, wukt)


def _decode_kernel(pt_ref, new_ref, qm_ref, ab_ref, wukp_ref, cs_ref, *rest,
                   n_pages, page, pages_per_chunk, n_new):
    del pt_ref
    page_refs = rest[:n_pages]
    acc_o, l_o, s_scr, lat_scr = rest[n_pages:]
    qm = qm_ref[0]
    ab = ab_ref[0]
    wukp = wukp_ref[...]
    ck = pages_per_chunk * page
    n_chunk = n_pages // pages_per_chunk
    past = n_pages * page
    nn = new_ref.shape[1]

    def scores(lat, kpe, cs):
        latb = lat.astype(BF16)
        kcs = (jnp.concatenate([kpe, kpe], axis=1) * cs).astype(BF16)
        raw = _dot(latb, qm) + _dot(kcs, ab)
        kn = _dot(latb, wukp)
        sq = kn * kn
        ss = sq[:, 0:LANE]
        for j in range(1, (MLA_HEADS * MLA_NOPE) // LANE):
            ss = ss + sq[:, j * LANE:(j + 1) * LANE]
        ss = ss + pltpu.roll(ss, 16, 1)
        ss = ss + pltpu.roll(ss, 32, 1)
        ss = ss + pltpu.roll(ss, 64, 1)
        ss = ss + jnp.sum(kpe * kpe, axis=1, keepdims=True)
        return latb, lax.rsqrt(ss * (1.0 / MLA_QK) + EPS) * raw

    m = jnp.full((1, LANE), NEG, F32)
    for c in range(n_chunk):
        blks = [page_refs[c * pages_per_chunk + p][0] for p in range(pages_per_chunk)]
        lat = jnp.concatenate([b[:, :MLA_KV_LORA] for b in blks], axis=0)
        kpe = jnp.concatenate([b[:, MLA_KV_LORA:] for b in blks], axis=0)
        latb, s = scores(lat, kpe, cs_ref[c * ck:(c + 1) * ck, :])
        s_scr[c * ck:(c + 1) * ck, :] = s
        lat_scr[c * ck:(c + 1) * ck, :] = latb
        m = jnp.maximum(m, jnp.max(s, axis=0, keepdims=True))
    blk = new_ref[0]
    latb, s = scores(blk[:, :MLA_KV_LORA], blk[:, MLA_KV_LORA:], cs_ref[past:past + nn, :])
    krow = lax.broadcasted_iota(jnp.int32, (nn, LANE), 0)
    tok = lax.broadcasted_iota(jnp.int32, (nn, LANE), 1) // MLA_HEADS
    s = jnp.where((krow <= tok) | (tok >= n_new), s, NEG)
    s_scr[past:past + nn, :] = s
    lat_scr[past:past + nn, :] = latb
    m = jnp.maximum(m, jnp.max(s, axis=0, keepdims=True))

    l = jnp.zeros((1, LANE), F32)
    acc = jnp.zeros((LANE, MLA_KV_LORA), F32)
    bounds = [(c * ck, ck) for c in range(n_chunk)] + [(past, nn)]
    for start, size in bounds:
        p = jnp.exp2(s_scr[start:start + size, :] - m)
        l = l + jnp.sum(p, axis=0, keepdims=True)
        acc = acc + _dot_tn(p.astype(BF16), lat_scr[start:start + size, :])
    acc_o[0] = acc
    l_o[0] = jnp.broadcast_to(l, (8, LANE))


def _decode(page_table, cache, new_rows, qm, ab, wukp, cs, *, n_new, pages_per_chunk=8):
    nb, n_pages = page_table.shape
    page = cache.shape[1]
    nn = new_rows.shape[1]
    kern = functools.partial(_decode_kernel, n_pages=n_pages, page=page,
                             pages_per_chunk=pages_per_chunk, n_new=n_new)
    per_seq = lambda shape: pl.BlockSpec((1,) + shape, lambda b, pt: (b, 0, 0))
    page_specs = [pl.BlockSpec((1, page, MLA_CACHE_DIM),
                               lambda b, pt, j=j: (pt[b * n_pages + j], 0, 0))
                  for j in range(n_pages)]
    total = n_pages * page + nn
    grid_spec = pltpu.PrefetchScalarGridSpec(
        num_scalar_prefetch=1, grid=(nb,),
        in_specs=[per_seq((nn, MLA_CACHE_DIM)), per_seq((MLA_KV_LORA, LANE)), per_seq((64, LANE)),
                  pl.BlockSpec(wukp.shape, lambda b, pt: (0, 0)),
                  pl.BlockSpec(cs.shape, lambda b, pt: (0, 0))] + page_specs,
        out_specs=[per_seq((LANE, MLA_KV_LORA)), per_seq((8, LANE))],
        scratch_shapes=[pltpu.VMEM((total, LANE), F32), pltpu.VMEM((total, MLA_KV_LORA), BF16)])
    return pl.pallas_call(
        kern, grid_spec=grid_spec,
        out_shape=[_sds((nb, LANE, MLA_KV_LORA), F32), _sds((nb, 8, LANE), F32)],
        compiler_params=_params(1), name="decode",
    )(page_table.reshape(-1), new_rows, qm, ab, wukp, cs, *([cache] * n_pages))


def _dec_out_kernel(acc_ref, l_ref, wuv, wo, x_ref, y_ref):
    parts = []
    for hd in range(MLA_HEADS):
        on = (acc_ref[hd] / l_ref[hd]).astype(BF16)
        parts.append(_dot(on, wuv[hd]))
    o = jnp.concatenate(parts, axis=1).astype(BF16)
    y_ref[...] = x_ref[...] + _dot(o, wo[...])


def _dec_out(acc, l, wuv, wo, x):
    r = x.shape[0]
    return pl.pallas_call(
        _dec_out_kernel, grid=(1,),
        in_specs=[_full_spec(acc.shape), _full_spec(l.shape), _full_spec(wuv.shape),
                  _full_spec(wo.shape), _full_spec((r, D_MODEL))],
        out_specs=_full_spec((r, D_MODEL)), out_shape=_sds((r, D_MODEL), F32),
        compiler_params=_params(1), name="decode_out",
    )(acc, l, wuv, wo, x)


def _conv_in_kernel(x_ref, g_ref, w_ref, b_ref, gl_ref, *, tps, pad):
    tm = x_ref.shape[0]
    h = _rms(x_ref[...], g_ref[...]).astype(BF16)
    a = _dot(h, w_ref[...]) + b_ref[...]
    gl = a[:, :D_MODEL] * _sigmoid(a[:, D_MODEL:])
    if pad:
        gl = jnp.where(_valid_rows(pl.program_id(0), tm, tps, pad), gl, 0.0)
    gl_ref[...] = gl


def _conv_in(x, g, w, b, *, tm, tps, pad):
    r = x.shape[0]
    kern = functools.partial(_conv_in_kernel, tps=tps, pad=pad)
    return pl.pallas_call(
        kern, grid=(r // tm,),
        in_specs=[_row_spec(tm, D_MODEL), _full_spec((1, D_MODEL)),
                  _full_spec((D_MODEL, 2 * D_MODEL)), _full_spec((1, 2 * D_MODEL))],
        out_specs=_row_spec(tm, D_MODEL), out_shape=_sds((r, D_MODEL), F32),
        compiler_params=_params(1), name="conv_in",
    )(x, g, w, b)


def _ln_silu_proj(c, x, lng, lnb, w2, b2):
    mu = jnp.mean(c, axis=-1, keepdims=True)
    cc = c - mu
    var = jnp.mean(cc * cc, axis=-1, keepdims=True)
    y = cc * lax.rsqrt(var + EPS) * lng + lnb
    a = (y * _sigmoid(y)).astype(BF16)
    return x + _dot(a, w2) + b2


def _conv_out_kernel(gl_ref, x_ref, wdw, bdw, lng, lnb, w2, b2, y_ref, ext_s, *, tps, pad):
    i = pl.program_id(0)
    tm = x_ref.shape[0]
    hist = 32

    @pl.when(i % tps == 0)
    def _():
        ext_s[0:hist] = jnp.zeros((hist, D_MODEL), F32)

    @pl.when(i % tps != 0)
    def _():
        ext_s[0:hist] = ext_s[tm:tm + hist]

    ext_s[hist:hist + tm] = gl_ref[...]
    off = hist - CONV_HIST
    acc = wdw[0:1, :] * ext_s[off:off + tm, :]
    for j in range(1, CONV_WIDTH):
        acc = acc + wdw[j:j + 1, :] * ext_s[off + j:off + j + tm, :]
    y = _ln_silu_proj(acc + bdw[...], x_ref[...], lng[...], lnb[...], w2[...], b2[...])
    if pad:
        y = jnp.where(_valid_rows(i, tm, tps, pad), y, 0.0)
    y_ref[...] = y


def _conv_out(gl, x, w, *, tm, tps, pad):
    r = x.shape[0]
    kern = functools.partial(_conv_out_kernel, tps=tps, pad=pad)
    vec = _full_spec((1, D_MODEL))
    return pl.pallas_call(
        kern, grid=(r // tm,),
        in_specs=[_row_spec(tm, D_MODEL), _row_spec(tm, D_MODEL), _full_spec((32, D_MODEL)),
                  vec, vec, vec, _full_spec((D_MODEL, D_MODEL)), vec],
        out_specs=_row_spec(tm, D_MODEL), out_shape=_sds((r, D_MODEL), F32),
        scratch_shapes=[pltpu.VMEM((tm + 32, D_MODEL), F32)],
        compiler_params=_params(1), name="conv_out",
    )(gl, x, w["wdw"], w["bdw"], w["lng"], w["lnb"], w["w2"], w["b2"])


def _conv_out_s_kernel(hist_ref, gl_ref, x_ref, wdw, bdw, lng, lnb, w2, b2, y_ref, *, nb):
    nt = x_ref.shape[0] // nb
    outs = []
    for t in range(nt):
        acc = None
        for j in range(CONV_WIDTH):
            src = t + j
            if src < CONV_HIST:
                slab = hist_ref[src * nb:(src + 1) * nb, :]
            else:
                slab = gl_ref[(src - CONV_HIST) * nb:(src - CONV_HIST + 1) * nb, :]
            term = wdw[j:j + 1, :] * slab
            acc = term if acc is None else acc + term
        outs.append(acc)
    c = jnp.concatenate(outs, axis=0) + bdw[...]
    y_ref[...] = _ln_silu_proj(c, x_ref[...], lng[...], lnb[...], w2[...], b2[...])


def _conv_out_s(hist, gl, x, w, *, nb):
    r = x.shape[0]
    kern = functools.partial(_conv_out_s_kernel, nb=nb)
    vec = _full_spec((1, D_MODEL))
    return pl.pallas_call(
        kern, grid=(1,),
        in_specs=[_full_spec(hist.shape), _full_spec((r, D_MODEL)), _full_spec((r, D_MODEL)),
                  _full_spec((32, D_MODEL)), vec, vec, vec, _full_spec((D_MODEL, D_MODEL)), vec],
        out_specs=_full_spec((r, D_MODEL)), out_shape=_sds((r, D_MODEL), F32),
        compiler_params=_params(1), name="conv_out_sample",
    )(hist, gl, x, w["wdw"], w["bdw"], w["lng"], w["lnb"], w["w2"], w["b2"])


def _pad_lanes(a, width=LANE):
    return jnp.pad(a, [(0, 0)] * (a.ndim - 1) + [(0, width - a.shape[-1])])


def _row(a):
    return a.reshape(1, -1).astype(F32)


def _prep_mlstm(m_w_in, m_b_gates, m_head_norm_g, m_w_out):
    i0 = M_HEADS * M_DK
    i1 = 2 * i0
    i2 = i1 + M_HEADS * M_DV
    i3 = i2 + M_HEADS * M_DV
    wgate = _pad_lanes(m_w_in[:, i3:])
    wgh = wgate.astype(BF16)
    wgl = (wgate - wgh.astype(F32)).astype(BF16)
    return dict(wq=m_w_in[:, :i0].astype(BF16), wk=m_w_in[:, i0:i1].astype(BF16),
                wv=m_w_in[:, i1:i2].astype(BF16), wo=m_w_in[:, i2:i3].astype(BF16),
                wgh=wgh, wgl=wgl, bg=_pad_lanes(_row(m_b_gates)),
                hg=_row(m_head_norm_g), wout=m_w_out.astype(BF16))


def _prep_ffn(w_up, conv_w, conv_b, w_down):
    chunks = lambda a: a.reshape(D_MODEL, N_FF_CHUNK, FF_CHUNK).transpose(1, 0, 2).astype(BF16)
    cw = jnp.pad(conv_w, ((0, 8 - conv_w.shape[0]), (0, 0)))
    return dict(wg=chunks(w_up[:, :D_FF]), wu=chunks(w_up[:, D_FF:]),
                cw=cw.reshape(8, N_FF_CHUNK, FF_CHUNK).transpose(1, 0, 2).astype(F32),
                cb=conv_b.reshape(N_FF_CHUNK, 1, FF_CHUNK).astype(F32),
                wd=w_down.reshape(N_FF_CHUNK, FF_CHUNK, D_MODEL).astype(BF16))


def _prep_mla(mla_w_dq, mla_q_lora_g, mla_w_uq, mla_w_dkv, mla_kv_lora_g, mla_w_ukv,
              mla_q_norm_g, mla_k_norm_g, mla_w_o):
    wuq = _pad_lanes(mla_w_uq.reshape(MLA_Q_LORA, MLA_HEADS, MLA_QK)).transpose(1, 0, 2)
    ukv = mla_w_ukv.reshape(MLA_KV_LORA, MLA_HEADS, MLA_NOPE + MLA_V)
    uk = ukv[:, :, :MLA_NOPE]
    uv = ukv[:, :, MLA_NOPE:]
    wkpe = jnp.zeros((D_MODEL, LANE), F32).at[:, MLA_NOPE:MLA_QK].set(mla_w_dkv[:, MLA_KV_LORA:])
    qscale = (MLA_QK ** -0.5) * LOG2E
    return dict(
        wdq=mla_w_dq.astype(BF16), qlg=_row(mla_q_lora_g), wuq=wuq.astype(BF16),
        qng=_pad_lanes(_row(mla_q_norm_g)) * qscale,
        wlat=mla_w_dkv[:, :MLA_KV_LORA].astype(BF16), wkpe=wkpe.astype(BF16),
        kvg=_row(mla_kv_lora_g), wuk=_pad_lanes(uk).transpose(1, 0, 2).astype(BF16),
        wuv=uv.reshape(MLA_KV_LORA, MLA_HEADS // 2, 2 * MLA_V).transpose(1, 0, 2).astype(BF16),
        kng=_pad_lanes(_row(mla_k_norm_g)),
        wukt=jnp.pad(uk.transpose(1, 2, 0), ((0, 0), (0, LANE - MLA_NOPE), (0, 0))).astype(BF16),
        wukp=uk.transpose(0, 2, 1).reshape(MLA_KV_LORA, MLA_HEADS * MLA_NOPE).astype(BF16),
        wuv_h=uv.transpose(1, 0, 2).astype(BF16), wo=mla_w_o.astype(BF16))


def _rope_tables(pos):
    half = MLA_ROPE // 2
    inv = ROPE_BASE ** (-jnp.arange(half, dtype=F32) / half)
    ang = pos.astype(F32)[:, None] * inv[None, :]
    cos, sin = jnp.cos(ang), jnp.sin(ang)
    r = pos.shape[0]
    ones = jnp.ones((r, MLA_NOPE), F32)
    z16 = jnp.zeros((r, half), F32)
    z32 = jnp.zeros((r, LANE - MLA_QK), F32)
    z64 = jnp.zeros((r, MLA_NOPE), F32)
    tc = jnp.concatenate([ones, cos, cos, z32], axis=1)
    ts1 = jnp.concatenate([z64, z16, sin, z32], axis=1)
    ts2 = jnp.concatenate([z64, -sin, z16, z32], axis=1)
    cs = jnp.concatenate([cos, cos, sin, sin], axis=1)
    return (tc, ts1, ts2), cs


def _prompt_trunk(x_prompt, meta_tokens, norm_mix_g, norm_ffn_g, wm, wffn, pool_w, pool_scale,
                  wmla, wconv):
    n, seq, _ = x_prompt.shape
    t_real = seq + N_META
    tm = 768
    tp = -(-t_real // tm) * tm
    pad = tp - t_real
    tps = tp // tm
    meta = jnp.broadcast_to(meta_tokens[None].astype(F32), (n, N_META, D_MODEL))
    x = jnp.concatenate([jnp.zeros((n, pad, D_MODEL), F32), meta, x_prompt], axis=1)
    x = x.reshape(n * tp, D_MODEL)
    kw = dict(tm=tm, tps=tps, pad=pad)
    st = {}
    ffn_hist = []

    def ffn(x, layer):
        y, tail = _ffn(x, _row(norm_ffn_g[layer]), wffn[layer], **kw)
        tail = tail.reshape(n, tps, N_FF_CHUNK, 8, FF_CHUNK)[:, -1, :, 8 - FFN_HIST:, :]
        ffn_hist.append(tail.transpose(0, 2, 1, 3).reshape(n, FFN_HIST, D_FF))
        return y

    q, k, v, o, gt = _m_in(x, _row(norm_mix_g[0]), wm, **kw)
    grow = gt[:, :8].reshape(n, tp, 8).transpose(0, 2, 1)
    zc = jnp.zeros((n, M_HEADS, M_DK, M_DV), F32)
    zn = jnp.zeros((n, M_HEADS, 1, M_DK), F32)
    zm = jnp.zeros((n, M_HEADS, 1, 1), F32)
    hh, c_new, n_new, m_new = _m_chunk(q, k, v, gt, grow, zc, zn, zm, ln=tm)
    st["mlstm_C"] = c_new
    st["mlstm_n"] = n_new.reshape(n, M_HEADS, M_DK)
    st["mlstm_m"] = m_new.reshape(n, M_HEADS)
    x = _m_out(hh, o, x, wm["hg"], wm["wout"], **kw)
    x = ffn(x, 0)

    x, tail = _pool(x, _row(norm_mix_g[1]), pool_w.astype(BF16), _row(pool_scale), **kw)
    st["pool"] = tail.reshape(n, tps, 16, D_MODEL)[:, -1, 16 - POOL_HIST:, :]
    x = ffn(x, 1)

    pos = jnp.maximum(jnp.arange(tp, dtype=jnp.int32) - pad, 0)
    tabs, _ = _rope_tables(jnp.tile(pos, n))
    qh, kh, vp, lat, kpe = _mla_in(x, _row(norm_mix_g[2]), wmla, tabs, tm=tm, qdtype=BF16)
    rows = jnp.concatenate([lat, kpe[:, MLA_NOPE:MLA_QK]], axis=1)
    st["mla"] = rows.reshape(n, tp, MLA_CACHE_DIM)[:, pad:, :]
    o = _attn(qh, kh, vp, n=n, tp=tp, tq=tm, pad=pad)
    x = _proj_out(o, x, wmla["wo"], **kw)
    x = ffn(x, 2)

    gl = _conv_in(x, _row(norm_mix_g[3]), wconv["w1"], wconv["b1"], **kw)
    st["conv"] = gl.reshape(n, tp, D_MODEL)[:, tp - CONV_HIST:, :]
    x = _conv_out(gl, x, wconv, **kw)
    x = ffn(x, 3)

    st["ffn"] = jnp.stack(ffn_hist, axis=0)
    y = x.reshape(n, tp, D_MODEL)[:, pad + N_META:, :]
    return y, st


def _sample_trunk(x_sample, state_mlstm_C, state_mlstm_n, state_mlstm_m, state_pool, cache_mla,
                  page_table, state_conv, state_ffn, norm_mix_g, norm_ffn_g, wm, wffn, pool_w,
                  pool_scale, wmla, wconv):
    nb, nt, _ = x_sample.shape
    r = nb * nt
    past_len = page_table.shape[1] * cache_mla.shape[1]
    tmaj = lambda a: a.transpose(1, 0, 2).reshape(a.shape[1] * nb, a.shape[2])
    smaj = lambda a: a.reshape(-1, nb, a.shape[-1]).transpose(1, 0, 2)
    x = tmaj(x_sample.astype(F32))
    kw = dict(tm=r, tps=1, pad=0)
    st = {}
    ffn_hist = []

    def ffn(x, layer):
        y, tail = _ffn_s(x, _row(norm_ffn_g[layer]), tmaj(state_ffn[layer]), wffn[layer], nb=nb)
        ffn_hist.append(smaj(tail))
        return y

    ln = 16
    q, k, v, o, gt = _m_in(x, _row(norm_mix_g[0]), wm, **kw)

    def seq_rows(a, fill=None):
        a = smaj(a)
        if fill is None:
            a = jnp.pad(a, ((0, 0), (0, ln - nt), (0, 0)))
        else:
            a = jnp.concatenate([a, jnp.broadcast_to(fill, (nb, ln - nt, a.shape[2]))], axis=1)
        return a.reshape(nb * ln, a.shape[2])

    lane = jnp.arange(LANE)
    neutral = jnp.where(lane < M_HEADS, NEG, 0.0).astype(F32)
    gts = seq_rows(gt, fill=neutral)
    grow = gts[:, :8].reshape(nb, ln, 8).transpose(0, 2, 1)
    hh, c_new, n_new, m_new = _m_chunk(
        seq_rows(q), seq_rows(k), seq_rows(v), gts, grow, state_mlstm_C.astype(F32),
        state_mlstm_n.astype(F32).reshape(nb, M_HEADS, 1, M_DK),
        state_mlstm_m.astype(F32).reshape(nb, M_HEADS, 1, 1), ln=ln)
    st["mlstm_C"] = c_new
    st["mlstm_n"] = n_new.reshape(nb, M_HEADS, M_DK)
    st["mlstm_m"] = m_new.reshape(nb, M_HEADS)
    hh = tmaj(hh.reshape(nb, ln, M_HEADS * M_DV)[:, :nt, :])
    x = _m_out(hh, o, x, wm["hg"], wm["wout"], **kw)
    x = ffn(x, 0)

    x, u = _pool_s(x, _row(norm_mix_g[1]), tmaj(state_pool.astype(F32)), pool_w.astype(BF16),
                   _row(pool_scale), nb=nb, past_len=past_len)
    st["pool"] = jnp.concatenate([state_pool.astype(F32), smaj(u)], axis=1)[:, -POOL_HIST:, :]
    x = ffn(x, 1)

    n_new_pad = 16
    pos_new = past_len + jnp.repeat(jnp.arange(nt, dtype=jnp.int32), nb)
    tabs, _ = _rope_tables(pos_new)
    _, cs = _rope_tables(jnp.arange(past_len + n_new_pad, dtype=jnp.int32))
    qh, _, _, lat, kpe = _mla_in(x, _row(norm_mix_g[2]), wmla, tabs, tm=r, qdtype=F32)
    rows = smaj(jnp.concatenate([lat, kpe[:, MLA_NOPE:MLA_QK]], axis=1))
    st["mla"] = rows
    qabs, ab = _dec_prep(qh, wmla["kng"], wmla["wukt"])
    cols = lambda a: _pad_lanes(a.reshape(MLA_HEADS, nt, nb, a.shape[-1]).transpose(2, 3, 1, 0)
                                .reshape(nb, a.shape[-1], nt * MLA_HEADS))
    new_rows = jnp.pad(rows, ((0, 0), (0, n_new_pad - nt), (0, 0)))
    acc, l = _decode(page_table, cache_mla.astype(F32), new_rows, cols(qabs), cols(ab),
                     wmla["wukp"], cs, n_new=nt)
    acc = acc[:, :nt * MLA_HEADS, :].reshape(nb, nt, MLA_HEADS, MLA_KV_LORA)
    acc = acc.transpose(2, 1, 0, 3).reshape(MLA_HEADS, r, MLA_KV_LORA)
    l = l[:, 0, :nt * MLA_HEADS].reshape(nb, nt, MLA_HEADS).transpose(2, 1, 0).reshape(MLA_HEADS, r, 1)
    x = _dec_out(acc, l, wmla["wuv_h"], wmla["wo"], x)
    x = ffn(x, 2)

    gl = _conv_in(x, _row(norm_mix_g[3]), wconv["w1"], wconv["b1"], **kw)
    st["conv"] = jnp.concatenate([state_conv.astype(F32), smaj(gl)], axis=1)[:, -CONV_HIST:, :]
    x = _conv_out_s(tmaj(state_conv.astype(F32)), gl, x, wconv, nb=nb)
    x = ffn(x, 3)

    st["ffn"] = jnp.stack(ffn_hist, axis=0)
    return smaj(x), st


def kernel(x_prompt, x_sample, state_mlstm_C, state_mlstm_n, state_mlstm_m, state_pool, cache_mla, page_table, state_conv, state_ffn, meta_tokens, norm_mix_g, norm_ffn_g, m_w_in, m_b_gates, m_head_norm_g, m_w_out, pool_w, pool_scale, mla_w_dq, mla_q_lora_g, mla_w_uq, mla_w_dkv, mla_kv_lora_g, mla_w_ukv, mla_q_norm_g, mla_k_norm_g, mla_w_o, conv_w_pw1, conv_b_pw1, conv_w_dw, conv_b_dw, conv_ln_g, conv_ln_b, conv_w_pw2, conv_b_pw2, ffn_w_up, ffn_conv_w, ffn_conv_b, ffn_w_down):
    depth = ffn_w_up.shape[0]
    wm = _prep_mlstm(m_w_in, m_b_gates, m_head_norm_g, m_w_out)
    wffn = [_prep_ffn(ffn_w_up[i], ffn_conv_w[i], ffn_conv_b[i], ffn_w_down[i]) for i in range(depth)]
    wmla = _prep_mla(mla_w_dq, mla_q_lora_g, mla_w_uq, mla_w_dkv, mla_kv_lora_g, mla_w_ukv,
                     mla_q_norm_g, mla_k_norm_g, mla_w_o)
    wconv = dict(w1=conv_w_pw1.astype(BF16), b1=_row(conv_b_pw1),
                 wdw=jnp.pad(conv_w_dw, ((0, 32 - CONV_WIDTH), (0, 0))).astype(F32),
                 bdw=_row(conv_b_dw), lng=_row(conv_ln_g), lnb=_row(conv_ln_b),
                 w2=conv_w_pw2.astype(BF16), b2=_row(conv_b_pw2))
    yp, sp = _prompt_trunk(x_prompt, meta_tokens, norm_mix_g, norm_ffn_g, wm, wffn, pool_w,
                           pool_scale, wmla, wconv)
    ys, ss = _sample_trunk(x_sample, state_mlstm_C, state_mlstm_n, state_mlstm_m, state_pool,
                           cache_mla, page_table, state_conv, state_ffn, norm_mix_g, norm_ffn_g,
                           wm, wffn, pool_w, pool_scale, wmla, wconv)
    names = ("mlstm_C", "mlstm_n", "mlstm_m", "pool", "mla", "conv", "ffn")
    return (yp, ys) + tuple(sp[k] for k in names) + tuple(ss[k] for k in names)
```

```python
import functools
import math

import jax
import jax.numpy as jnp
from jax import lax
from jax.experimental import pallas as pl
from jax.experimental.pallas import tpu as pltpu

F32 = jnp.float32
BF16 = jnp.bfloat16

D_MODEL = 1024
N_META = 16
EPS = 1e-6
M_HEADS = 4
M_DK = 128
M_DV = 256
POOL_WINDOWS = (2, 4, 8, 16)
POOL_GDIM = 256
POOL_HIST = 15
MLA_HEADS = 16
MLA_NOPE = 64
MLA_ROPE = 32
MLA_QK = 96
MLA_V = 64
MLA_Q_LORA = 512
MLA_KV_LORA = 256
MLA_CACHE_DIM = 288
ROPE_BASE = 10000.0
CONV_WIDTH = 31
CONV_HIST = 30
D_FF = 2816
FFN_HIST = 2

LANE = 128
FF_CHUNK = 256
N_FF_CHUNK = D_FF // FF_CHUNK
NEG = -1e30
LOG2E = 1.4426950408889634
VMEM_LIMIT = 56 * 1024 * 1024


def _dot(a, b):
    return jnp.dot(a, b, preferred_element_type=F32)


def _dot_nt(a, b):
    return lax.dot_general(a, b, (((1,), (1,)), ((), ())), preferred_element_type=F32)


def _dot_tn(a, b):
    return lax.dot_general(a, b, (((0,), (0,)), ((), ())), preferred_element_type=F32)


def _rms(xf, g):
    return xf * lax.rsqrt(jnp.mean(xf * xf, axis=-1, keepdims=True) + EPS) * g


def _sigmoid(x):
    return 1.0 / (1.0 + jnp.exp(-x))


def _valid_rows(i, tm, tps, pad):
    r = (i % tps) * tm + lax.broadcasted_iota(jnp.int32, (tm, 1), 0)
    return r >= pad


def _params(n_axes):
    return pltpu.CompilerParams(dimension_semantics=("arbitrary",) * n_axes,
                                vmem_limit_bytes=VMEM_LIMIT)


def _row_spec(tm, c):
    return pl.BlockSpec((tm, c), lambda i: (i, 0))


def _full_spec(shape):
    zeros = (0,) * len(shape)
    return pl.BlockSpec(shape, lambda *_: zeros)


def _sds(shape, dtype):
    return jax.ShapeDtypeStruct(shape, dtype)


def _m_in_kernel(x_ref, g_ref, wq, wk, wv, wo, wgh, wgl, bg,
                 q_o, k_o, v_o, o_o, gt_o, *, tps, pad):
    tm = x_ref.shape[0]
    hf = _rms(x_ref[...], g_ref[...])
    h = hf.astype(BF16)
    q_o[...] = (_dot(h, wq[...]) * (M_DK ** -0.5)).astype(BF16)
    k_o[...] = _dot(h, wk[...]).astype(BF16)
    v_o[...] = _dot(h, wv[...]).astype(BF16)
    o_o[...] = _sigmoid(_dot(h, wo[...])).astype(BF16)
    hl = (hf - h.astype(F32)).astype(BF16)
    gts = _dot(h, wgh[...]) + _dot(hl, wgh[...]) + _dot(h, wgl[...]) + bg[...]
    lane = lax.broadcasted_iota(jnp.int32, gts.shape, 1)
    lf = jnp.minimum(gts, 0.0) - jnp.log(1.0 + jnp.exp(-jnp.abs(gts)))
    out = jnp.where(lane < M_HEADS, gts, lf)
    if pad:
        valid = _valid_rows(pl.program_id(0), tm, tps, pad)
        out = jnp.where(valid, out, jnp.where(lane < M_HEADS, NEG, 0.0))
    gt_o[...] = out


def _m_in(x, g, w, *, tm, tps, pad):
    r = x.shape[0]
    kern = functools.partial(_m_in_kernel, tps=tps, pad=pad)
    return pl.pallas_call(
        kern, grid=(r // tm,),
        in_specs=[_row_spec(tm, D_MODEL), _full_spec((1, D_MODEL)),
                  _full_spec(w["wq"].shape), _full_spec(w["wk"].shape), _full_spec(w["wv"].shape),
                  _full_spec(w["wo"].shape), _full_spec(w["wgh"].shape), _full_spec(w["wgl"].shape),
                  _full_spec((1, LANE))],
        out_specs=[_row_spec(tm, 512), _row_spec(tm, 512), _row_spec(tm, 1024),
                   _row_spec(tm, 1024), _row_spec(tm, LANE)],
        out_shape=[_sds((r, 512), BF16), _sds((r, 512), BF16), _sds((r, 1024), BF16),
                   _sds((r, 1024), BF16), _sds((r, LANE), F32)],
        compiler_params=_params(1), name="mlstm_in",
    )(x, g, w["wq"], w["wk"], w["wv"], w["wo"], w["wgh"], w["wgl"], w["bg"])


def _split3(a):
    a1 = a.astype(BF16)
    r1 = a - a1.astype(F32)
    a2 = r1.astype(BF16)
    a3 = (r1 - a2.astype(F32)).astype(BF16)
    return a1, a2, a3


def _m_chunk_kernel(q_ref, k_ref, v_ref, gc_ref, gr_ref, c0_ref, n0_ref, m0_ref,
                    h_ref, c_o, n_o, m_o, c_s, n_s, m_s):
    c = pl.program_id(1)
    ln = q_ref.shape[0]

    @pl.when(c == 0)
    def _():
        c_s[...] = c0_ref[0]
        n_s[...] = n0_ref[0]
        m_s[...] = m0_ref[0]

    row = lax.broadcasted_iota(jnp.int32, (ln, ln), 0)
    col = lax.broadcasted_iota(jnp.int32, (ln, ln), 1)
    causal = col <= row
    tril = jnp.where(causal, 1.0, 0.0).astype(BF16)
    triu = jnp.where(row <= col, 1.0, 0.0).astype(BF16)
    gc = gc_ref[...]
    gr = gr_ref[0]
    c1, c2, c3 = _split3(gc)
    b_col = _dot(tril, c1) + _dot(tril, c2) + _dot(tril, c3)
    r1, r2, r3 = _split3(gr)
    b_row = _dot(r1, triu) + _dot(r2, triu) + _dot(r3, triu)

    for hd in range(M_HEADS):
        ig_c = gc[:, hd:hd + 1]
        b_c = b_col[:, M_HEADS + hd:M_HEADS + hd + 1]
        ig_r = gr[hd:hd + 1, :]
        b_r = b_row[M_HEADS + hd:M_HEADS + hd + 1, :]
        m0 = m_s[hd]
        c0 = c_s[hd]
        n0 = n_s[hd]
        dmat = jnp.where(causal, b_c - b_r + ig_r, NEG)
        inter = b_c + m0
        m_t = jnp.maximum(inter, jnp.max(dmat, axis=1, keepdims=True))
        w_intra = jnp.exp(dmat - m_t)
        w_inter = jnp.exp(inter - m_t)
        qh = q_ref[:, hd * M_DK:(hd + 1) * M_DK]
        kh = k_ref[:, hd * M_DK:(hd + 1) * M_DK]
        vh = v_ref[:, hd * M_DV:(hd + 1) * M_DV]
        s = _dot_nt(qh, kh) * w_intra
        num = _dot(s.astype(BF16), vh) + w_inter * _dot(qh, c0.astype(BF16))
        den = (jnp.sum(s, axis=1, keepdims=True)
               + w_inter * jnp.sum(qh.astype(F32) * n0, axis=1, keepdims=True))
        h_ref[:, hd * M_DV:(hd + 1) * M_DV] = num / jnp.maximum(jnp.abs(den), jnp.exp(-m_t))
        b_last = b_c[ln - 1:ln, :]
        m_new = m_t[ln - 1:ln, :]
        wk_c = jnp.exp(b_last - b_c + ig_c - m_new)
        decay = jnp.exp(b_last + m0 - m_new)
        vw = (vh.astype(F32) * wk_c).astype(BF16)
        c_s[hd] = decay * c0 + _dot_tn(kh, vw)
        n_s[hd] = decay * n0 + jnp.sum(kh.astype(F32) * wk_c, axis=0, keepdims=True)
        m_s[hd] = m_new

    @pl.when(c == pl.num_programs(1) - 1)
    def _():
        c_o[0] = c_s[...]
        n_o[0] = n_s[...]
        m_o[0] = m_s[...]


def _m_chunk(q, k, v, gcol, grow, c0, n0, m0, *, ln):
    n = c0.shape[0]
    t = q.shape[0] // n
    nc = t // ln
    rows = lambda w: pl.BlockSpec((ln, w), lambda i, c: (i * nc + c, 0))
    st_c = pl.BlockSpec((1, M_HEADS, M_DK, M_DV), lambda i, c: (i, 0, 0, 0))
    st_n = pl.BlockSpec((1, M_HEADS, 1, M_DK), lambda i, c: (i, 0, 0, 0))
    st_m = pl.BlockSpec((1, M_HEADS, 1, 1), lambda i, c: (i, 0, 0, 0))
    return pl.pallas_call(
        _m_chunk_kernel, grid=(n, nc),
        in_specs=[rows(512), rows(512), rows(1024), rows(LANE),
                  pl.BlockSpec((1, 8, ln), lambda i, c: (i, 0, c)), st_c, st_n, st_m],
        out_specs=[rows(1024), st_c, st_n, st_m],
        out_shape=[_sds((n * t, 1024), F32), _sds((n, M_HEADS, M_DK, M_DV), F32),
                   _sds((n, M_HEADS, 1, M_DK), F32), _sds((n, M_HEADS, 1, 1), F32)],
        scratch_shapes=[pltpu.VMEM((M_HEADS, M_DK, M_DV), F32), pltpu.VMEM((M_HEADS, 1, M_DK), F32),
                        pltpu.VMEM((M_HEADS, 1, 1), F32)],
        compiler_params=_params(2), name="mlstm_chunk",
    )(q, k, v, gcol, grow, c0, n0, m0)


def _m_out_kernel(hh_ref, o_ref, x_ref, g_ref, w_ref, y_ref, *, tps, pad):
    tm = x_ref.shape[0]
    parts = []
    for hd in range(M_HEADS):
        sl = slice(hd * M_DV, (hd + 1) * M_DV)
        hn = _rms(hh_ref[:, sl], g_ref[:, sl])
        parts.append((hn * o_ref[:, sl].astype(F32)).astype(BF16))
    y = x_ref[...] + _dot(jnp.concatenate(parts, axis=1), w_ref[...])
    if pad:
        y = jnp.where(_valid_rows(pl.program_id(0), tm, tps, pad), y, 0.0)
    y_ref[...] = y


def _m_out(hh, o, x, g, w, *, tm, tps, pad):
    r = x.shape[0]
    kern = functools.partial(_m_out_kernel, tps=tps, pad=pad)
    return pl.pallas_call(
        kern, grid=(r // tm,),
        in_specs=[_row_spec(tm, 1024), _row_spec(tm, 1024), _row_spec(tm, D_MODEL),
                  _full_spec((1, 1024)), _full_spec((1024, D_MODEL))],
        out_specs=_row_spec(tm, D_MODEL), out_shape=_sds((r, D_MODEL), F32),
        compiler_params=_params(1), name="mlstm_out",
    )(hh, o, x, g, w)


def _ffn_kernel(x_ref, g_ref, wg_ref, wu_ref, cw_ref, cb_ref, wd_ref,
                y_ref, tail_ref, carry_s, acc_s, *, tps, pad):
    i = pl.program_id(0)
    tm = x_ref.shape[0]
    x = x_ref[...]
    h = _rms(x, g_ref[...]).astype(BF16)

    @pl.when(i % tps == 0)
    def _():
        carry_s[...] = jnp.zeros_like(carry_s)

    acc_s[...] = jnp.zeros_like(acc_s)
    rowid = lax.broadcasted_iota(jnp.int32, (tm, 1), 0)

    def chunk(c, carry):
        g = _dot(h, wg_ref[c])
        u = _dot(h, wu_ref[c])
        prev = carry_s[c]
        g1 = jnp.where(rowid == 0, prev[7:8], pltpu.roll(g, 1, 0))
        g2 = jnp.where(rowid == 0, prev[6:7], jnp.where(rowid == 1, prev[7:8], pltpu.roll(g, 2, 0)))
        cw = cw_ref[c]
        gc = cw[0:1] * g2 + cw[1:2] * g1 + cw[2:3] * g + cb_ref[c]
        act = (gc * _sigmoid(gc) * u).astype(BF16)
        acc_s[...] += _dot(act, wd_ref[c])
        last = g[tm - 8:tm]
        carry_s[c] = last
        tail_ref[0, c] = last
        return carry

    lax.fori_loop(0, N_FF_CHUNK, chunk, 0)
    y = x + acc_s[...]
    if pad:
        y = jnp.where(_valid_rows(i, tm, tps, pad), y, 0.0)
    y_ref[...] = y


def _ffn(x, g, w, *, tm, tps, pad):
    r = x.shape[0]
    nt = r // tm
    kern = functools.partial(_ffn_kernel, tps=tps, pad=pad)
    return pl.pallas_call(
        kern, grid=(nt,),
        in_specs=[_row_spec(tm, D_MODEL), _full_spec((1, D_MODEL)),
                  _full_spec(w["wg"].shape), _full_spec(w["wu"].shape), _full_spec(w["cw"].shape),
                  _full_spec(w["cb"].shape), _full_spec(w["wd"].shape)],
        out_specs=[_row_spec(tm, D_MODEL),
                   pl.BlockSpec((1, N_FF_CHUNK, 8, FF_CHUNK), lambda i: (i, 0, 0, 0))],
        out_shape=[_sds((r, D_MODEL), F32), _sds((nt, N_FF_CHUNK, 8, FF_CHUNK), F32)],
        scratch_shapes=[pltpu.VMEM((N_FF_CHUNK, 8, FF_CHUNK), F32), pltpu.VMEM((tm, D_MODEL), F32)],
        compiler_params=_params(1), name="ffn",
    )(x, g, w["wg"], w["wu"], w["cw"], w["cb"], w["wd"])


def _ffn_s_kernel(x_ref, g_ref, hist_ref, wg_ref, wu_ref, cw_ref, cb_ref, wd_ref,
                  y_ref, tail_ref, acc_s, *, nb):
    c = pl.program_id(0)
    x = x_ref[...]
    h = _rms(x, g_ref[...]).astype(BF16)

    @pl.when(c == 0)
    def _():
        acc_s[...] = jnp.zeros_like(acc_s)

    g = _dot(h, wg_ref[0])
    u = _dot(h, wu_ref[0])
    hist = hist_ref[...]
    t = g.shape[0]
    ext0 = jnp.concatenate([hist, g[:t - 2 * nb]], axis=0)
    ext1 = jnp.concatenate([hist[nb:], g[:t - nb]], axis=0)
    cw = cw_ref[0]
    gc = cw[0:1] * ext0 + cw[1:2] * ext1 + cw[2:3] * g + cb_ref[0]
    act = (gc * _sigmoid(gc) * u).astype(BF16)
    acc_s[...] += _dot(act, wd_ref[0])
    tail_ref[...] = g[t - 2 * nb:]

    @pl.when(c == pl.num_programs(0) - 1)
    def _():
        y_ref[...] = x + acc_s[...]


def _ffn_s(x, g, hist, w, *, nb):
    r = x.shape[0]
    kern = functools.partial(_ffn_s_kernel, nb=nb)
    chunk3 = lambda a, b: pl.BlockSpec((1, a, b), lambda c: (c, 0, 0))
    return pl.pallas_call(
        kern, grid=(N_FF_CHUNK,),
        in_specs=[_full_spec((r, D_MODEL)), _full_spec((1, D_MODEL)),
                  pl.BlockSpec((2 * nb, FF_CHUNK), lambda c: (0, c)),
                  chunk3(D_MODEL, FF_CHUNK), chunk3(D_MODEL, FF_CHUNK), chunk3(8, FF_CHUNK),
                  chunk3(1, FF_CHUNK), chunk3(FF_CHUNK, D_MODEL)],
        out_specs=[_full_spec((r, D_MODEL)), pl.BlockSpec((2 * nb, FF_CHUNK), lambda c: (0, c))],
        out_shape=[_sds((r, D_MODEL), F32), _sds((2 * nb, D_FF), F32)],
        scratch_shapes=[pltpu.VMEM((r, D_MODEL), F32)],
        compiler_params=_params(1), name="ffn_sample",
    )(x, g, hist, w["wg"], w["wu"], w["cw"], w["cb"], w["wd"])


def _pool_kernel(x_ref, g_ref, pw_ref, ps_ref, y_ref, tail_ref, ext_s, *, tps, pad):
    i = pl.program_id(0)
    tm = x_ref.shape[0]
    x = x_ref[...]
    u = _rms(x, g_ref[...])
    hist = POOL_HIST + 1

    @pl.when(i % tps == 0)
    def _():
        ext_s[0:hist] = jnp.zeros((hist, D_MODEL), F32)

    @pl.when(i % tps != 0)
    def _():
        ext_s[0:hist] = ext_s[tm:tm + hist]

    ext_s[hist:hist + tm] = u
    tail_ref[0] = u[tm - hist:]
    pos = (i % tps) * tm + lax.broadcasted_iota(jnp.int32, (tm, 1), 0) - pad
    valid = pos >= 0
    for gi, w in enumerate(POOL_WINDOWS):
        sl = slice(gi * POOL_GDIM, (gi + 1) * POOL_GDIM)
        ug = u[:, sl]
        win = ug
        for j in range(1, w):
            win = win + ext_s[hist - j:hist - j + tm, sl]
        cnt = jnp.maximum(jnp.minimum(pos + 1, w), 1).astype(F32)
        d = (win / cnt - ug).astype(BF16)
        y = x[:, sl] + _dot(d, pw_ref[gi]) * ps_ref[:, sl]
        y_ref[:, sl] = jnp.where(valid, y, 0.0)


def _pool(x, g, pw, ps, *, tm, tps, pad):
    r = x.shape[0]
    nt = r // tm
    kern = functools.partial(_pool_kernel, tps=tps, pad=pad)
    return pl.pallas_call(
        kern, grid=(nt,),
        in_specs=[_row_spec(tm, D_MODEL), _full_spec((1, D_MODEL)),
                  _full_spec((4, POOL_GDIM, POOL_GDIM)), _full_spec((1, D_MODEL))],
        out_specs=[_row_spec(tm, D_MODEL), pl.BlockSpec((1, 16, D_MODEL), lambda i: (i, 0, 0))],
        out_shape=[_sds((r, D_MODEL), F32), _sds((nt, 16, D_MODEL), F32)],
        scratch_shapes=[pltpu.VMEM((tm + 16, D_MODEL), F32)],
        compiler_params=_params(1), name="pool",
    )(x, g, pw, ps)


def _pool_s_kernel(x_ref, g_ref, hist_ref, pw_ref, ps_ref, y_ref, u_ref, *, nb, past_len):
    x = x_ref[...]
    u = _rms(x, g_ref[...])
    u_ref[...] = u
    nt = x.shape[0] // nb
    for gi, w in enumerate(POOL_WINDOWS):
        sl = slice(gi * POOL_GDIM, (gi + 1) * POOL_GDIM)
        outs = []
        for t in range(nt):
            win = u[t * nb:(t + 1) * nb, sl]
            for j in range(1, w):
                src = t - j
                if src >= 0:
                    win = win + u[src * nb:(src + 1) * nb, sl]
                else:
                    hrow = POOL_HIST + src
                    win = win + hist_ref[hrow * nb:(hrow + 1) * nb, sl]
            cnt = float(min(past_len + t + 1, w))
            outs.append(win / cnt - u[t * nb:(t + 1) * nb, sl])
        d = jnp.concatenate(outs, axis=0).astype(BF16)
        y_ref[:, sl] = x[:, sl] + _dot(d, pw_ref[gi]) * ps_ref[:, sl]


def _pool_s(x, g, hist, pw, ps, *, nb, past_len):
    r = x.shape[0]
    kern = functools.partial(_pool_s_kernel, nb=nb, past_len=past_len)
    return pl.pallas_call(
        kern, grid=(1,),
        in_specs=[_full_spec((r, D_MODEL)), _full_spec((1, D_MODEL)), _full_spec(hist.shape),
                  _full_spec((4, POOL_GDIM, POOL_GDIM)), _full_spec((1, D_MODEL))],
        out_specs=[_full_spec((r, D_MODEL)), _full_spec((r, D_MODEL))],
        out_shape=[_sds((r, D_MODEL), F32), _sds((r, D_MODEL), F32)],
        compiler_params=_params(1), name="pool_sample",
    )(x, g, hist, pw, ps)


def _rope_norm(z, g, tc, ts1, ts2):
    zn = z * lax.rsqrt(jnp.sum(z * z, axis=-1, keepdims=True) * (1.0 / MLA_QK) + EPS) * g
    return zn * tc + pltpu.roll(zn, 16, 1) * ts1 + pltpu.roll(zn, LANE - 16, 1) * ts2


def _mla_in_kernel(x_ref, g_ref, wdq, qlg, wuq, qng, wlat, wkpe, kvg, wuk, wuv, kng,
                   tc_ref, ts1_ref, ts2_ref, qh_o, kh_o, vp_o, lat_o, kpe_o):
    h = _rms(x_ref[...], g_ref[...]).astype(BF16)
    cqn = _rms(_dot(h, wdq[...]), qlg[...]).astype(BF16)
    latn = _rms(_dot(h, wlat[...]), kvg[...])
    lat_o[...] = latn
    latb = latn.astype(BF16)
    kpe = _dot(h, wkpe[...])
    kpe_o[...] = kpe
    tc, ts1, ts2 = tc_ref[...], ts1_ref[...], ts2_ref[...]
    for j in range(MLA_HEADS // 2):
        q2 = _dot(cqn, wuq[j])
        k2 = _dot(latb, wuk[j])
        for half in range(2):
            sl = slice(half * LANE, (half + 1) * LANE)
            qh_o[2 * j + half] = _rope_norm(q2[:, sl], qng[...], tc, ts1, ts2).astype(qh_o.dtype)
            kh_o[2 * j + half] = _rope_norm(k2[:, sl] + kpe, kng[...], tc, ts1, ts2).astype(BF16)
    ones = jnp.ones((latb.shape[0], LANE), BF16)
    for j in range(MLA_HEADS // 4):
        v4 = _dot(latb, wuv[j]).astype(BF16)
        vp_o[2 * j] = jnp.concatenate([v4[:, :LANE], ones], axis=1)
        vp_o[2 * j + 1] = jnp.concatenate([v4[:, LANE:], ones], axis=1)


def _mla_in(x, g, w, tabs, *, tm, qdtype):
    r = x.shape[0]
    heads = lambda n, w=LANE: pl.BlockSpec((n, tm, w), lambda i: (0, i, 0))
    return pl.pallas_call(
        _mla_in_kernel, grid=(r // tm,),
        in_specs=[_row_spec(tm, D_MODEL), _full_spec((1, D_MODEL)),
                  _full_spec(w["wdq"].shape), _full_spec((1, MLA_Q_LORA)), _full_spec(w["wuq"].shape),
                  _full_spec((1, LANE)), _full_spec(w["wlat"].shape), _full_spec(w["wkpe"].shape),
                  _full_spec((1, MLA_KV_LORA)), _full_spec(w["wuk"].shape), _full_spec(w["wuv"].shape),
                  _full_spec((1, LANE)), _row_spec(tm, LANE), _row_spec(tm, LANE), _row_spec(tm, LANE)],
        out_specs=[heads(MLA_HEADS), heads(MLA_HEADS), heads(MLA_HEADS // 2, 2 * LANE),
                   _row_spec(tm, MLA_KV_LORA), _row_spec(tm, LANE)],
        out_shape=[_sds((MLA_HEADS, r, LANE), qdtype), _sds((MLA_HEADS, r, LANE), BF16),
                   _sds((MLA_HEADS // 2, r, 2 * LANE), BF16), _sds((r, MLA_KV_LORA), F32),
                   _sds((r, LANE), F32)],
        compiler_params=_params(1), name="mla_in",
    )(x, g, w["wdq"], w["qlg"], w["wuq"], w["qng"], w["wlat"], w["wkpe"], w["kvg"],
      w["wuk"], w["wuv"], w["kng"], *tabs)


def _attn_kernel(q_ref, k_ref, v_ref, o_ref, acc_s, m_s, *, pad, nsub):
    qi = pl.program_id(2)
    tq = q_ref.shape[1]
    tk = tq
    tqs = tq // nsub
    for hh in range(2):
        m_s[hh] = jnp.full((tq, LANE), NEG, F32)
        acc_s[hh] = jnp.zeros((tq, 2 * LANE), F32)

    def block(ki, masked):
        ks = pl.multiple_of(ki * tk, tk)
        vblk = v_ref[0, pl.ds(ks, tk), :]
        for qs in range(nsub):
            rows = slice(qs * tqs, (qs + 1) * tqs)
            for hh in range(2):
                s = _dot_nt(q_ref[hh, rows, :], k_ref[hh, pl.ds(ks, tk), :])
                if masked:
                    qpos = qi * tq + qs * tqs + lax.broadcasted_iota(jnp.int32, (tqs, tk), 0)
                    kpos = ki * tk + lax.broadcasted_iota(jnp.int32, (tqs, tk), 1)
                    s = jnp.where(kpos <= qpos, jnp.where(kpos >= pad, s, NEG), NEG)
                m_old = m_s[hh, rows, :]
                m_new = jnp.maximum(m_old, jnp.max(s, axis=1, keepdims=True))
                p = jnp.exp2(s - jnp.tile(m_new, (1, tk // LANE)))
                alpha = jnp.exp2(m_old - m_new)
                acc_s[hh, rows, :] = (jnp.tile(alpha, (1, 2)) * acc_s[hh, rows, :]
                                      + _dot(p.astype(BF16), vblk))
                m_s[hh, rows, :] = m_new

    block(0, True)

    def body(ki, carry):
        block(ki, False)
        return carry

    lax.fori_loop(1, qi, body, 0)

    @pl.when(qi > 0)
    def _():
        block(qi, True)

    lane = lax.broadcasted_iota(jnp.int32, (tq, LANE), 1)
    a0 = acc_s[0]
    a1 = acc_s[1]
    o_ref[...] = jnp.where(lane < MLA_V, a0[:, :LANE] / a0[:, LANE:],
                           a1[:, :LANE] / a1[:, LANE:]).astype(BF16)


def _attn(qh, kh, vp, *, n, tp, tq, pad):
    nq = tp // tq
    kern = functools.partial(_attn_kernel, pad=pad, nsub=2)
    return pl.pallas_call(
        kern, grid=(n, MLA_HEADS // 2, nq),
        in_specs=[pl.BlockSpec((2, tq, LANE), lambda b, hp, qi: (hp, b * nq + qi, 0)),
                  pl.BlockSpec((2, tp, LANE), lambda b, hp, qi: (hp, b, 0)),
                  pl.BlockSpec((1, tp, 2 * LANE), lambda b, hp, qi: (hp, b, 0))],
        out_specs=pl.BlockSpec((tq, LANE), lambda b, hp, qi: (b * nq + qi, hp)),
        out_shape=_sds((n * tp, MLA_HEADS * MLA_V), BF16),
        scratch_shapes=[pltpu.VMEM((2, tq, 2 * LANE), F32), pltpu.VMEM((2, tq, LANE), F32)],
        compiler_params=_params(3), name="attn",
    )(qh, kh, vp)


def _proj_out_kernel(o_ref, x_ref, w_ref, y_ref, *, tps, pad):
    tm = x_ref.shape[0]
    y = x_ref[...] + _dot(o_ref[...], w_ref[...])
    if pad:
        y = jnp.where(_valid_rows(pl.program_id(0), tm, tps, pad), y, 0.0)
    y_ref[...] = y


def _proj_out(o, x, w, *, tm, tps, pad):
    r = x.shape[0]
    kern = functools.partial(_proj_out_kernel, tps=tps, pad=pad)
    return pl.pallas_call(
        kern, grid=(r // tm,),
        in_specs=[_row_spec(tm, 1024), _row_spec(tm, D_MODEL), _full_spec((1024, D_MODEL))],
        out_specs=_row_spec(tm, D_MODEL), out_shape=_sds((r, D_MODEL), F32),
        compiler_params=_params(1), name="mla_out",
    )(o, x, w)


def _dec_prep_kernel(qh_ref, kng, wukt, qabs_o, ab_o):
    g = kng[...]
    g1 = g[:, MLA_NOPE:MLA_NOPE + 16]
    g2 = g[:, MLA_NOPE + 16:MLA_QK]
    for hd in range(MLA_HEADS):
        qh = qh_ref[hd]
        qabs_o[hd] = _dot((qh * g).astype(BF16), wukt[hd]).astype(BF16)
        q1 = qh[:, MLA_NOPE:MLA_NOPE + 16]
        q2 = qh[:, MLA_NOPE + 16:MLA_QK]
        ab_o[hd] = jnp.concatenate([g1 * q1, g2 * q2, g1 * q2, -(g2 * q1)], axis=1).astype(BF16)


def _dec_prep(qh, kng, wukt):
    r = qh.shape[1]
    return pl.pallas_call(
        _dec_prep_kernel, grid=(1,),
        in_specs=[_full_spec(qh.shape), _full_spec((1, LANE)), _full_spec(wukt.shape)],
        out_specs=[_full_spec((MLA_HEADS, r, MLA_KV_LORA)), _full_spec((MLA_HEADS, r, 64))],
        out_shape=[_sds((MLA_HEADS, r, MLA_KV_LORA), BF16), _sds((MLA_HEADS, r, 64), BF16)],
        compiler_params=_params(1), name="decode_prep",
    )(qh, kng, wukt)


def _decode_kernel(pt_ref, new_ref, qm_ref, ab_ref, wuk_ref, cs_ref, *rest,
                   n_pages, page, pages_per_chunk):
    del pt_ref
    page_refs = rest[:n_pages]
    o_ref, s_scr, lat_scr = rest[n_pages:]
    nq = qm_ref.shape[1]
    lhs = jnp.concatenate([qm_ref[0], wuk_ref[...]], axis=0)
    ab = ab_ref[0]
    ck = pages_per_chunk * page
    n_chunk = n_pages // pages_per_chunk
    past = n_pages * page
    nn = new_ref.shape[2]

    def scores(lat, kpe, cs):
        n = lat.shape[1]
        latb = lat.astype(BF16)
        kcs = (jnp.concatenate([kpe, kpe], axis=0) * cs).astype(BF16)
        big = _dot(lhs, latb)
        raw = big[:nq] + _dot(ab, kcs)
        kn = big[nq:]
        ss = jnp.sum((kn * kn).reshape(MLA_NOPE, MLA_HEADS, n), axis=0)
        ss = ss + jnp.sum(kpe * kpe, axis=0, keepdims=True)
        r = lax.rsqrt(ss * (1.0 / MLA_QK) + EPS)
        return latb, jnp.concatenate([r] * (nq // MLA_HEADS), axis=0) * raw

    m = jnp.full((nq, 1), NEG, F32)
    for c in range(n_chunk):
        blks = [page_refs[c * pages_per_chunk + p][0] for p in range(pages_per_chunk)]
        lat = jnp.concatenate([b[:MLA_KV_LORA, :] for b in blks], axis=1)
        kpe = jnp.concatenate([b[MLA_KV_LORA:, :] for b in blks], axis=1)
        latb, s = scores(lat, kpe, cs_ref[:, c * ck:(c + 1) * ck])
        s_scr[:, c * ck:(c + 1) * ck] = s
        lat_scr[:, c * ck:(c + 1) * ck] = latb
        m = jnp.maximum(m, jnp.max(s, axis=1, keepdims=True))
    blk = new_ref[0]
    latb, s = scores(blk[:MLA_KV_LORA, :], blk[MLA_KV_LORA:, :], cs_ref[:, past:past + nn])
    key = lax.broadcasted_iota(jnp.int32, (nq, nn), 1)
    tok = lax.broadcasted_iota(jnp.int32, (nq, nn), 0) // MLA_HEADS
    s = jnp.where(key <= tok, s, NEG)
    s_scr[:, past:past + nn] = s
    lat_scr[:, past:past + nn] = latb
    m = jnp.maximum(m, jnp.max(s, axis=1, keepdims=True))

    p = jnp.exp2(s_scr[...] - m)
    l = jnp.sum(p, axis=1, keepdims=True)
    o_ref[0] = _dot_nt(p.astype(BF16), lat_scr[...]) / l


def _decode(page_table, cache_t, new_t, qm, ab, wuk, cs, *, pages_per_chunk=8):
    nb, n_pages = page_table.shape
    page = cache_t.shape[2]
    nn = new_t.shape[2]
    nq = qm.shape[1]
    kern = functools.partial(_decode_kernel, n_pages=n_pages, page=page,
                             pages_per_chunk=pages_per_chunk)
    per_seq = lambda shape: pl.BlockSpec((1,) + shape, lambda b, pt: (b, 0, 0))
    page_specs = [pl.BlockSpec((1, MLA_CACHE_DIM, page),
                               lambda b, pt, j=j: (pt[b * n_pages + j], 0, 0))
                  for j in range(n_pages)]
    total = n_pages * page + nn
    grid_spec = pltpu.PrefetchScalarGridSpec(
        num_scalar_prefetch=1, grid=(nb,),
        in_specs=[per_seq((MLA_CACHE_DIM, nn)), per_seq((nq, MLA_KV_LORA)), per_seq((nq, 64)),
                  pl.BlockSpec(wuk.shape, lambda b, pt: (0, 0)),
                  pl.BlockSpec(cs.shape, lambda b, pt: (0, 0))] + page_specs,
        out_specs=per_seq((nq, MLA_KV_LORA)),
        scratch_shapes=[pltpu.VMEM((nq, total), F32), pltpu.VMEM((MLA_KV_LORA, total), BF16)])
    return pl.pallas_call(
        kern, grid_spec=grid_spec, out_shape=_sds((nb, nq, MLA_KV_LORA), F32),
        compiler_params=_params(1), name="decode",
    )(page_table.reshape(-1), new_t, qm, ab, wuk, cs, *([cache_t] * n_pages))


def _dec_out_kernel(ol_ref, wuv, wo, x_ref, y_ref):
    parts = [_dot(ol_ref[hd].astype(BF16), wuv[hd]) for hd in range(MLA_HEADS)]
    o = jnp.concatenate(parts, axis=1).astype(BF16)
    y_ref[...] = x_ref[...] + _dot(o, wo[...])


def _dec_out(ol, wuv, wo, x):
    r = x.shape[0]
    return pl.pallas_call(
        _dec_out_kernel, grid=(1,),
        in_specs=[_full_spec(ol.shape), _full_spec(wuv.shape),
                  _full_spec(wo.shape), _full_spec((r, D_MODEL))],
        out_specs=_full_spec((r, D_MODEL)), out_shape=_sds((r, D_MODEL), F32),
        compiler_params=_params(1), name="decode_out",
    )(ol, wuv, wo, x)


def _conv_in_kernel(x_ref, g_ref, w_ref, b_ref, gl_ref, *, tps, pad):
    tm = x_ref.shape[0]
    h = _rms(x_ref[...], g_ref[...]).astype(BF16)
    a = _dot(h, w_ref[...]) + b_ref[...]
    gl = a[:, :D_MODEL] * _sigmoid(a[:, D_MODEL:])
    if pad:
        gl = jnp.where(_valid_rows(pl.program_id(0), tm, tps, pad), gl, 0.0)
    gl_ref[...] = gl


def _conv_in(x, g, w, b, *, tm, tps, pad):
    r = x.shape[0]
    kern = functools.partial(_conv_in_kernel, tps=tps, pad=pad)
    return pl.pallas_call(
        kern, grid=(r // tm,),
        in_specs=[_row_spec(tm, D_MODEL), _full_spec((1, D_MODEL)),
                  _full_spec((D_MODEL, 2 * D_MODEL)), _full_spec((1, 2 * D_MODEL))],
        out_specs=_row_spec(tm, D_MODEL), out_shape=_sds((r, D_MODEL), F32),
        compiler_params=_params(1), name="conv_in",
    )(x, g, w, b)


def _ln_silu_proj(c, x, lng, lnb, w2, b2):
    mu = jnp.mean(c, axis=-1, keepdims=True)
    cc = c - mu
    var = jnp.mean(cc * cc, axis=-1, keepdims=True)
    y = cc * lax.rsqrt(var + EPS) * lng + lnb
    a = (y * _sigmoid(y)).astype(BF16)
    return x + _dot(a, w2) + b2


def _conv_out_kernel(gl_ref, x_ref, wdw, bdw, lng, lnb, w2, b2, y_ref, ext_s, *, tps, pad):
    i = pl.program_id(0)
    tm = x_ref.shape[0]
    hist = 32

    @pl.when(i % tps == 0)
    def _():
        ext_s[0:hist] = jnp.zeros((hist, D_MODEL), F32)

    @pl.when(i % tps != 0)
    def _():
        ext_s[0:hist] = ext_s[tm:tm + hist]

    ext_s[hist:hist + tm] = gl_ref[...]
    off = hist - CONV_HIST
    acc = wdw[0:1, :] * ext_s[off:off + tm, :]
    for j in range(1, CONV_WIDTH):
        acc = acc + wdw[j:j + 1, :] * ext_s[off + j:off + j + tm, :]
    y = _ln_silu_proj(acc + bdw[...], x_ref[...], lng[...], lnb[...], w2[...], b2[...])
    if pad:
        y = jnp.where(_valid_rows(i, tm, tps, pad), y, 0.0)
    y_ref[...] = y


def _conv_out(gl, x, w, *, tm, tps, pad):
    r = x.shape[0]
    kern = functools.partial(_conv_out_kernel, tps=tps, pad=pad)
    vec = _full_spec((1, D_MODEL))
    return pl.pallas_call(
        kern, grid=(r // tm,),
        in_specs=[_row_spec(tm, D_MODEL), _row_spec(tm, D_MODEL), _full_spec((32, D_MODEL)),
                  vec, vec, vec, _full_spec((D_MODEL, D_MODEL)), vec],
        out_specs=_row_spec(tm, D_MODEL), out_shape=_sds((r, D_MODEL), F32),
        scratch_shapes=[pltpu.VMEM((tm + 32, D_MODEL), F32)],
        compiler_params=_params(1), name="conv_out",
    )(gl, x, w["wdw"], w["bdw"], w["lng"], w["lnb"], w["w2"], w["b2"])


def _conv_out_s_kernel(hist_ref, gl_ref, x_ref, wdw, bdw, lng, lnb, w2, b2, y_ref, *, nb):
    nt = x_ref.shape[0] // nb
    outs = []
    for t in range(nt):
        acc = None
        for j in range(CONV_WIDTH):
            src = t + j
            if src < CONV_HIST:
                slab = hist_ref[src * nb:(src + 1) * nb, :]
            else:
                slab = gl_ref[(src - CONV_HIST) * nb:(src - CONV_HIST + 1) * nb, :]
            term = wdw[j:j + 1, :] * slab
            acc = term if acc is None else acc + term
        outs.append(acc)
    c = jnp.concatenate(outs, axis=0) + bdw[...]
    y_ref[...] = _ln_silu_proj(c, x_ref[...], lng[...], lnb[...], w2[...], b2[...])


def _conv_out_s(hist, gl, x, w, *, nb):
    r = x.shape[0]
    kern = functools.partial(_conv_out_s_kernel, nb=nb)
    vec = _full_spec((1, D_MODEL))
    return pl.pallas_call(
        kern, grid=(1,),
        in_specs=[_full_spec(hist.shape), _full_spec((r, D_MODEL)), _full_spec((r, D_MODEL)),
                  _full_spec((32, D_MODEL)), vec, vec, vec, _full_spec((D_MODEL, D_MODEL)), vec],
        out_specs=_full_spec((r, D_MODEL)), out_shape=_sds((r, D_MODEL), F32),
        compiler_params=_params(1), name="conv_out_sample",
    )(hist, gl, x, w["wdw"], w["bdw"], w["lng"], w["lnb"], w["w2"], w["b2"])


def _pad_lanes(a, width=LANE):
    return jnp.pad(a, [(0, 0)] * (a.ndim - 1) + [(0, width - a.shape[-1])])


def _row(a):
    return a.reshape(1, -1).astype(F32)


def _prep_mlstm(m_w_in, m_b_gates, m_head_norm_g, m_w_out):
    i0 = M_HEADS * M_DK
    i1 = 2 * i0
    i2 = i1 + M_HEADS * M_DV
    i3 = i2 + M_HEADS * M_DV
    wgate = _pad_lanes(m_w_in[:, i3:])
    wgh = wgate.astype(BF16)
    wgl = (wgate - wgh.astype(F32)).astype(BF16)
    return dict(wq=m_w_in[:, :i0].astype(BF16), wk=m_w_in[:, i0:i1].astype(BF16),
                wv=m_w_in[:, i1:i2].astype(BF16), wo=m_w_in[:, i2:i3].astype(BF16),
                wgh=wgh, wgl=wgl, bg=_pad_lanes(_row(m_b_gates)),
                hg=_row(m_head_norm_g), wout=m_w_out.astype(BF16))


def _prep_ffn(w_up, conv_w, conv_b, w_down):
    chunks = lambda a: a.reshape(D_MODEL, N_FF_CHUNK, FF_CHUNK).transpose(1, 0, 2).astype(BF16)
    cw = jnp.pad(conv_w, ((0, 8 - conv_w.shape[0]), (0, 0)))
    return dict(wg=chunks(w_up[:, :D_FF]), wu=chunks(w_up[:, D_FF:]),
                cw=cw.reshape(8, N_FF_CHUNK, FF_CHUNK).transpose(1, 0, 2).astype(F32),
                cb=conv_b.reshape(N_FF_CHUNK, 1, FF_CHUNK).astype(F32),
                wd=w_down.reshape(N_FF_CHUNK, FF_CHUNK, D_MODEL).astype(BF16))


def _prep_mla(mla_w_dq, mla_q_lora_g, mla_w_uq, mla_w_dkv, mla_kv_lora_g, mla_w_ukv,
              mla_q_norm_g, mla_k_norm_g, mla_w_o):
    pair = lambda a: (a.reshape(a.shape[0], MLA_HEADS // 2, 2 * a.shape[2]).transpose(1, 0, 2))
    wuq = pair(_pad_lanes(mla_w_uq.reshape(MLA_Q_LORA, MLA_HEADS, MLA_QK)))
    ukv = mla_w_ukv.reshape(MLA_KV_LORA, MLA_HEADS, MLA_NOPE + MLA_V)
    uk = ukv[:, :, :MLA_NOPE]
    uv = ukv[:, :, MLA_NOPE:]
    wkpe = jnp.zeros((D_MODEL, LANE), F32).at[:, MLA_NOPE:MLA_QK].set(mla_w_dkv[:, MLA_KV_LORA:])
    qscale = (MLA_QK ** -0.5) * LOG2E
    return dict(
        wdq=mla_w_dq.astype(BF16), qlg=_row(mla_q_lora_g), wuq=wuq.astype(BF16),
        qng=_pad_lanes(_row(mla_q_norm_g)) * qscale,
        wlat=mla_w_dkv[:, :MLA_KV_LORA].astype(BF16), wkpe=wkpe.astype(BF16),
        kvg=_row(mla_kv_lora_g), wuk=pair(_pad_lanes(uk)).astype(BF16),
        wuv=uv.reshape(MLA_KV_LORA, MLA_HEADS // 4, 4 * MLA_V).transpose(1, 0, 2).astype(BF16),
        kng=_pad_lanes(_row(mla_k_norm_g)),
        wukt=jnp.pad(uk.transpose(1, 2, 0), ((0, 0), (0, LANE - MLA_NOPE), (0, 0))).astype(BF16),
        wuk_rows=uk.transpose(2, 1, 0).reshape(MLA_NOPE * MLA_HEADS, MLA_KV_LORA).astype(BF16),
        wuv_h=uv.transpose(1, 0, 2).astype(BF16), wo=mla_w_o.astype(BF16))


def _rope_tables(pos):
    half = MLA_ROPE // 2
    inv = ROPE_BASE ** (-jnp.arange(half, dtype=F32) / half)
    ang = pos.astype(F32)[:, None] * inv[None, :]
    cos, sin = jnp.cos(ang), jnp.sin(ang)
    r = pos.shape[0]
    ones = jnp.ones((r, MLA_NOPE), F32)
    z16 = jnp.zeros((r, half), F32)
    z32 = jnp.zeros((r, LANE - MLA_QK), F32)
    z64 = jnp.zeros((r, MLA_NOPE), F32)
    tc = jnp.concatenate([ones, cos, cos, z32], axis=1)
    ts1 = jnp.concatenate([z64, z16, sin, z32], axis=1)
    ts2 = jnp.concatenate([z64, -sin, z16, z32], axis=1)
    cs = jnp.concatenate([cos, cos, sin, sin], axis=1)
    return (tc, ts1, ts2), cs


def _prompt_trunk(x_prompt, meta_tokens, norm_mix_g, norm_ffn_g, wm, wffn, pool_w, pool_scale,
                  wmla, wconv):
    n, seq, _ = x_prompt.shape
    t_real = seq + N_META
    tm = 768
    tp = -(-t_real // tm) * tm
    pad = tp - t_real
    tps = tp // tm
    meta = jnp.broadcast_to(meta_tokens[None].astype(F32), (n, N_META, D_MODEL))
    x = jnp.concatenate([jnp.zeros((n, pad, D_MODEL), F32), meta, x_prompt], axis=1)
    x = x.reshape(n * tp, D_MODEL)
    kw = dict(tm=tm, tps=tps, pad=pad)
    st = {}
    ffn_hist = []

    def ffn(x, layer):
        y, tail = _ffn(x, _row(norm_ffn_g[layer]), wffn[layer], **kw)
        tail = tail.reshape(n, tps, N_FF_CHUNK, 8, FF_CHUNK)[:, -1, :, 8 - FFN_HIST:, :]
        ffn_hist.append(tail.transpose(0, 2, 1, 3).reshape(n, FFN_HIST, D_FF))
        return y

    q, k, v, o, gt = _m_in(x, _row(norm_mix_g[0]), wm, **kw)
    grow = gt[:, :8].reshape(n, tp, 8).transpose(0, 2, 1)
    zc = jnp.zeros((n, M_HEADS, M_DK, M_DV), F32)
    zn = jnp.zeros((n, M_HEADS, 1, M_DK), F32)
    zm = jnp.zeros((n, M_HEADS, 1, 1), F32)
    hh, c_new, n_new, m_new = _m_chunk(q, k, v, gt, grow, zc, zn, zm, ln=tm)
    st["mlstm_C"] = c_new
    st["mlstm_n"] = n_new.reshape(n, M_HEADS, M_DK)
    st["mlstm_m"] = m_new.reshape(n, M_HEADS)
    x = _m_out(hh, o, x, wm["hg"], wm["wout"], **kw)
    x = ffn(x, 0)

    x, tail = _pool(x, _row(norm_mix_g[1]), pool_w.astype(BF16), _row(pool_scale), **kw)
    st["pool"] = tail.reshape(n, tps, 16, D_MODEL)[:, -1, 16 - POOL_HIST:, :]
    x = ffn(x, 1)

    pos = jnp.maximum(jnp.arange(tp, dtype=jnp.int32) - pad, 0)
    tabs, _ = _rope_tables(jnp.tile(pos, n))
    qh, kh, vp, lat, kpe = _mla_in(x, _row(norm_mix_g[2]), wmla, tabs, tm=tm, qdtype=BF16)
    rows = jnp.concatenate([lat, kpe[:, MLA_NOPE:MLA_QK]], axis=1)
    st["mla"] = rows.reshape(n, tp, MLA_CACHE_DIM)[:, pad:, :]
    o = _attn(qh, kh, vp, n=n, tp=tp, tq=tm, pad=pad)
    x = _proj_out(o, x, wmla["wo"], **kw)
    x = ffn(x, 2)

    gl = _conv_in(x, _row(norm_mix_g[3]), wconv["w1"], wconv["b1"], **kw)
    st["conv"] = gl.reshape(n, tp, D_MODEL)[:, tp - CONV_HIST:, :]
    x = _conv_out(gl, x, wconv, **kw)
    x = ffn(x, 3)

    st["ffn"] = jnp.stack(ffn_hist, axis=0)
    y = x.reshape(n, tp, D_MODEL)[:, pad + N_META:, :]
    return y, st


def _sample_trunk(x_sample, state_mlstm_C, state_mlstm_n, state_mlstm_m, state_pool, cache_mla,
                  page_table, state_conv, state_ffn, norm_mix_g, norm_ffn_g, wm, wffn, pool_w,
                  pool_scale, wmla, wconv):
    nb, nt, _ = x_sample.shape
    r = nb * nt
    past_len = page_table.shape[1] * cache_mla.shape[1]
    tmaj = lambda a: a.transpose(1, 0, 2).reshape(a.shape[1] * nb, a.shape[2])
    smaj = lambda a: a.reshape(-1, nb, a.shape[-1]).transpose(1, 0, 2)
    x = tmaj(x_sample.astype(F32))
    kw = dict(tm=r, tps=1, pad=0)
    st = {}
    ffn_hist = []

    def ffn(x, layer):
        y, tail = _ffn_s(x, _row(norm_ffn_g[layer]), tmaj(state_ffn[layer]), wffn[layer], nb=nb)
        ffn_hist.append(smaj(tail))
        return y

    ln = 16
    q, k, v, o, gt = _m_in(x, _row(norm_mix_g[0]), wm, **kw)

    def seq_rows(a, fill=None):
        a = smaj(a)
        if fill is None:
            a = jnp.pad(a, ((0, 0), (0, ln - nt), (0, 0)))
        else:
            a = jnp.concatenate([a, jnp.broadcast_to(fill, (nb, ln - nt, a.shape[2]))], axis=1)
        return a.reshape(nb * ln, a.shape[2])

    lane = jnp.arange(LANE)
    neutral = jnp.where(lane < M_HEADS, NEG, 0.0).astype(F32)
    gts = seq_rows(gt, fill=neutral)
    grow = gts[:, :8].reshape(nb, ln, 8).transpose(0, 2, 1)
    hh, c_new, n_new, m_new = _m_chunk(
        seq_rows(q), seq_rows(k), seq_rows(v), gts, grow, state_mlstm_C.astype(F32),
        state_mlstm_n.astype(F32).reshape(nb, M_HEADS, 1, M_DK),
        state_mlstm_m.astype(F32).reshape(nb, M_HEADS, 1, 1), ln=ln)
    st["mlstm_C"] = c_new
    st["mlstm_n"] = n_new.reshape(nb, M_HEADS, M_DK)
    st["mlstm_m"] = m_new.reshape(nb, M_HEADS)
    hh = tmaj(hh.reshape(nb, ln, M_HEADS * M_DV)[:, :nt, :])
    x = _m_out(hh, o, x, wm["hg"], wm["wout"], **kw)
    x = ffn(x, 0)

    x, u = _pool_s(x, _row(norm_mix_g[1]), tmaj(state_pool.astype(F32)), pool_w.astype(BF16),
                   _row(pool_scale), nb=nb, past_len=past_len)
    st["pool"] = jnp.concatenate([state_pool.astype(F32), smaj(u)], axis=1)[:, -POOL_HIST:, :]
    x = ffn(x, 1)

    pos_new = past_len + jnp.repeat(jnp.arange(nt, dtype=jnp.int32), nb)
    tabs, _ = _rope_tables(pos_new)
    _, cs = _rope_tables(jnp.arange(past_len + LANE, dtype=jnp.int32))
    qh, _, _, lat, kpe = _mla_in(x, _row(norm_mix_g[2]), wmla, tabs, tm=r, qdtype=F32)
    rows = smaj(jnp.concatenate([lat, kpe[:, MLA_NOPE:MLA_QK]], axis=1))
    st["mla"] = rows
    qabs, ab = _dec_prep(qh, wmla["kng"], wmla["wukt"])
    seq_q = lambda a: (a.reshape(MLA_HEADS, nt, nb, a.shape[-1]).transpose(2, 1, 0, 3)
                       .reshape(nb, nt * MLA_HEADS, a.shape[-1]))
    new_t = _pad_lanes(rows.transpose(0, 2, 1))
    cache_t = cache_mla.astype(F32).transpose(0, 2, 1)
    ol = _decode(page_table, cache_t, new_t, seq_q(qabs), seq_q(ab), wmla["wuk_rows"], cs.T)
    ol = ol.reshape(nb, nt, MLA_HEADS, MLA_KV_LORA).transpose(2, 1, 0, 3).reshape(MLA_HEADS, r, MLA_KV_LORA)
    x = _dec_out(ol, wmla["wuv_h"], wmla["wo"], x)
    x = ffn(x, 2)

    gl = _conv_in(x, _row(norm_mix_g[3]), wconv["w1"], wconv["b1"], **kw)
    st["conv"] = jnp.concatenate([state_conv.astype(F32), smaj(gl)], axis=1)[:, -CONV_HIST:, :]
    x = _conv_out_s(tmaj(state_conv.astype(F32)), gl, x, wconv, nb=nb)
    x = ffn(x, 3)

    st["ffn"] = jnp.stack(ffn_hist, axis=0)
    return smaj(x), st


def kernel(x_prompt, x_sample, state_mlstm_C, state_mlstm_n, state_mlstm_m, state_pool, cache_mla, page_table, state_conv, state_ffn, meta_tokens, norm_mix_g, norm_ffn_g, m_w_in, m_b_gates, m_head_norm_g, m_w_out, pool_w, pool_scale, mla_w_dq, mla_q_lora_g, mla_w_uq, mla_w_dkv, mla_kv_lora_g, mla_w_ukv, mla_q_norm_g, mla_k_norm_g, mla_w_o, conv_w_pw1, conv_b_pw1, conv_w_dw, conv_b_dw, conv_ln_g, conv_ln_b, conv_w_pw2, conv_b_pw2, ffn_w_up, ffn_conv_w, ffn_conv_b, ffn_w_down):
    depth = ffn_w_up.shape[0]
    wm = _prep_mlstm(m_w_in, m_b_gates, m_head_norm_g, m_w_out)
    wffn = [_prep_ffn(ffn_w_up[i], ffn_conv_w[i], ffn_conv_b[i], ffn_w_down[i]) for i in range(depth)]
    wmla = _prep_mla(mla_w_dq, mla_q_lora_g, mla_w_uq, mla_w_dkv, mla_kv_lora_g, mla_w_ukv,
                     mla_q_norm_g, mla_k_norm_g, mla_w_o)
    wconv = dict(w1=conv_w_pw1.astype(BF16), b1=_row(conv_b_pw1),
                 wdw=jnp.pad(conv_w_dw, ((0, 32 - CONV_WIDTH), (0, 0))).astype(F32),
                 bdw=_row(conv_b_dw), lng=_row(conv_ln_g), lnb=_row(conv_ln_b),
                 w2=conv_w_pw2.astype(BF16), b2=_row(conv_b_pw2))
    yp, sp = _prompt_trunk(x_prompt, meta_tokens, norm_mix_g, norm_ffn_g, wm, wffn, pool_w,
                           pool_scale, wmla, wconv)
    ys, ss = _sample_trunk(x_sample, state_mlstm_C, state_mlstm_n, state_mlstm_m, state_pool,
                           cache_mla, page_table, state_conv, state_ffn, norm_mix_g, norm_ffn_g,
                           wm, wffn, pool_w, pool_scale, wmla, wconv)
    names = ("mlstm_C", "mlstm_n", "mlstm_m", "pool", "mla", "conv", "ffn")
    return (yp, ys) + tuple(sp[k] for k in names) + tuple(ss[k] for k in names)
```

```python
import functools
import math

import jax
import jax.numpy as jnp
from jax import lax
from jax.experimental import pallas as pl
from jax.experimental.pallas import tpu as pltpu

F32 = jnp.float32
BF16 = jnp.bfloat16

D_MODEL = 1024
N_META = 16
EPS = 1e-6
M_HEADS = 4
M_DK = 128
M_DV = 256
POOL_WINDOWS = (2, 4, 8, 16)
POOL_GDIM = 256
POOL_HIST = 15
MLA_HEADS = 16
MLA_NOPE = 64
MLA_ROPE = 32
MLA_QK = 96
MLA_V = 64
MLA_Q_LORA = 512
MLA_KV_LORA = 256
MLA_CACHE_DIM = 288
ROPE_BASE = 10000.0
CONV_WIDTH = 31
CONV_HIST = 30
D_FF = 2816
FFN_HIST = 2

LANE = 128
FF_CHUNK = 256
N_FF_CHUNK = D_FF // FF_CHUNK
NEG = -1e30
LOG2E = 1.4426950408889634
VMEM_LIMIT = 56 * 1024 * 1024


def _dot(a, b):
    return jnp.dot(a, b, preferred_element_type=F32)


def _dot_nt(a, b):
    return lax.dot_general(a, b, (((1,), (1,)), ((), ())), preferred_element_type=F32)


def _dot_tn(a, b):
    return lax.dot_general(a, b, (((0,), (0,)), ((), ())), preferred_element_type=F32)


def _rms(xf, g):
    return xf * lax.rsqrt(jnp.mean(xf * xf, axis=-1, keepdims=True) + EPS) * g


def _sigmoid(x):
    return 1.0 / (1.0 + jnp.exp(-x))


def _valid_rows(i, tm, tps, pad):
    r = (i % tps) * tm + lax.broadcasted_iota(jnp.int32, (tm, 1), 0)
    return r >= pad


def _params(n_axes):
    return pltpu.CompilerParams(dimension_semantics=("arbitrary",) * n_axes,
                                vmem_limit_bytes=VMEM_LIMIT)


def _row_spec(tm, c):
    return pl.BlockSpec((tm, c), lambda i: (i, 0))


def _full_spec(shape):
    zeros = (0,) * len(shape)
    return pl.BlockSpec(shape, lambda *_: zeros)


def _sds(shape, dtype):
    return jax.ShapeDtypeStruct(shape, dtype)


def _m_in_kernel(x_ref, g_ref, wq, wk, wv, wo, wgh, wgl, bg,
                 q_o, k_o, v_o, o_o, gt_o, *, tps, pad):
    tm = x_ref.shape[0]
    hf = _rms(x_ref[...], g_ref[...])
    h = hf.astype(BF16)
    q_o[...] = (_dot(h, wq[...]) * (M_DK ** -0.5)).astype(BF16)
    k_o[...] = _dot(h, wk[...]).astype(BF16)
    v_o[...] = _dot(h, wv[...]).astype(BF16)
    o_o[...] = _sigmoid(_dot(h, wo[...])).astype(BF16)
    hl = (hf - h.astype(F32)).astype(BF16)
    gts = _dot(h, wgh[...]) + _dot(hl, wgh[...]) + _dot(h, wgl[...]) + bg[...]
    lane = lax.broadcasted_iota(jnp.int32, gts.shape, 1)
    lf = jnp.minimum(gts, 0.0) - jnp.log(1.0 + jnp.exp(-jnp.abs(gts)))
    out = jnp.where(lane < M_HEADS, gts, lf)
    if pad:
        valid = _valid_rows(pl.program_id(0), tm, tps, pad)
        out = jnp.where(valid, out, jnp.where(lane < M_HEADS, NEG, 0.0))
    gt_o[...] = out


def _m_in(x, g, w, *, tm, tps, pad):
    r = x.shape[0]
    kern = functools.partial(_m_in_kernel, tps=tps, pad=pad)
    return pl.pallas_call(
        kern, grid=(r // tm,),
        in_specs=[_row_spec(tm, D_MODEL), _full_spec((1, D_MODEL)),
                  _full_spec(w["wq"].shape), _full_spec(w["wk"].shape), _full_spec(w["wv"].shape),
                  _full_spec(w["wo"].shape), _full_spec(w["wgh"].shape), _full_spec(w["wgl"].shape),
                  _full_spec((1, LANE))],
        out_specs=[_row_spec(tm, 512), _row_spec(tm, 512), _row_spec(tm, 1024),
                   _row_spec(tm, 1024), _row_spec(tm, LANE)],
        out_shape=[_sds((r, 512), BF16), _sds((r, 512), BF16), _sds((r, 1024), BF16),
                   _sds((r, 1024), BF16), _sds((r, LANE), F32)],
        compiler_params=_params(1), name="mlstm_in",
    )(x, g, w["wq"], w["wk"], w["wv"], w["wo"], w["wgh"], w["wgl"], w["bg"])


def _split3(a):
    a1 = a.astype(BF16)
    r1 = a - a1.astype(F32)
    a2 = r1.astype(BF16)
    a3 = (r1 - a2.astype(F32)).astype(BF16)
    return a1, a2, a3


def _m_chunk_kernel(q_ref, k_ref, v_ref, gc_ref, gr_ref, c0_ref, n0_ref, m0_ref,
                    h_ref, c_o, n_o, m_o, c_s, n_s, m_s, *, ln):
    c = pl.program_id(1)
    nseq = q_ref.shape[0] // ln

    @pl.when(c == 0)
    def _():
        c_s[...] = c0_ref[...]
        n_s[...] = n0_ref[...]
        m_s[...] = m0_ref[...]

    row = lax.broadcasted_iota(jnp.int32, (ln, ln), 0)
    col = lax.broadcasted_iota(jnp.int32, (ln, ln), 1)
    causal = col <= row
    tril = jnp.where(causal, 1.0, 0.0).astype(BF16)
    triu = jnp.where(row <= col, 1.0, 0.0).astype(BF16)
    for sq in range(nseq):
        rows = slice(sq * ln, (sq + 1) * ln)
        gc = gc_ref[rows, :]
        gr = gr_ref[sq]
        c1, c2, c3 = _split3(gc)
        b_col = _dot(tril, c1) + _dot(tril, c2) + _dot(tril, c3)
        r1, r2, r3 = _split3(gr)
        b_row = _dot(r1, triu) + _dot(r2, triu) + _dot(r3, triu)

        for hd in range(M_HEADS):
            ig_c = gc[:, hd:hd + 1]
            b_c = b_col[:, M_HEADS + hd:M_HEADS + hd + 1]
            ig_r = gr[hd:hd + 1, :]
            b_r = b_row[M_HEADS + hd:M_HEADS + hd + 1, :]
            m0 = m_s[sq, hd]
            c0 = c_s[sq, hd]
            n0 = n_s[sq, hd]
            dmat = jnp.where(causal, b_c - b_r + ig_r, NEG)
            inter = b_c + m0
            m_t = jnp.maximum(inter, jnp.max(dmat, axis=1, keepdims=True))
            w_intra = jnp.exp(dmat - m_t)
            w_inter = jnp.exp(inter - m_t)
            qh = q_ref[rows, hd * M_DK:(hd + 1) * M_DK]
            kh = k_ref[rows, hd * M_DK:(hd + 1) * M_DK]
            vh = v_ref[rows, hd * M_DV:(hd + 1) * M_DV]
            s = _dot_nt(qh, kh) * w_intra
            num = _dot(s.astype(BF16), vh) + w_inter * _dot(qh, c0.astype(BF16))
            den = (jnp.sum(s, axis=1, keepdims=True)
                   + w_inter * jnp.sum(qh.astype(F32) * n0, axis=1, keepdims=True))
            h_ref[rows, hd * M_DV:(hd + 1) * M_DV] = num / jnp.maximum(jnp.abs(den), jnp.exp(-m_t))
            b_last = b_c[ln - 1:ln, :]
            m_new = m_t[ln - 1:ln, :]
            wk_c = jnp.exp(b_last - b_c + ig_c - m_new)
            decay = jnp.exp(b_last + m0 - m_new)
            vw = (vh.astype(F32) * wk_c).astype(BF16)
            c_s[sq, hd] = decay * c0 + _dot_tn(kh, vw)
            n_s[sq, hd] = decay * n0 + jnp.sum(kh.astype(F32) * wk_c, axis=0, keepdims=True)
            m_s[sq, hd] = m_new

    @pl.when(c == pl.num_programs(1) - 1)
    def _():
        c_o[...] = c_s[...]
        n_o[...] = n_s[...]
        m_o[...] = m_s[...]


def _m_chunk(q, k, v, gcol, grow, c0, n0, m0, *, ln, nseq=1):
    n = c0.shape[0]
    t = q.shape[0] // n
    nc = t // ln
    assert n % nseq == 0 and (nseq == 1 or nc == 1)
    rows = lambda w: pl.BlockSpec((nseq * ln, w), lambda i, c: (i * nc + c, 0))
    st_c = pl.BlockSpec((nseq, M_HEADS, M_DK, M_DV), lambda i, c: (i, 0, 0, 0))
    st_n = pl.BlockSpec((nseq, M_HEADS, 1, M_DK), lambda i, c: (i, 0, 0, 0))
    st_m = pl.BlockSpec((nseq, M_HEADS, 1, 1), lambda i, c: (i, 0, 0, 0))
    return pl.pallas_call(
        functools.partial(_m_chunk_kernel, ln=ln), grid=(n // nseq, nc),
        in_specs=[rows(512), rows(512), rows(1024), rows(LANE),
                  pl.BlockSpec((nseq, 8, ln), lambda i, c: (i, 0, c)), st_c, st_n, st_m],
        out_specs=[rows(1024), st_c, st_n, st_m],
        out_shape=[_sds((n * t, 1024), F32), _sds((n, M_HEADS, M_DK, M_DV), F32),
                   _sds((n, M_HEADS, 1, M_DK), F32), _sds((n, M_HEADS, 1, 1), F32)],
        scratch_shapes=[pltpu.VMEM((nseq, M_HEADS, M_DK, M_DV), F32),
                        pltpu.VMEM((nseq, M_HEADS, 1, M_DK), F32),
                        pltpu.VMEM((nseq, M_HEADS, 1, 1), F32)],
        compiler_params=_params(2), name="mlstm_chunk",
    )(q, k, v, gcol, grow, c0, n0, m0)


def _m_out_kernel(hh_ref, o_ref, x_ref, g_ref, w_ref, y_ref, *, tps, pad):
    tm = x_ref.shape[0]
    parts = []
    for hd in range(M_HEADS):
        sl = slice(hd * M_DV, (hd + 1) * M_DV)
        hn = _rms(hh_ref[:, sl], g_ref[:, sl])
        parts.append((hn * o_ref[:, sl].astype(F32)).astype(BF16))
    y = x_ref[...] + _dot(jnp.concatenate(parts, axis=1), w_ref[...])
    if pad:
        y = jnp.where(_valid_rows(pl.program_id(0), tm, tps, pad), y, 0.0)
    y_ref[...] = y


def _m_out(hh, o, x, g, w, *, tm, tps, pad):
    r = x.shape[0]
    kern = functools.partial(_m_out_kernel, tps=tps, pad=pad)
    return pl.pallas_call(
        kern, grid=(r // tm,),
        in_specs=[_row_spec(tm, 1024), _row_spec(tm, 1024), _row_spec(tm, D_MODEL),
                  _full_spec((1, 1024)), _full_spec((1024, D_MODEL))],
        out_specs=_row_spec(tm, D_MODEL), out_shape=_sds((r, D_MODEL), F32),
        compiler_params=_params(1), name="mlstm_out",
    )(hh, o, x, g, w)


def _ffn_kernel(x_ref, g_ref, wg_ref, wu_ref, cw_ref, cb_ref, wd_ref,
                y_ref, tail_ref, carry_s, act_s, *, tps, pad):
    i = pl.program_id(0)
    tm = x_ref.shape[0]
    x = x_ref[...]
    h = _rms(x, g_ref[...]).astype(BF16)

    @pl.when(i % tps == 0)
    def _():
        carry_s[...] = jnp.zeros_like(carry_s)

    rowid = lax.broadcasted_iota(jnp.int32, (tm, 1), 0)

    def up(c, slot):
        g = _dot(h, wg_ref[c])
        u = _dot(h, wu_ref[c])
        prev = carry_s[c]
        g1 = jnp.where(rowid == 0, prev[7:8], pltpu.roll(g, 1, 0))
        g2 = jnp.where(rowid == 0, prev[6:7], jnp.where(rowid == 1, prev[7:8], pltpu.roll(g, 2, 0)))
        cw = cw_ref[c]
        gc = cw[0:1] * g2 + cw[1:2] * g1 + cw[2:3] * g + cb_ref[c]
        act_s[slot] = (gc * _sigmoid(gc) * u).astype(BF16)
        last = g[tm - 8:tm]
        carry_s[c] = last
        tail_ref[0, c] = last

    up(0, 0)
    y_ref[...] = x

    def pair(j, carry):
        c = 2 * j + 1
        d0 = _dot(act_s[0], wd_ref[c - 1])
        up(c, 1)
        d1 = _dot(act_s[1], wd_ref[c])
        up(c + 1, 0)
        y_ref[...] += d0 + d1
        return carry

    lax.fori_loop(0, (N_FF_CHUNK - 1) // 2, pair, 0)
    y = y_ref[...] + _dot(act_s[0], wd_ref[N_FF_CHUNK - 1])
    if pad:
        y = jnp.where(_valid_rows(i, tm, tps, pad), y, 0.0)
    y_ref[...] = y


def _ffn(x, g, w, *, tm, tps, pad):
    r = x.shape[0]
    nt = r // tm
    kern = functools.partial(_ffn_kernel, tps=tps, pad=pad)
    return pl.pallas_call(
        kern, grid=(nt,),
        in_specs=[_row_spec(tm, D_MODEL), _full_spec((1, D_MODEL)),
                  _full_spec(w["wg"].shape), _full_spec(w["wu"].shape), _full_spec(w["cw"].shape),
                  _full_spec(w["cb"].shape), _full_spec(w["wd"].shape)],
        out_specs=[_row_spec(tm, D_MODEL),
                   pl.BlockSpec((1, N_FF_CHUNK, 8, FF_CHUNK), lambda i: (i, 0, 0, 0))],
        out_shape=[_sds((r, D_MODEL), F32), _sds((nt, N_FF_CHUNK, 8, FF_CHUNK), F32)],
        scratch_shapes=[pltpu.VMEM((N_FF_CHUNK, 8, FF_CHUNK), F32), pltpu.VMEM((2, tm, FF_CHUNK), BF16)],
        compiler_params=_params(1), name="ffn",
    )(x, g, w["wg"], w["wu"], w["cw"], w["cb"], w["wd"])


def _ffn_s_kernel(x_ref, g_ref, hist_ref, wg_ref, wu_ref, cw_ref, cb_ref, wd_ref,
                  y_ref, tail_ref, acc_s, *, nb):
    c = pl.program_id(0)
    x = x_ref[...]
    h = _rms(x, g_ref[...]).astype(BF16)

    @pl.when(c == 0)
    def _():
        acc_s[...] = jnp.zeros_like(acc_s)

    g = _dot(h, wg_ref[0])
    u = _dot(h, wu_ref[0])
    hist = hist_ref[...]
    t = g.shape[0]
    ext0 = jnp.concatenate([hist, g[:t - 2 * nb]], axis=0)
    ext1 = jnp.concatenate([hist[nb:], g[:t - nb]], axis=0)
    cw = cw_ref[0]
    gc = cw[0:1] * ext0 + cw[1:2] * ext1 + cw[2:3] * g + cb_ref[0]
    act = (gc * _sigmoid(gc) * u).astype(BF16)
    acc_s[...] += _dot(act, wd_ref[0])
    tail_ref[...] = g[t - 2 * nb:]

    @pl.when(c == pl.num_programs(0) - 1)
    def _():
        y_ref[...] = x + acc_s[...]


def _ffn_s(x, g, hist, w, *, nb):
    r = x.shape[0]
    kern = functools.partial(_ffn_s_kernel, nb=nb)
    chunk3 = lambda a, b: pl.BlockSpec((1, a, b), lambda c: (c, 0, 0))
    return pl.pallas_call(
        kern, grid=(N_FF_CHUNK,),
        in_specs=[_full_spec((r, D_MODEL)), _full_spec((1, D_MODEL)),
                  pl.BlockSpec((2 * nb, FF_CHUNK), lambda c: (0, c)),
                  chunk3(D_MODEL, FF_CHUNK), chunk3(D_MODEL, FF_CHUNK), chunk3(8, FF_CHUNK),
                  chunk3(1, FF_CHUNK), chunk3(FF_CHUNK, D_MODEL)],
        out_specs=[_full_spec((r, D_MODEL)), pl.BlockSpec((2 * nb, FF_CHUNK), lambda c: (0, c))],
        out_shape=[_sds((r, D_MODEL), F32), _sds((2 * nb, D_FF), F32)],
        scratch_shapes=[pltpu.VMEM((r, D_MODEL), F32)],
        compiler_params=_params(1), name="ffn_sample",
    )(x, g, hist, w["wg"], w["wu"], w["cw"], w["cb"], w["wd"])


def _pool_kernel(x_ref, g_ref, pw_ref, ps_ref, y_ref, tail_ref, ext_s, *, tps, pad):
    i = pl.program_id(0)
    tm = x_ref.shape[0]
    x = x_ref[...]
    u = _rms(x, g_ref[...])
    hist = POOL_HIST + 1

    @pl.when(i % tps == 0)
    def _():
        ext_s[0:hist] = jnp.zeros((hist, D_MODEL), F32)

    @pl.when(i % tps != 0)
    def _():
        ext_s[0:hist] = ext_s[tm:tm + hist]

    ext_s[hist:hist + tm] = u
    tail_ref[0] = u[tm - hist:]
    pos = (i % tps) * tm + lax.broadcasted_iota(jnp.int32, (tm, 1), 0) - pad
    valid = pos >= 0
    for gi, w in enumerate(POOL_WINDOWS):
        sl = slice(gi * POOL_GDIM, (gi + 1) * POOL_GDIM)
        ug = u[:, sl]
        win = ug
        for j in range(1, w):
            win = win + ext_s[hist - j:hist - j + tm, sl]
        cnt = jnp.maximum(jnp.minimum(pos + 1, w), 1).astype(F32)
        d = (win / cnt - ug).astype(BF16)
        y = x[:, sl] + _dot(d, pw_ref[gi]) * ps_ref[:, sl]
        y_ref[:, sl] = jnp.where(valid, y, 0.0)


def _pool(x, g, pw, ps, *, tm, tps, pad):
    r = x.shape[0]
    nt = r // tm
    kern = functools.partial(_pool_kernel, tps=tps, pad=pad)
    return pl.pallas_call(
        kern, grid=(nt,),
        in_specs=[_row_spec(tm, D_MODEL), _full_spec((1, D_MODEL)),
                  _full_spec((4, POOL_GDIM, POOL_GDIM)), _full_spec((1, D_MODEL))],
        out_specs=[_row_spec(tm, D_MODEL), pl.BlockSpec((1, 16, D_MODEL), lambda i: (i, 0, 0))],
        out_shape=[_sds((r, D_MODEL), F32), _sds((nt, 16, D_MODEL), F32)],
        scratch_shapes=[pltpu.VMEM((tm + 16, D_MODEL), F32)],
        compiler_params=_params(1), name="pool",
    )(x, g, pw, ps)


def _pool_s_kernel(x_ref, g_ref, hist_ref, pw_ref, ps_ref, y_ref, u_ref, *, nb, past_len):
    x = x_ref[...]
    u = _rms(x, g_ref[...])
    u_ref[...] = u
    nt = x.shape[0] // nb
    for gi, w in enumerate(POOL_WINDOWS):
        sl = slice(gi * POOL_GDIM, (gi + 1) * POOL_GDIM)
        outs = []
        for t in range(nt):
            win = u[t * nb:(t + 1) * nb, sl]
            for j in range(1, w):
                src = t - j
                if src >= 0:
                    win = win + u[src * nb:(src + 1) * nb, sl]
                else:
                    hrow = POOL_HIST + src
                    win = win + hist_ref[hrow * nb:(hrow + 1) * nb, sl]
            cnt = float(min(past_len + t + 1, w))
            outs.append(win / cnt - u[t * nb:(t + 1) * nb, sl])
        d = jnp.concatenate(outs, axis=0).astype(BF16)
        y_ref[:, sl] = x[:, sl] + _dot(d, pw_ref[gi]) * ps_ref[:, sl]


def _pool_s(x, g, hist, pw, ps, *, nb, past_len):
    r = x.shape[0]
    kern = functools.partial(_pool_s_kernel, nb=nb, past_len=past_len)
    return pl.pallas_call(
        kern, grid=(1,),
        in_specs=[_full_spec((r, D_MODEL)), _full_spec((1, D_MODEL)), _full_spec(hist.shape),
                  _full_spec((4, POOL_GDIM, POOL_GDIM)), _full_spec((1, D_MODEL))],
        out_specs=[_full_spec((r, D_MODEL)), _full_spec((r, D_MODEL))],
        out_shape=[_sds((r, D_MODEL), F32), _sds((r, D_MODEL), F32)],
        compiler_params=_params(1), name="pool_sample",
    )(x, g, hist, pw, ps)


def _rope_norm(z, g, tc, ts1, ts2):
    zn = z * lax.rsqrt(jnp.sum(z * z, axis=-1, keepdims=True) * (1.0 / MLA_QK) + EPS) * g
    return zn * tc + pltpu.roll(zn, 16, 1) * ts1 + pltpu.roll(zn, LANE - 16, 1) * ts2


def _mla_in_kernel(x_ref, g_ref, wdq, qlg, wuq, qng, wlat, wkpe, kvg, wuk, wuv, kng,
                   tc_ref, ts1_ref, ts2_ref, qh_o, kh_o, vp_o, lat_o, kpe_o):
    h = _rms(x_ref[...], g_ref[...]).astype(BF16)
    cqn = _rms(_dot(h, wdq[...]), qlg[...]).astype(BF16)
    latn = _rms(_dot(h, wlat[...]), kvg[...])
    lat_o[...] = latn
    latb = latn.astype(BF16)
    kpe = _dot(h, wkpe[...])
    kpe_o[...] = kpe
    tc, ts1, ts2 = tc_ref[...], ts1_ref[...], ts2_ref[...]
    for j in range(MLA_HEADS // 2):
        q2 = _dot(cqn, wuq[j])
        k2 = _dot(latb, wuk[j])
        for half in range(2):
            sl = slice(half * LANE, (half + 1) * LANE)
            qh_o[2 * j + half] = _rope_norm(q2[:, sl], qng[...], tc, ts1, ts2).astype(qh_o.dtype)
            kh_o[2 * j + half] = _rope_norm(k2[:, sl] + kpe, kng[...], tc, ts1, ts2).astype(BF16)
    ones = jnp.ones((latb.shape[0], LANE), BF16)
    for j in range(MLA_HEADS // 4):
        v4 = _dot(latb, wuv[j]).astype(BF16)
        vp_o[2 * j] = jnp.concatenate([v4[:, :LANE], ones], axis=1)
        vp_o[2 * j + 1] = jnp.concatenate([v4[:, LANE:], ones], axis=1)


def _mla_in(x, g, w, tabs, *, tm, qdtype):
    r = x.shape[0]
    heads = lambda n, w=LANE: pl.BlockSpec((n, tm, w), lambda i: (0, i, 0))
    tps = tabs[0].shape[0] // tm
    tab = pl.BlockSpec((tm, LANE), lambda i: (i % tps, 0))
    return pl.pallas_call(
        _mla_in_kernel, grid=(r // tm,),
        in_specs=[_row_spec(tm, D_MODEL), _full_spec((1, D_MODEL)),
                  _full_spec(w["wdq"].shape), _full_spec((1, MLA_Q_LORA)), _full_spec(w["wuq"].shape),
                  _full_spec((1, LANE)), _full_spec(w["wlat"].shape), _full_spec(w["wkpe"].shape),
                  _full_spec((1, MLA_KV_LORA)), _full_spec(w["wuk"].shape), _full_spec(w["wuv"].shape),
                  _full_spec((1, LANE)), tab, tab, tab],
        out_specs=[heads(MLA_HEADS), heads(MLA_HEADS), heads(MLA_HEADS // 2, 2 * LANE),
                   _row_spec(tm, MLA_KV_LORA), _row_spec(tm, LANE)],
        out_shape=[_sds((MLA_HEADS, r, LANE), qdtype), _sds((MLA_HEADS, r, LANE), BF16),
                   _sds((MLA_HEADS // 2, r, 2 * LANE), BF16), _sds((r, MLA_KV_LORA), F32),
                   _sds((r, LANE), F32)],
        compiler_params=_params(1), name="mla_in",
    )(x, g, w["wdq"], w["qlg"], w["wuq"], w["qng"], w["wlat"], w["wkpe"], w["kvg"],
      w["wuk"], w["wuv"], w["kng"], *tabs)


def _attn_kernel(q_ref, k_ref, v_ref, o_ref, acc_s, m_s, *, pad, nsub):
    qi = pl.program_id(2)
    tq = q_ref.shape[1]
    tk = tq
    tqs = tq // nsub
    for hh in range(2):
        m_s[hh] = jnp.full((tq, LANE), NEG, F32)
        acc_s[hh] = jnp.zeros((tq, 2 * LANE), F32)

    def block(ki, masked):
        ks = pl.multiple_of(ki * tk, tk)
        vblk = v_ref[0, pl.ds(ks, tk), :]
        for qs in range(nsub):
            rows = slice(qs * tqs, (qs + 1) * tqs)
            for hh in range(2):
                s = _dot_nt(q_ref[hh, rows, :], k_ref[hh, pl.ds(ks, tk), :])
                if masked:
                    qpos = qi * tq + qs * tqs + lax.broadcasted_iota(jnp.int32, (tqs, tk), 0)
                    kpos = ki * tk + lax.broadcasted_iota(jnp.int32, (tqs, tk), 1)
                    s = jnp.where(kpos <= qpos, jnp.where(kpos >= pad, s, NEG), NEG)
                m_old = m_s[hh, rows, :]
                m_new = jnp.maximum(m_old, jnp.max(s, axis=1, keepdims=True))
                p = jnp.exp2(s - jnp.tile(m_new, (1, tk // LANE)))
                alpha = jnp.exp2(m_old - m_new)
                acc_s[hh, rows, :] = (jnp.tile(alpha, (1, 2)) * acc_s[hh, rows, :]
                                      + _dot(p.astype(BF16), vblk))
                m_s[hh, rows, :] = m_new

    block(0, True)

    def body(ki, carry):
        block(ki, False)
        return carry

    lax.fori_loop(1, qi, body, 0)

    @pl.when(qi > 0)
    def _():
        block(qi, True)

    lane = lax.broadcasted_iota(jnp.int32, (tq, LANE), 1)
    a0 = acc_s[0]
    a1 = acc_s[1]
    o_ref[...] = jnp.where(lane < MLA_V, a0[:, :LANE] / a0[:, LANE:],
                           a1[:, :LANE] / a1[:, LANE:]).astype(BF16)


def _attn(qh, kh, vp, *, n, tp, tq, pad):
    nq = tp // tq
    kern = functools.partial(_attn_kernel, pad=pad, nsub=2)
    return pl.pallas_call(
        kern, grid=(n, MLA_HEADS // 2, nq),
        in_specs=[pl.BlockSpec((2, tq, LANE), lambda b, hp, qi: (hp, b * nq + qi, 0)),
                  pl.BlockSpec((2, tp, LANE), lambda b, hp, qi: (hp, b, 0)),
                  pl.BlockSpec((1, tp, 2 * LANE), lambda b, hp, qi: (hp, b, 0))],
        out_specs=pl.BlockSpec((tq, LANE), lambda b, hp, qi: (b * nq + qi, hp)),
        out_shape=_sds((n * tp, MLA_HEADS * MLA_V), BF16),
        scratch_shapes=[pltpu.VMEM((2, tq, 2 * LANE), F32), pltpu.VMEM((2, tq, LANE), F32)],
        compiler_params=_params(3), name="attn",
    )(qh, kh, vp)


def _proj_out_kernel(o_ref, x_ref, w_ref, y_ref, *, tps, pad):
    tm = x_ref.shape[0]
    y = x_ref[...] + _dot(o_ref[...], w_ref[...])
    if pad:
        y = jnp.where(_valid_rows(pl.program_id(0), tm, tps, pad), y, 0.0)
    y_ref[...] = y


def _proj_out(o, x, w, *, tm, tps, pad):
    r = x.shape[0]
    kern = functools.partial(_proj_out_kernel, tps=tps, pad=pad)
    return pl.pallas_call(
        kern, grid=(r // tm,),
        in_specs=[_row_spec(tm, 1024), _row_spec(tm, D_MODEL), _full_spec((1024, D_MODEL))],
        out_specs=_row_spec(tm, D_MODEL), out_shape=_sds((r, D_MODEL), F32),
        compiler_params=_params(1), name="mla_out",
    )(o, x, w)


def _dec_prep_kernel(qh_ref, kng, wukt, qabs_o, ab_o):
    g = kng[...]
    g1 = g[:, MLA_NOPE:MLA_NOPE + 16]
    g2 = g[:, MLA_NOPE + 16:MLA_QK]
    for hd in range(MLA_HEADS):
        qh = qh_ref[hd]
        qabs_o[hd] = _dot((qh * g).astype(BF16), wukt[hd]).astype(BF16)
        q1 = qh[:, MLA_NOPE:MLA_NOPE + 16]
        q2 = qh[:, MLA_NOPE + 16:MLA_QK]
        ab_o[hd] = jnp.concatenate([g1 * q1, g2 * q2, g1 * q2, -(g2 * q1)], axis=1).astype(BF16)


def _dec_prep(qh, kng, wukt):
    r = qh.shape[1]
    return pl.pallas_call(
        _dec_prep_kernel, grid=(1,),
        in_specs=[_full_spec(qh.shape), _full_spec((1, LANE)), _full_spec(wukt.shape)],
        out_specs=[_full_spec((MLA_HEADS, r, MLA_KV_LORA)), _full_spec((MLA_HEADS, r, 64))],
        out_shape=[_sds((MLA_HEADS, r, MLA_KV_LORA), BF16), _sds((MLA_HEADS, r, 64), BF16)],
        compiler_params=_params(1), name="decode_prep",
    )(qh, kng, wukt)


def _decode_kernel(pt_ref, new_ref, qm_ref, ab_ref, wuk_ref, cs_ref, *rest,
                   n_pages, page, pages_per_chunk):
    del pt_ref
    page_refs = rest[:n_pages]
    (o_ref,) = rest[n_pages:]
    nq = qm_ref.shape[1]
    lhs = jnp.concatenate([qm_ref[0], wuk_ref[...]], axis=0)
    ab = ab_ref[0]
    ck = pages_per_chunk * page
    n_chunk = n_pages // pages_per_chunk
    past = n_pages * page
    nn = new_ref.shape[2]

    def scores(lat, kpe, cs):
        n = lat.shape[1]
        latb = lat.astype(BF16)
        kcs = (jnp.concatenate([kpe, kpe], axis=0) * cs).astype(BF16)
        big = _dot(lhs, latb)
        raw = big[:nq] + _dot(ab, kcs)
        kn = big[nq:]
        ss = jnp.sum((kn * kn).reshape(MLA_NOPE, MLA_HEADS, n), axis=0)
        ss = ss + jnp.sum(kpe * kpe, axis=0, keepdims=True)
        r = lax.rsqrt(ss * (1.0 / MLA_QK) + EPS)
        return latb, jnp.concatenate([r] * (nq // MLA_HEADS), axis=0) * raw

    def fold(state, latb, s):
        m, l, acc = state
        m_new = jnp.maximum(m, jnp.max(s, axis=1, keepdims=True))
        p = jnp.exp2(s - m_new)
        alpha = jnp.exp2(m - m_new)
        l = alpha * l + jnp.sum(p, axis=1, keepdims=True)
        acc = alpha * acc + _dot_nt(p.astype(BF16), latb)
        return m_new, l, acc

    state = (jnp.full((nq, 1), NEG, F32), jnp.zeros((nq, 1), F32),
             jnp.zeros((nq, MLA_KV_LORA), F32))
    for c in range(n_chunk):
        blks = [page_refs[c * pages_per_chunk + p][0] for p in range(pages_per_chunk)]
        lat = jnp.concatenate([b[:MLA_KV_LORA, :] for b in blks], axis=1)
        kpe = jnp.concatenate([b[MLA_KV_LORA:, :] for b in blks], axis=1)
        latb, s = scores(lat, kpe, cs_ref[:, c * ck:(c + 1) * ck])
        state = fold(state, latb, s)
    blk = new_ref[0]
    latb, s = scores(blk[:MLA_KV_LORA, :], blk[MLA_KV_LORA:, :], cs_ref[:, past:past + nn])
    key = lax.broadcasted_iota(jnp.int32, (nq, nn), 1)
    tok = lax.broadcasted_iota(jnp.int32, (nq, nn), 0) // MLA_HEADS
    _, l, acc = fold(state, latb, jnp.where(key <= tok, s, NEG))
    o_ref[0] = acc / l


def _decode(page_table, cache_t, new_t, qm, ab, wuk, cs, *, pages_per_chunk=16):
    nb, n_pages = page_table.shape
    assert n_pages % pages_per_chunk == 0
    page = cache_t.shape[2]
    nn = new_t.shape[2]
    nq = qm.shape[1]
    kern = functools.partial(_decode_kernel, n_pages=n_pages, page=page,
                             pages_per_chunk=pages_per_chunk)
    per_seq = lambda shape: pl.BlockSpec((1,) + shape, lambda b, pt: (b, 0, 0))
    page_specs = [pl.BlockSpec((1, MLA_CACHE_DIM, page),
                               lambda b, pt, j=j: (pt[b * n_pages + j], 0, 0))
                  for j in range(n_pages)]
    grid_spec = pltpu.PrefetchScalarGridSpec(
        num_scalar_prefetch=1, grid=(nb,),
        in_specs=[per_seq((MLA_CACHE_DIM, nn)), per_seq((nq, MLA_KV_LORA)), per_seq((nq, 64)),
                  pl.BlockSpec(wuk.shape, lambda b, pt: (0, 0)),
                  pl.BlockSpec(cs.shape, lambda b, pt: (0, 0))] + page_specs,
        out_specs=per_seq((nq, MLA_KV_LORA)))
    return pl.pallas_call(
        kern, grid_spec=grid_spec, out_shape=_sds((nb, nq, MLA_KV_LORA), F32),
        compiler_params=_params(1), name="decode",
    )(page_table.reshape(-1), new_t, qm, ab, wuk, cs, *([cache_t] * n_pages))


def _dec_out_kernel(ol_ref, wuv, wo, x_ref, y_ref):
    parts = [_dot(ol_ref[hd].astype(BF16), wuv[hd]) for hd in range(MLA_HEADS)]
    o = jnp.concatenate(parts, axis=1).astype(BF16)
    y_ref[...] = x_ref[...] + _dot(o, wo[...])


def _dec_out(ol, wuv, wo, x):
    r = x.shape[0]
    return pl.pallas_call(
        _dec_out_kernel, grid=(1,),
        in_specs=[_full_spec(ol.shape), _full_spec(wuv.shape),
                  _full_spec(wo.shape), _full_spec((r, D_MODEL))],
        out_specs=_full_spec((r, D_MODEL)), out_shape=_sds((r, D_MODEL), F32),
        compiler_params=_params(1), name="decode_out",
    )(ol, wuv, wo, x)


def _conv_in_kernel(x_ref, g_ref, w_ref, b_ref, gl_ref, *, tps, pad):
    tm = x_ref.shape[0]
    h = _rms(x_ref[...], g_ref[...]).astype(BF16)
    a = _dot(h, w_ref[...]) + b_ref[...]
    gl = a[:, :D_MODEL] * _sigmoid(a[:, D_MODEL:])
    if pad:
        gl = jnp.where(_valid_rows(pl.program_id(0), tm, tps, pad), gl, 0.0)
    gl_ref[...] = gl


def _conv_in(x, g, w, b, *, tm, tps, pad):
    r = x.shape[0]
    kern = functools.partial(_conv_in_kernel, tps=tps, pad=pad)
    return pl.pallas_call(
        kern, grid=(r // tm,),
        in_specs=[_row_spec(tm, D_MODEL), _full_spec((1, D_MODEL)),
                  _full_spec((D_MODEL, 2 * D_MODEL)), _full_spec((1, 2 * D_MODEL))],
        out_specs=_row_spec(tm, D_MODEL), out_shape=_sds((r, D_MODEL), F32),
        compiler_params=_params(1), name="conv_in",
    )(x, g, w, b)


def _ln_silu_proj(c, x, lng, lnb, w2, b2):
    mu = jnp.mean(c, axis=-1, keepdims=True)
    cc = c - mu
    var = jnp.mean(cc * cc, axis=-1, keepdims=True)
    y = cc * lax.rsqrt(var + EPS) * lng + lnb
    a = (y * _sigmoid(y)).astype(BF16)
    return x + _dot(a, w2) + b2


CONV_COLS = 256
CONV_ROWS = 64


def _conv_out_kernel(gl_ref, x_ref, wdw, bdw, lng, lnb, w2, b2, y_ref, ext_s, sh_s, c_s, *, tps, pad):
    i = pl.program_id(0)
    tm = x_ref.shape[0]
    hist = 32
    n_ext = tm + hist

    @pl.when(i % tps == 0)
    def _():
        ext_s[0:hist] = jnp.zeros((hist, D_MODEL), F32)

    @pl.when(i % tps != 0)
    def _():
        ext_s[0:hist] = ext_s[tm:tm + hist]

    ext_s[hist:n_ext] = gl_ref[...]
    ext_s[n_ext:n_ext + 8] = jnp.zeros((8, D_MODEL), F32)
    off = hist - CONV_HIST
    for cb in range(D_MODEL // CONV_COLS):
        cols = slice(cb * CONV_COLS, (cb + 1) * CONV_COLS)
        for s in range(8):
            sh_s[s] = ext_s[s:s + n_ext, cols]
        wblk = wdw[:, cols]
        bias = bdw[:, cols]

        def rows(rb, carry):
            r0 = pl.multiple_of(rb * CONV_ROWS, CONV_ROWS)
            acc = jnp.zeros((CONV_ROWS, CONV_COLS), F32)
            for j in range(CONV_WIDTH):
                a, s = divmod(off + j, 8)
                acc = acc + wblk[j:j + 1] * sh_s[s, pl.ds(r0 + 8 * a, CONV_ROWS), :]
            c_s[pl.ds(r0, CONV_ROWS), cols] = acc + bias
            return carry

        lax.fori_loop(0, tm // CONV_ROWS, rows, 0)
    y = _ln_silu_proj(c_s[...], x_ref[...], lng[...], lnb[...], w2[...], b2[...])
    if pad:
        y = jnp.where(_valid_rows(i, tm, tps, pad), y, 0.0)
    y_ref[...] = y


def _conv_out(gl, x, w, *, tm, tps, pad):
    r = x.shape[0]
    kern = functools.partial(_conv_out_kernel, tps=tps, pad=pad)
    vec = _full_spec((1, D_MODEL))
    return pl.pallas_call(
        kern, grid=(r // tm,),
        in_specs=[_row_spec(tm, D_MODEL), _row_spec(tm, D_MODEL), _full_spec((32, D_MODEL)),
                  vec, vec, vec, _full_spec((D_MODEL, D_MODEL)), vec],
        out_specs=_row_spec(tm, D_MODEL), out_shape=_sds((r, D_MODEL), F32),
        scratch_shapes=[pltpu.VMEM((tm + 40, D_MODEL), F32), pltpu.VMEM((8, tm + 32, CONV_COLS), F32),
                        pltpu.VMEM((tm, D_MODEL), F32)],
        compiler_params=_params(1), name="conv_out",
    )(gl, x, w["wdw"], w["bdw"], w["lng"], w["lnb"], w["w2"], w["b2"])


def _conv_out_s_kernel(hist_ref, gl_ref, x_ref, wdw, bdw, lng, lnb, w2, b2, y_ref, *, nb):
    nt = x_ref.shape[0] // nb
    outs = []
    for t in range(nt):
        acc = None
        for j in range(CONV_WIDTH):
            src = t + j
            if src < CONV_HIST:
                slab = hist_ref[src * nb:(src + 1) * nb, :]
            else:
                slab = gl_ref[(src - CONV_HIST) * nb:(src - CONV_HIST + 1) * nb, :]
            term = wdw[j:j + 1, :] * slab
            acc = term if acc is None else acc + term
        outs.append(acc)
    c = jnp.concatenate(outs, axis=0) + bdw[...]
    y_ref[...] = _ln_silu_proj(c, x_ref[...], lng[...], lnb[...], w2[...], b2[...])


def _conv_out_s(hist, gl, x, w, *, nb):
    r = x.shape[0]
    kern = functools.partial(_conv_out_s_kernel, nb=nb)
    vec = _full_spec((1, D_MODEL))
    return pl.pallas_call(
        kern, grid=(1,),
        in_specs=[_full_spec(hist.shape), _full_spec((r, D_MODEL)), _full_spec((r, D_MODEL)),
                  _full_spec((32, D_MODEL)), vec, vec, vec, _full_spec((D_MODEL, D_MODEL)), vec],
        out_specs=_full_spec((r, D_MODEL)), out_shape=_sds((r, D_MODEL), F32),
        compiler_params=_params(1), name="conv_out_sample",
    )(hist, gl, x, w["wdw"], w["bdw"], w["lng"], w["lnb"], w["w2"], w["b2"])


def _pad_lanes(a, width=LANE):
    return jnp.pad(a, [(0, 0)] * (a.ndim - 1) + [(0, width - a.shape[-1])])


def _row(a):
    return a.reshape(1, -1).astype(F32)


def _prep_mlstm(m_w_in, m_b_gates, m_head_norm_g, m_w_out):
    i0 = M_HEADS * M_DK
    i1 = 2 * i0
    i2 = i1 + M_HEADS * M_DV
    i3 = i2 + M_HEADS * M_DV
    wgate = _pad_lanes(m_w_in[:, i3:])
    wgh = wgate.astype(BF16)
    wgl = (wgate - wgh.astype(F32)).astype(BF16)
    return dict(wq=m_w_in[:, :i0].astype(BF16), wk=m_w_in[:, i0:i1].astype(BF16),
                wv=m_w_in[:, i1:i2].astype(BF16), wo=m_w_in[:, i2:i3].astype(BF16),
                wgh=wgh, wgl=wgl, bg=_pad_lanes(_row(m_b_gates)),
                hg=_row(m_head_norm_g), wout=m_w_out.astype(BF16))


def _prep_ffn(w_up, conv_w, conv_b, w_down):
    chunks = lambda a: a.reshape(D_MODEL, N_FF_CHUNK, FF_CHUNK).transpose(1, 0, 2).astype(BF16)
    cw = jnp.pad(conv_w, ((0, 8 - conv_w.shape[0]), (0, 0)))
    return dict(wg=chunks(w_up[:, :D_FF]), wu=chunks(w_up[:, D_FF:]),
                cw=cw.reshape(8, N_FF_CHUNK, FF_CHUNK).transpose(1, 0, 2).astype(F32),
                cb=conv_b.reshape(N_FF_CHUNK, 1, FF_CHUNK).astype(F32),
                wd=w_down.reshape(N_FF_CHUNK, FF_CHUNK, D_MODEL).astype(BF16))


def _prep_mla(mla_w_dq, mla_q_lora_g, mla_w_uq, mla_w_dkv, mla_kv_lora_g, mla_w_ukv,
              mla_q_norm_g, mla_k_norm_g, mla_w_o):
    pair = lambda a: (a.reshape(a.shape[0], MLA_HEADS // 2, 2 * a.shape[2]).transpose(1, 0, 2))
    wuq = pair(_pad_lanes(mla_w_uq.reshape(MLA_Q_LORA, MLA_HEADS, MLA_QK)))
    ukv = mla_w_ukv.reshape(MLA_KV_LORA, MLA_HEADS, MLA_NOPE + MLA_V)
    uk = ukv[:, :, :MLA_NOPE]
    uv = ukv[:, :, MLA_NOPE:]
    wkpe = jnp.zeros((D_MODEL, LANE), F32).at[:, MLA_NOPE:MLA_QK].set(mla_w_dkv[:, MLA_KV_LORA:])
    qscale = (MLA_QK ** -0.5) * LOG2E
    return dict(
        wdq=mla_w_dq.astype(BF16), qlg=_row(mla_q_lora_g), wuq=wuq.astype(BF16),
        qng=_pad_lanes(_row(mla_q_norm_g)) * qscale,
        wlat=mla_w_dkv[:, :MLA_KV_LORA].astype(BF16), wkpe=wkpe.astype(BF16),
        kvg=_row(mla_kv_lora_g), wuk=pair(_pad_lanes(uk)).astype(BF16),
        wuv=uv.reshape(MLA_KV_LORA, MLA_HEADS // 4, 4 * MLA_V).transpose(1, 0, 2).astype(BF16),
        kng=_pad_lanes(_row(mla_k_norm_g)),
        wukt=jnp.pad(uk.transpose(1, 2, 0), ((0, 0), (0, LANE - MLA_NOPE), (0, 0))).astype(BF16),
        wuk_rows=uk.transpose(2, 1, 0).reshape(MLA_NOPE * MLA_HEADS, MLA_KV_LORA).astype(BF16),
        wuv_h=uv.transpose(1, 0, 2).astype(BF16), wo=mla_w_o.astype(BF16))


def _rope_tables(pos):
    half = MLA_ROPE // 2
    inv = ROPE_BASE ** (-jnp.arange(half, dtype=F32) / half)
    ang = pos.astype(F32)[:, None] * inv[None, :]
    cos, sin = jnp.cos(ang), jnp.sin(ang)
    r = pos.shape[0]
    ones = jnp.ones((r, MLA_NOPE), F32)
    z16 = jnp.zeros((r, half), F32)
    z32 = jnp.zeros((r, LANE - MLA_QK), F32)
    z64 = jnp.zeros((r, MLA_NOPE), F32)
    tc = jnp.concatenate([ones, cos, cos, z32], axis=1)
    ts1 = jnp.concatenate([z64, z16, sin, z32], axis=1)
    ts2 = jnp.concatenate([z64, -sin, z16, z32], axis=1)
    cs = jnp.concatenate([cos, cos, sin, sin], axis=1)
    return (tc, ts1, ts2), cs


def _prompt_trunk(x_prompt, meta_tokens, norm_mix_g, norm_ffn_g, wm, wffn, pool_w, pool_scale,
                  wmla, wconv):
    n, seq, _ = x_prompt.shape
    t_real = seq + N_META
    tm = 768
    tp = -(-t_real // tm) * tm
    pad = tp - t_real
    tps = tp // tm
    meta = jnp.broadcast_to(meta_tokens[None].astype(F32), (n, N_META, D_MODEL))
    x = jnp.concatenate([jnp.zeros((n, pad, D_MODEL), F32), meta, x_prompt], axis=1)
    x = x.reshape(n * tp, D_MODEL)
    kw = dict(tm=tm, tps=tps, pad=pad)
    st = {}
    ffn_hist = []

    def ffn(x, layer):
        y, tail = _ffn(x, _row(norm_ffn_g[layer]), wffn[layer], **kw)
        tail = tail.reshape(n, tps, N_FF_CHUNK, 8, FF_CHUNK)[:, -1, :, 8 - FFN_HIST:, :]
        ffn_hist.append(tail.transpose(0, 2, 1, 3).reshape(n, FFN_HIST, D_FF))
        return y

    q, k, v, o, gt = _m_in(x, _row(norm_mix_g[0]), wm, **kw)
    grow = gt[:, :8].reshape(n, tp, 8).transpose(0, 2, 1)
    zc = jnp.zeros((n, M_HEADS, M_DK, M_DV), F32)
    zn = jnp.zeros((n, M_HEADS, 1, M_DK), F32)
    zm = jnp.zeros((n, M_HEADS, 1, 1), F32)
    hh, c_new, n_new, m_new = _m_chunk(q, k, v, gt, grow, zc, zn, zm, ln=tm // 2)
    st["mlstm_C"] = c_new
    st["mlstm_n"] = n_new.reshape(n, M_HEADS, M_DK)
    st["mlstm_m"] = m_new.reshape(n, M_HEADS)
    x = _m_out(hh, o, x, wm["hg"], wm["wout"], **kw)
    x = ffn(x, 0)

    x, tail = _pool(x, _row(norm_mix_g[1]), pool_w.astype(BF16), _row(pool_scale), **kw)
    st["pool"] = tail.reshape(n, tps, 16, D_MODEL)[:, -1, 16 - POOL_HIST:, :]
    x = ffn(x, 1)

    pos = jnp.maximum(jnp.arange(tp, dtype=jnp.int32) - pad, 0)
    tabs, _ = _rope_tables(pos)
    qh, kh, vp, lat, kpe = _mla_in(x, _row(norm_mix_g[2]), wmla, tabs, tm=tm, qdtype=BF16)
    rows = jnp.concatenate([lat, kpe[:, MLA_NOPE:MLA_QK]], axis=1)
    st["mla"] = rows.reshape(n, tp, MLA_CACHE_DIM)[:, pad:, :]
    o = _attn(qh, kh, vp, n=n, tp=tp, tq=tm, pad=pad)
    x = _proj_out(o, x, wmla["wo"], **kw)
    x = ffn(x, 2)

    gl = _conv_in(x, _row(norm_mix_g[3]), wconv["w1"], wconv["b1"], **kw)
    st["conv"] = gl.reshape(n, tp, D_MODEL)[:, tp - CONV_HIST:, :]
    x = _conv_out(gl, x, wconv, **kw)
    x = ffn(x, 3)

    st["ffn"] = jnp.stack(ffn_hist, axis=0)
    y = x.reshape(n, tp, D_MODEL)[:, pad + N_META:, :]
    return y, st


def _sample_trunk(x_sample, state_mlstm_C, state_mlstm_n, state_mlstm_m, state_pool, cache_mla,
                  page_table, state_conv, state_ffn, norm_mix_g, norm_ffn_g, wm, wffn, pool_w,
                  pool_scale, wmla, wconv):
    nb, nt, _ = x_sample.shape
    r = nb * nt
    past_len = page_table.shape[1] * cache_mla.shape[1]
    tmaj = lambda a: a.transpose(1, 0, 2).reshape(a.shape[1] * nb, a.shape[2])
    smaj = lambda a: a.reshape(-1, nb, a.shape[-1]).transpose(1, 0, 2)
    x = tmaj(x_sample.astype(F32))
    kw = dict(tm=r, tps=1, pad=0)
    st = {}
    ffn_hist = []

    def ffn(x, layer):
        y, tail = _ffn_s(x, _row(norm_ffn_g[layer]), tmaj(state_ffn[layer]), wffn[layer], nb=nb)
        ffn_hist.append(smaj(tail))
        return y

    ln = 16
    q, k, v, o, gt = _m_in(x, _row(norm_mix_g[0]), wm, **kw)

    def seq_rows(a, fill=None):
        a = smaj(a)
        if fill is None:
            a = jnp.pad(a, ((0, 0), (0, ln - nt), (0, 0)))
        else:
            a = jnp.concatenate([a, jnp.broadcast_to(fill, (nb, ln - nt, a.shape[2]))], axis=1)
        return a.reshape(nb * ln, a.shape[2])

    lane = jnp.arange(LANE)
    neutral = jnp.where(lane < M_HEADS, NEG, 0.0).astype(F32)
    gts = seq_rows(gt, fill=neutral)
    grow = gts[:, :8].reshape(nb, ln, 8).transpose(0, 2, 1)
    hh, c_new, n_new, m_new = _m_chunk(
        seq_rows(q), seq_rows(k), seq_rows(v), gts, grow, state_mlstm_C.astype(F32),
        state_mlstm_n.astype(F32).reshape(nb, M_HEADS, 1, M_DK),
        state_mlstm_m.astype(F32).reshape(nb, M_HEADS, 1, 1), ln=ln, nseq=4)
    st["mlstm_C"] = c_new
    st["mlstm_n"] = n_new.reshape(nb, M_HEADS, M_DK)
    st["mlstm_m"] = m_new.reshape(nb, M_HEADS)
    hh = tmaj(hh.reshape(nb, ln, M_HEADS * M_DV)[:, :nt, :])
    x = _m_out(hh, o, x, wm["hg"], wm["wout"], **kw)
    x = ffn(x, 0)

    x, u = _pool_s(x, _row(norm_mix_g[1]), tmaj(state_pool.astype(F32)), pool_w.astype(BF16),
                   _row(pool_scale), nb=nb, past_len=past_len)
    st["pool"] = jnp.concatenate([state_pool.astype(F32), smaj(u)], axis=1)[:, -POOL_HIST:, :]
    x = ffn(x, 1)

    pos_new = past_len + jnp.repeat(jnp.arange(nt, dtype=jnp.int32), nb)
    tabs, _ = _rope_tables(pos_new)
    _, cs = _rope_tables(jnp.arange(past_len + LANE, dtype=jnp.int32))
    qh, _, _, lat, kpe = _mla_in(x, _row(norm_mix_g[2]), wmla, tabs, tm=r, qdtype=F32)
    rows = smaj(jnp.concatenate([lat, kpe[:, MLA_NOPE:MLA_QK]], axis=1))
    st["mla"] = rows
    qabs, ab = _dec_prep(qh, wmla["kng"], wmla["wukt"])
    seq_q = lambda a: (a.reshape(MLA_HEADS, nt, nb, a.shape[-1]).transpose(2, 1, 0, 3)
                       .reshape(nb, nt * MLA_HEADS, a.shape[-1]))
    new_t = _pad_lanes(rows.transpose(0, 2, 1))
    cache_t = cache_mla.astype(F32).transpose(0, 2, 1)
    ol = _decode(page_table, cache_t, new_t, seq_q(qabs), seq_q(ab), wmla["wuk_rows"], cs.T)
    ol = ol.reshape(nb, nt, MLA_HEADS, MLA_KV_LORA).transpose(2, 1, 0, 3).reshape(MLA_HEADS, r, MLA_KV_LORA)
    x = _dec_out(ol, wmla["wuv_h"], wmla["wo"], x)
    x = ffn(x, 2)

    gl = _conv_in(x, _row(norm_mix_g[3]), wconv["w1"], wconv["b1"], **kw)
    st["conv"] = jnp.concatenate([state_conv.astype(F32), smaj(gl)], axis=1)[:, -CONV_HIST:, :]
    x = _conv_out_s(tmaj(state_conv.astype(F32)), gl, x, wconv, nb=nb)
    x = ffn(x, 3)

    st["ffn"] = jnp.stack(ffn_hist, axis=0)
    return smaj(x), st


def kernel(x_prompt, x_sample, state_mlstm_C, state_mlstm_n, state_mlstm_m, state_pool, cache_mla, page_table, state_conv, state_ffn, meta_tokens, norm_mix_g, norm_ffn_g, m_w_in, m_b_gates, m_head_norm_g, m_w_out, pool_w, pool_scale, mla_w_dq, mla_q_lora_g, mla_w_uq, mla_w_dkv, mla_kv_lora_g, mla_w_ukv, mla_q_norm_g, mla_k_norm_g, mla_w_o, conv_w_pw1, conv_b_pw1, conv_w_dw, conv_b_dw, conv_ln_g, conv_ln_b, conv_w_pw2, conv_b_pw2, ffn_w_up, ffn_conv_w, ffn_conv_b, ffn_w_down):
    depth = ffn_w_up.shape[0]
    wm = _prep_mlstm(m_w_in, m_b_gates, m_head_norm_g, m_w_out)
    wffn = [_prep_ffn(ffn_w_up[i], ffn_conv_w[i], ffn_conv_b[i], ffn_w_down[i]) for i in range(depth)]
    wmla = _prep_mla(mla_w_dq, mla_q_lora_g, mla_w_uq, mla_w_dkv, mla_kv_lora_g, mla_w_ukv,
                     mla_q_norm_g, mla_k_norm_g, mla_w_o)
    wconv = dict(w1=conv_w_pw1.astype(BF16), b1=_row(conv_b_pw1),
                 wdw=jnp.pad(conv_w_dw, ((0, 32 - CONV_WIDTH), (0, 0))).astype(F32),
                 bdw=_row(conv_b_dw), lng=_row(conv_ln_g), lnb=_row(conv_ln_b),
                 w2=conv_w_pw2.astype(BF16), b2=_row(conv_b_pw2))
    yp, sp = _prompt_trunk(x_prompt, meta_tokens, norm_mix_g, norm_ffn_g, wm, wffn, pool_w,
                           pool_scale, wmla, wconv)
    ys, ss = _sample_trunk(x_sample, state_mlstm_C, state_mlstm_n, state_mlstm_m, state_pool,
                           cache_mla, page_table, state_conv, state_ffn, norm_mix_g, norm_ffn_g,
                           wm, wffn, pool_w, pool_scale, wmla, wconv)
    names = ("mlstm_C", "mlstm_n", "mlstm_m", "pool", "mla", "conv", "ffn")
    return (yp, ys) + tuple(sp[k] for k in names) + tuple(ss[k] for k in names)
```

```python
import functools
import math

import jax
import jax.numpy as jnp
from jax import lax
from jax.experimental import pallas as pl
from jax.experimental.pallas import tpu as pltpu

F32 = jnp.float32
BF16 = jnp.bfloat16

D_MODEL = 1024
N_META = 16
EPS = 1e-6
M_HEADS = 4
M_DK = 128
M_DV = 256
POOL_WINDOWS = (2, 4, 8, 16)
POOL_GDIM = 256
POOL_HIST = 15
MLA_HEADS = 16
MLA_NOPE = 64
MLA_ROPE = 32
MLA_QK = 96
MLA_V = 64
MLA_Q_LORA = 512
MLA_KV_LORA = 256
MLA_CACHE_DIM = 288
ROPE_BASE = 10000.0
CONV_WIDTH = 31
CONV_HIST = 30
D_FF = 2816
FFN_HIST = 2

LANE = 128
FF_CHUNK = 256
N_FF_CHUNK = D_FF // FF_CHUNK
assert N_FF_CHUNK % 2 == 1 and N_FF_CHUNK >= 3
NEG = -1e30
LOG2E = 1.4426950408889634
VMEM_LIMIT = 56 * 1024 * 1024


def _dot(a, b):
    return jnp.dot(a, b, preferred_element_type=F32)


def _dot_nt(a, b):
    return lax.dot_general(a, b, (((1,), (1,)), ((), ())), preferred_element_type=F32)


def _dot_tn(a, b):
    return lax.dot_general(a, b, (((0,), (0,)), ((), ())), preferred_element_type=F32)


def _rms(xf, g):
    return xf * lax.rsqrt(jnp.mean(xf * xf, axis=-1, keepdims=True) + EPS) * g


def _sigmoid(x):
    return 1.0 / (1.0 + jnp.exp(-x))


def _valid_rows(i, tm, tps, pad):
    r = (i % tps) * tm + lax.broadcasted_iota(jnp.int32, (tm, 1), 0)
    return r >= pad


def _params(n_axes):
    return pltpu.CompilerParams(dimension_semantics=("arbitrary",) * n_axes,
                                vmem_limit_bytes=VMEM_LIMIT)


def _row_spec(tm, c):
    return pl.BlockSpec((tm, c), lambda i: (i, 0))


def _full_spec(shape):
    zeros = (0,) * len(shape)
    return pl.BlockSpec(shape, lambda *_: zeros)


def _sds(shape, dtype):
    return jax.ShapeDtypeStruct(shape, dtype)


def _m_in_kernel(x_ref, g_ref, wq, wk, wv, wo, wgh, wgl, bg,
                 q_o, k_o, v_o, o_o, gt_o, *, tps, pad):
    tm = x_ref.shape[0]
    hf = _rms(x_ref[...], g_ref[...])
    h = hf.astype(BF16)
    q_o[...] = (_dot(h, wq[...]) * (M_DK ** -0.5)).astype(BF16)
    k_o[...] = _dot(h, wk[...]).astype(BF16)
    v_o[...] = _dot(h, wv[...]).astype(BF16)
    o_o[...] = _sigmoid(_dot(h, wo[...])).astype(BF16)
    hl = (hf - h.astype(F32)).astype(BF16)
    gts = _dot(h, wgh[...]) + _dot(hl, wgh[...]) + _dot(h, wgl[...]) + bg[...]
    lane = lax.broadcasted_iota(jnp.int32, gts.shape, 1)
    lf = jnp.minimum(gts, 0.0) - jnp.log(1.0 + jnp.exp(-jnp.abs(gts)))
    out = jnp.where(lane < M_HEADS, gts, lf)
    if pad:
        valid = _valid_rows(pl.program_id(0), tm, tps, pad)
        out = jnp.where(valid, out, jnp.where(lane < M_HEADS, NEG, 0.0))
    gt_o[...] = out


def _m_in(x, g, w, *, tm, tps, pad):
    r = x.shape[0]
    kern = functools.partial(_m_in_kernel, tps=tps, pad=pad)
    return pl.pallas_call(
        kern, grid=(r // tm,),
        in_specs=[_row_spec(tm, D_MODEL), _full_spec((1, D_MODEL)),
                  _full_spec(w["wq"].shape), _full_spec(w["wk"].shape), _full_spec(w["wv"].shape),
                  _full_spec(w["wo"].shape), _full_spec(w["wgh"].shape), _full_spec(w["wgl"].shape),
                  _full_spec((1, LANE))],
        out_specs=[_row_spec(tm, 512), _row_spec(tm, 512), _row_spec(tm, 1024),
                   _row_spec(tm, 1024), _row_spec(tm, LANE)],
        out_shape=[_sds((r, 512), BF16), _sds((r, 512), BF16), _sds((r, 1024), BF16),
                   _sds((r, 1024), BF16), _sds((r, LANE), F32)],
        compiler_params=_params(1), name="mlstm_in",
    )(x, g, w["wq"], w["wk"], w["wv"], w["wo"], w["wgh"], w["wgl"], w["bg"])


def _split3(a):
    a1 = a.astype(BF16)
    r1 = a - a1.astype(F32)
    a2 = r1.astype(BF16)
    a3 = (r1 - a2.astype(F32)).astype(BF16)
    return a1, a2, a3


def _m_chunk_kernel(q_ref, k_ref, v_ref, gc_ref, gr_ref, c0_ref, n0_ref, m0_ref,
                    h_ref, c_o, n_o, m_o, c_s, n_s, m_s, *, ln):
    c = pl.program_id(1)
    nseq = q_ref.shape[0] // ln

    @pl.when(c == 0)
    def _():
        c_s[...] = c0_ref[...]
        n_s[...] = n0_ref[...]
        m_s[...] = m0_ref[...]

    row = lax.broadcasted_iota(jnp.int32, (ln, ln), 0)
    col = lax.broadcasted_iota(jnp.int32, (ln, ln), 1)
    causal = col <= row
    tril = jnp.where(causal, 1.0, 0.0).astype(BF16)
    triu = jnp.where(row <= col, 1.0, 0.0).astype(BF16)
    for sq in range(nseq):
        rows = slice(sq * ln, (sq + 1) * ln)
        gc = gc_ref[rows, :]
        gr = gr_ref[sq]
        c1, c2, c3 = _split3(gc)
        b_col = _dot(tril, c1) + _dot(tril, c2) + _dot(tril, c3)
        r1, r2, r3 = _split3(gr)
        b_row = _dot(r1, triu) + _dot(r2, triu) + _dot(r3, triu)

        for hd in range(M_HEADS):
            ig_c = gc[:, hd:hd + 1]
            b_c = b_col[:, M_HEADS + hd:M_HEADS + hd + 1]
            ig_r = gr[hd:hd + 1, :]
            b_r = b_row[M_HEADS + hd:M_HEADS + hd + 1, :]
            m0 = m_s[sq, hd]
            c0 = c_s[sq, hd]
            n0 = n_s[sq, hd]
            dmat = jnp.where(causal, b_c - b_r + ig_r, NEG)
            inter = b_c + m0
            m_t = jnp.maximum(inter, jnp.max(dmat, axis=1, keepdims=True))
            w_intra = jnp.exp(dmat - m_t)
            w_inter = jnp.exp(inter - m_t)
            qh = q_ref[rows, hd * M_DK:(hd + 1) * M_DK]
            kh = k_ref[rows, hd * M_DK:(hd + 1) * M_DK]
            vh = v_ref[rows, hd * M_DV:(hd + 1) * M_DV]
            s = _dot_nt(qh, kh) * w_intra
            num = _dot(s.astype(BF16), vh) + w_inter * _dot(qh, c0.astype(BF16))
            den = (jnp.sum(s, axis=1, keepdims=True)
                   + w_inter * jnp.sum(qh.astype(F32) * n0, axis=1, keepdims=True))
            h_ref[rows, hd * M_DV:(hd + 1) * M_DV] = num / jnp.maximum(jnp.abs(den), jnp.exp(-m_t))
            b_last = b_c[ln - 1:ln, :]
            m_new = m_t[ln - 1:ln, :]
            wk_c = jnp.exp(b_last - b_c + ig_c - m_new)
            decay = jnp.exp(b_last + m0 - m_new)
            vw = (vh.astype(F32) * wk_c).astype(BF16)
            c_s[sq, hd] = decay * c0 + _dot_tn(kh, vw)
            n_s[sq, hd] = decay * n0 + jnp.sum(kh.astype(F32) * wk_c, axis=0, keepdims=True)
            m_s[sq, hd] = m_new

    @pl.when(c == pl.num_programs(1) - 1)
    def _():
        c_o[...] = c_s[...]
        n_o[...] = n_s[...]
        m_o[...] = m_s[...]


def _m_chunk(q, k, v, gcol, grow, c0, n0, m0, *, ln, nseq=1):
    n = c0.shape[0]
    t = q.shape[0] // n
    nc = t // ln
    assert n % nseq == 0 and (nseq == 1 or nc == 1)
    rows = lambda w: pl.BlockSpec((nseq * ln, w), lambda i, c: (i * nc + c, 0))
    st_c = pl.BlockSpec((nseq, M_HEADS, M_DK, M_DV), lambda i, c: (i, 0, 0, 0))
    st_n = pl.BlockSpec((nseq, M_HEADS, 1, M_DK), lambda i, c: (i, 0, 0, 0))
    st_m = pl.BlockSpec((nseq, M_HEADS, 1, 1), lambda i, c: (i, 0, 0, 0))
    return pl.pallas_call(
        functools.partial(_m_chunk_kernel, ln=ln), grid=(n // nseq, nc),
        in_specs=[rows(512), rows(512), rows(1024), rows(LANE),
                  pl.BlockSpec((nseq, 8, ln), lambda i, c: (i, 0, c)), st_c, st_n, st_m],
        out_specs=[rows(1024), st_c, st_n, st_m],
        out_shape=[_sds((n * t, 1024), F32), _sds((n, M_HEADS, M_DK, M_DV), F32),
                   _sds((n, M_HEADS, 1, M_DK), F32), _sds((n, M_HEADS, 1, 1), F32)],
        scratch_shapes=[pltpu.VMEM((nseq, M_HEADS, M_DK, M_DV), F32),
                        pltpu.VMEM((nseq, M_HEADS, 1, M_DK), F32),
                        pltpu.VMEM((nseq, M_HEADS, 1, 1), F32)],
        compiler_params=_params(2), name="mlstm_chunk",
    )(q, k, v, gcol, grow, c0, n0, m0)


def _m_out_kernel(hh_ref, o_ref, x_ref, g_ref, w_ref, y_ref, *, tps, pad):
    tm = x_ref.shape[0]
    parts = []
    for hd in range(M_HEADS):
        sl = slice(hd * M_DV, (hd + 1) * M_DV)
        hn = _rms(hh_ref[:, sl], g_ref[:, sl])
        parts.append((hn * o_ref[:, sl].astype(F32)).astype(BF16))
    y = x_ref[...] + _dot(jnp.concatenate(parts, axis=1), w_ref[...])
    if pad:
        y = jnp.where(_valid_rows(pl.program_id(0), tm, tps, pad), y, 0.0)
    y_ref[...] = y


def _m_out(hh, o, x, g, w, *, tm, tps, pad):
    r = x.shape[0]
    kern = functools.partial(_m_out_kernel, tps=tps, pad=pad)
    return pl.pallas_call(
        kern, grid=(r // tm,),
        in_specs=[_row_spec(tm, 1024), _row_spec(tm, 1024), _row_spec(tm, D_MODEL),
                  _full_spec((1, 1024)), _full_spec((1024, D_MODEL))],
        out_specs=_row_spec(tm, D_MODEL), out_shape=_sds((r, D_MODEL), F32),
        compiler_params=_params(1), name="mlstm_out",
    )(hh, o, x, g, w)


def _ffn_kernel(x_ref, g_ref, wg_ref, wu_ref, cw_ref, cb_ref, wd_ref,
                y_ref, tail_ref, carry_s, gu_s, act_s, *, tps, pad):
    i = pl.program_id(0)
    tm = x_ref.shape[0]
    x = x_ref[...]
    h = _rms(x, g_ref[...]).astype(BF16)

    @pl.when(i % tps == 0)
    def _():
        carry_s[...] = jnp.zeros_like(carry_s)

    rowid = lax.broadcasted_iota(jnp.int32, (tm, 1), 0)

    def up(c, slot):
        gu_s[slot, 0] = _dot(h, wg_ref[c])
        gu_s[slot, 1] = _dot(h, wu_ref[c])

    def gate(c, slot):
        g = gu_s[slot, 0]
        u = gu_s[slot, 1]
        prev = carry_s[c]
        g1 = jnp.where(rowid == 0, prev[7:8], pltpu.roll(g, 1, 0))
        g2 = jnp.where(rowid == 0, prev[6:7], jnp.where(rowid == 1, prev[7:8], pltpu.roll(g, 2, 0)))
        cw = cw_ref[c]
        gc = cw[0:1] * g2 + cw[1:2] * g1 + cw[2:3] * g + cb_ref[c]
        act_s[slot] = (gc * _sigmoid(gc) * u).astype(BF16)
        last = g[tm - 8:tm]
        carry_s[c] = last
        tail_ref[0, c] = last

    def down(c, slot):
        return _dot(act_s[slot], wd_ref[c])

    y_ref[...] = x
    up(0, 0)
    gate(0, 0)

    def pair(j, carry):
        c = 2 * j + 1
        d0 = down(c - 1, 0)
        up(c, 1)
        gate(c, 1)
        d1 = down(c, 1)
        up(c + 1, 0)
        gate(c + 1, 0)
        y_ref[...] += d0 + d1
        return carry

    lax.fori_loop(0, (N_FF_CHUNK - 1) // 2, pair, 0)
    y = y_ref[...] + down(N_FF_CHUNK - 1, 0)
    if pad:
        y = jnp.where(_valid_rows(i, tm, tps, pad), y, 0.0)
    y_ref[...] = y


def _ffn(x, g, w, *, tm, tps, pad):
    r = x.shape[0]
    nt = r // tm
    kern = functools.partial(_ffn_kernel, tps=tps, pad=pad)
    return pl.pallas_call(
        kern, grid=(nt,),
        in_specs=[_row_spec(tm, D_MODEL), _full_spec((1, D_MODEL)),
                  _full_spec(w["wg"].shape), _full_spec(w["wu"].shape), _full_spec(w["cw"].shape),
                  _full_spec(w["cb"].shape), _full_spec(w["wd"].shape)],
        out_specs=[_row_spec(tm, D_MODEL),
                   pl.BlockSpec((1, N_FF_CHUNK, 8, FF_CHUNK), lambda i: (i, 0, 0, 0))],
        out_shape=[_sds((r, D_MODEL), F32), _sds((nt, N_FF_CHUNK, 8, FF_CHUNK), F32)],
        scratch_shapes=[pltpu.VMEM((N_FF_CHUNK, 8, FF_CHUNK), F32),
                        pltpu.VMEM((2, 2, tm, FF_CHUNK), F32), pltpu.VMEM((2, tm, FF_CHUNK), BF16)],
        compiler_params=_params(1), name="ffn",
    )(x, g, w["wg"], w["wu"], w["cw"], w["cb"], w["wd"])


def _ffn_s_kernel(x_ref, g_ref, hist_ref, wg_ref, wu_ref, cw_ref, cb_ref, wd_ref,
                  y_ref, tail_ref, acc_s, *, nb):
    c = pl.program_id(0)
    x = x_ref[...]
    h = _rms(x, g_ref[...]).astype(BF16)

    @pl.when(c == 0)
    def _():
        acc_s[...] = jnp.zeros_like(acc_s)

    g = _dot(h, wg_ref[0])
    u = _dot(h, wu_ref[0])
    hist = hist_ref[...]
    t = g.shape[0]
    ext0 = jnp.concatenate([hist, g[:t - 2 * nb]], axis=0)
    ext1 = jnp.concatenate([hist[nb:], g[:t - nb]], axis=0)
    cw = cw_ref[0]
    gc = cw[0:1] * ext0 + cw[1:2] * ext1 + cw[2:3] * g + cb_ref[0]
    act = (gc * _sigmoid(gc) * u).astype(BF16)
    acc_s[...] += _dot(act, wd_ref[0])
    tail_ref[...] = g[t - 2 * nb:]

    @pl.when(c == pl.num_programs(0) - 1)
    def _():
        y_ref[...] = x + acc_s[...]


def _ffn_s(x, g, hist, w, *, nb):
    r = x.shape[0]
    kern = functools.partial(_ffn_s_kernel, nb=nb)
    chunk3 = lambda a, b: pl.BlockSpec((1, a, b), lambda c: (c, 0, 0))
    return pl.pallas_call(
        kern, grid=(N_FF_CHUNK,),
        in_specs=[_full_spec((r, D_MODEL)), _full_spec((1, D_MODEL)),
                  pl.BlockSpec((2 * nb, FF_CHUNK), lambda c: (0, c)),
                  chunk3(D_MODEL, FF_CHUNK), chunk3(D_MODEL, FF_CHUNK), chunk3(8, FF_CHUNK),
                  chunk3(1, FF_CHUNK), chunk3(FF_CHUNK, D_MODEL)],
        out_specs=[_full_spec((r, D_MODEL)), pl.BlockSpec((2 * nb, FF_CHUNK), lambda c: (0, c))],
        out_shape=[_sds((r, D_MODEL), F32), _sds((2 * nb, D_FF), F32)],
        scratch_shapes=[pltpu.VMEM((r, D_MODEL), F32)],
        compiler_params=_params(1), name="ffn_sample",
    )(x, g, hist, w["wg"], w["wu"], w["cw"], w["cb"], w["wd"])


def _pool_kernel(x_ref, g_ref, pw_ref, ps_ref, y_ref, tail_ref, ext_s, *, tps, pad):
    i = pl.program_id(0)
    tm = x_ref.shape[0]
    x = x_ref[...]
    u = _rms(x, g_ref[...])
    hist = POOL_HIST + 1

    @pl.when(i % tps == 0)
    def _():
        ext_s[0:hist] = jnp.zeros((hist, D_MODEL), F32)

    @pl.when(i % tps != 0)
    def _():
        ext_s[0:hist] = ext_s[tm:tm + hist]

    ext_s[hist:hist + tm] = u
    tail_ref[0] = u[tm - hist:]
    pos = (i % tps) * tm + lax.broadcasted_iota(jnp.int32, (tm, 1), 0) - pad
    valid = pos >= 0
    for gi, w in enumerate(POOL_WINDOWS):
        sl = slice(gi * POOL_GDIM, (gi + 1) * POOL_GDIM)
        ug = u[:, sl]
        win = ug
        for j in range(1, w):
            win = win + ext_s[hist - j:hist - j + tm, sl]
        cnt = jnp.maximum(jnp.minimum(pos + 1, w), 1).astype(F32)
        d = (win / cnt - ug).astype(BF16)
        y = x[:, sl] + _dot(d, pw_ref[gi]) * ps_ref[:, sl]
        y_ref[:, sl] = jnp.where(valid, y, 0.0)


def _pool(x, g, pw, ps, *, tm, tps, pad):
    r = x.shape[0]
    nt = r // tm
    kern = functools.partial(_pool_kernel, tps=tps, pad=pad)
    return pl.pallas_call(
        kern, grid=(nt,),
        in_specs=[_row_spec(tm, D_MODEL), _full_spec((1, D_MODEL)),
                  _full_spec((4, POOL_GDIM, POOL_GDIM)), _full_spec((1, D_MODEL))],
        out_specs=[_row_spec(tm, D_MODEL), pl.BlockSpec((1, 16, D_MODEL), lambda i: (i, 0, 0))],
        out_shape=[_sds((r, D_MODEL), F32), _sds((nt, 16, D_MODEL), F32)],
        scratch_shapes=[pltpu.VMEM((tm + 16, D_MODEL), F32)],
        compiler_params=_params(1), name="pool",
    )(x, g, pw, ps)


def _pool_s_kernel(x_ref, g_ref, hist_ref, pw_ref, ps_ref, y_ref, u_ref, *, nb, past_len):
    x = x_ref[...]
    u = _rms(x, g_ref[...])
    u_ref[...] = u
    nt = x.shape[0] // nb
    for gi, w in enumerate(POOL_WINDOWS):
        sl = slice(gi * POOL_GDIM, (gi + 1) * POOL_GDIM)
        outs = []
        for t in range(nt):
            win = u[t * nb:(t + 1) * nb, sl]
            for j in range(1, w):
                src = t - j
                if src >= 0:
                    win = win + u[src * nb:(src + 1) * nb, sl]
                else:
                    hrow = POOL_HIST + src
                    win = win + hist_ref[hrow * nb:(hrow + 1) * nb, sl]
            cnt = float(min(past_len + t + 1, w))
            outs.append(win / cnt - u[t * nb:(t + 1) * nb, sl])
        d = jnp.concatenate(outs, axis=0).astype(BF16)
        y_ref[:, sl] = x[:, sl] + _dot(d, pw_ref[gi]) * ps_ref[:, sl]


def _pool_s(x, g, hist, pw, ps, *, nb, past_len):
    r = x.shape[0]
    kern = functools.partial(_pool_s_kernel, nb=nb, past_len=past_len)
    return pl.pallas_call(
        kern, grid=(1,),
        in_specs=[_full_spec((r, D_MODEL)), _full_spec((1, D_MODEL)), _full_spec(hist.shape),
                  _full_spec((4, POOL_GDIM, POOL_GDIM)), _full_spec((1, D_MODEL))],
        out_specs=[_full_spec((r, D_MODEL)), _full_spec((r, D_MODEL))],
        out_shape=[_sds((r, D_MODEL), F32), _sds((r, D_MODEL), F32)],
        compiler_params=_params(1), name="pool_sample",
    )(x, g, hist, pw, ps)


def _rope_norm(z, g, tc, ts, live):
    zn = z * lax.rsqrt(jnp.sum(z * z * live, axis=-1, keepdims=True) * (1.0 / MLA_QK) + EPS) * g
    return zn * tc + pltpu.roll(zn, LANE - 16, 1) * ts


def _mla_in_kernel(x_ref, g_ref, wdq, qlg, wuq, qng, wlat, wkpe, kvg, wuk, wuv, kng,
                   tc_ref, ts_ref, qh_o, kh_o, vp_o, lat_o, kpe_o):
    h = _rms(x_ref[...], g_ref[...]).astype(BF16)
    cqn = _rms(_dot(h, wdq[...]), qlg[...]).astype(BF16)
    latn = _rms(_dot(h, wlat[...]), kvg[...])
    lat_o[...] = latn
    latb = latn.astype(BF16)
    kpe = _dot(h, wkpe[...])
    kpe_o[...] = kpe
    tc, ts = tc_ref[...], ts_ref[...]
    live = jnp.where(lax.broadcasted_iota(jnp.int32, (1, LANE), 1) < MLA_QK, 1.0, 0.0)
    for j in range(MLA_HEADS // 2):
        q2 = _dot(cqn, wuq[j])
        k2 = _dot(latb, wuk[j])
        for half in range(2):
            sl = slice(half * LANE, (half + 1) * LANE)
            qh_o[2 * j + half] = _rope_norm(q2[:, sl], qng[...], tc, ts, live).astype(qh_o.dtype)
            kh_o[2 * j + half] = _rope_norm(k2[:, sl] + kpe, kng[...], tc, ts, live).astype(BF16)
    ones = jnp.ones((latb.shape[0], LANE), BF16)
    for j in range(MLA_HEADS // 4):
        v4 = _dot(latb, wuv[j]).astype(BF16)
        vp_o[2 * j] = jnp.concatenate([v4[:, :LANE], ones], axis=1)
        vp_o[2 * j + 1] = jnp.concatenate([v4[:, LANE:], ones], axis=1)


def _mla_in(x, g, w, tabs, *, tm, qdtype):
    r = x.shape[0]
    heads = lambda n, w=LANE: pl.BlockSpec((n, tm, w), lambda i: (0, i, 0))
    tps = tabs[0].shape[0] // tm
    tab = pl.BlockSpec((tm, LANE), lambda i: (i % tps, 0))
    return pl.pallas_call(
        _mla_in_kernel, grid=(r // tm,),
        in_specs=[_row_spec(tm, D_MODEL), _full_spec((1, D_MODEL)),
                  _full_spec(w["wdq"].shape), _full_spec((1, MLA_Q_LORA)), _full_spec(w["wuq"].shape),
                  _full_spec((1, LANE)), _full_spec(w["wlat"].shape), _full_spec(w["wkpe"].shape),
                  _full_spec((1, MLA_KV_LORA)), _full_spec(w["wuk"].shape), _full_spec(w["wuv"].shape),
                  _full_spec((1, LANE)), tab, tab],
        out_specs=[heads(MLA_HEADS), heads(MLA_HEADS), heads(MLA_HEADS // 2, 2 * LANE),
                   _row_spec(tm, MLA_KV_LORA), _row_spec(tm, LANE)],
        out_shape=[_sds((MLA_HEADS, r, LANE), qdtype), _sds((MLA_HEADS, r, LANE), BF16),
                   _sds((MLA_HEADS // 2, r, 2 * LANE), BF16), _sds((r, MLA_KV_LORA), F32),
                   _sds((r, LANE), F32)],
        compiler_params=_params(1), name="mla_in",
    )(x, g, w["wdq"], w["qlg"], w["wuq"], w["qng"], w["wlat"], w["wkpe"], w["kvg"],
      w["wuk"], w["wuv"], w["kng"], *tabs)


def _attn_kernel(q_ref, k_ref, v_ref, o_ref, acc_s, m_s, p_s, a_s, *, pad, nsub):
    qi = pl.program_id(2)
    tq = q_ref.shape[1]
    tk = tq
    tqs = tq // nsub
    for hh in range(2):
        m_s[hh] = jnp.full((tq, LANE), NEG, F32)
        acc_s[hh] = jnp.zeros((tq, 2 * LANE), F32)

    def score(ki, masked):
        ks = pl.multiple_of(ki * tk, tk)
        for qs in range(nsub):
            rows = slice(qs * tqs, (qs + 1) * tqs)
            for hh in range(2):
                s = _dot_nt(q_ref[hh, rows, :], k_ref[hh, pl.ds(ks, tk), :])
                if masked:
                    qpos = qi * tq + qs * tqs + lax.broadcasted_iota(jnp.int32, (tqs, tk), 0)
                    kpos = ki * tk + lax.broadcasted_iota(jnp.int32, (tqs, tk), 1)
                    s = jnp.where(kpos <= qpos, jnp.where(kpos >= pad, s, NEG), NEG)
                m_old = m_s[hh, rows, :]
                m_new = jnp.maximum(m_old, jnp.max(s, axis=1, keepdims=True))
                p_s[hh, rows, :] = jnp.exp2(s - jnp.tile(m_new, (1, tk // LANE))).astype(BF16)
                a_s[hh, rows, :] = jnp.exp2(m_old - m_new)
                m_s[hh, rows, :] = m_new

    def value(ki):
        ks = pl.multiple_of(ki * tk, tk)
        vblk = v_ref[0, pl.ds(ks, tk), :]
        for qs in range(nsub):
            rows = slice(qs * tqs, (qs + 1) * tqs)
            for hh in range(2):
                acc_s[hh, rows, :] = (jnp.tile(a_s[hh, rows, :], (1, 2)) * acc_s[hh, rows, :]
                                      + _dot(p_s[hh, rows, :], vblk))

    score(0, True)
    value(0)

    def body(ki, carry):
        score(ki, False)
        value(ki)
        return carry

    lax.fori_loop(1, qi, body, 0)

    @pl.when(qi > 0)
    def _():
        score(qi, True)
        value(qi)

    lane = lax.broadcasted_iota(jnp.int32, (tq, LANE), 1)
    a0 = acc_s[0]
    a1 = acc_s[1]
    o_ref[...] = jnp.where(lane < MLA_V, a0[:, :LANE] / a0[:, LANE:],
                           a1[:, :LANE] / a1[:, LANE:]).astype(BF16)


def _attn(qh, kh, vp, *, n, tp, tq, pad):
    nq = tp // tq
    kern = functools.partial(_attn_kernel, pad=pad, nsub=4)
    return pl.pallas_call(
        kern, grid=(n, MLA_HEADS // 2, nq),
        in_specs=[pl.BlockSpec((2, tq, LANE), lambda b, hp, qi: (hp, b * nq + qi, 0)),
                  pl.BlockSpec((2, tp, LANE), lambda b, hp, qi: (hp, b, 0)),
                  pl.BlockSpec((1, tp, 2 * LANE), lambda b, hp, qi: (hp, b, 0))],
        out_specs=pl.BlockSpec((tq, LANE), lambda b, hp, qi: (b * nq + qi, hp)),
        out_shape=_sds((n * tp, MLA_HEADS * MLA_V), BF16),
        scratch_shapes=[pltpu.VMEM((2, tq, 2 * LANE), F32), pltpu.VMEM((2, tq, LANE), F32),
                        pltpu.VMEM((2, tq, tq), BF16), pltpu.VMEM((2, tq, LANE), F32)],
        compiler_params=_params(3), name="attn",
    )(qh, kh, vp)


def _proj_out_kernel(o_ref, x_ref, w_ref, y_ref, *, tps, pad):
    tm = x_ref.shape[0]
    y = x_ref[...] + _dot(o_ref[...], w_ref[...])
    if pad:
        y = jnp.where(_valid_rows(pl.program_id(0), tm, tps, pad), y, 0.0)
    y_ref[...] = y


def _proj_out(o, x, w, *, tm, tps, pad):
    r = x.shape[0]
    kern = functools.partial(_proj_out_kernel, tps=tps, pad=pad)
    return pl.pallas_call(
        kern, grid=(r // tm,),
        in_specs=[_row_spec(tm, 1024), _row_spec(tm, D_MODEL), _full_spec((1024, D_MODEL))],
        out_specs=_row_spec(tm, D_MODEL), out_shape=_sds((r, D_MODEL), F32),
        compiler_params=_params(1), name="mla_out",
    )(o, x, w)


def _dec_prep_kernel(qh_ref, kng, wukt, qabs_o, ab_o):
    g = kng[...]
    g1 = g[:, MLA_NOPE:MLA_NOPE + 16]
    g2 = g[:, MLA_NOPE + 16:MLA_QK]
    for hd in range(MLA_HEADS):
        qh = qh_ref[hd]
        qabs_o[hd] = _dot((qh * g).astype(BF16), wukt[hd]).astype(BF16)
        q1 = qh[:, MLA_NOPE:MLA_NOPE + 16]
        q2 = qh[:, MLA_NOPE + 16:MLA_QK]
        ab_o[hd] = jnp.concatenate([g1 * q1, g2 * q2, g1 * q2, -(g2 * q1)], axis=1).astype(BF16)


def _dec_prep(qh, kng, wukt):
    r = qh.shape[1]
    return pl.pallas_call(
        _dec_prep_kernel, grid=(1,),
        in_specs=[_full_spec(qh.shape), _full_spec((1, LANE)), _full_spec(wukt.shape)],
        out_specs=[_full_spec((MLA_HEADS, r, MLA_KV_LORA)), _full_spec((MLA_HEADS, r, 64))],
        out_shape=[_sds((MLA_HEADS, r, MLA_KV_LORA), BF16), _sds((MLA_HEADS, r, 64), BF16)],
        compiler_params=_params(1), name="decode_prep",
    )(qh, kng, wukt)


def _decode_kernel(pt_ref, new_ref, qm_ref, ab_ref, wuk_ref, cs_ref, *rest,
                   n_pages, page, pages_per_chunk):
    del pt_ref
    page_refs = rest[:n_pages]
    (o_ref,) = rest[n_pages:]
    nq = qm_ref.shape[1]
    lhs = jnp.concatenate([qm_ref[0], wuk_ref[...]], axis=0)
    ab = ab_ref[0]
    ck = pages_per_chunk * page
    n_chunk = n_pages // pages_per_chunk
    past = n_pages * page
    nn = new_ref.shape[2]

    def scores(lat, kpe, cs):
        n = lat.shape[1]
        latb = lat.astype(BF16)
        kcs = (jnp.concatenate([kpe, kpe], axis=0) * cs).astype(BF16)
        big = _dot(lhs, latb)
        raw = big[:nq] + _dot(ab, kcs)
        kn = big[nq:]
        ss = jnp.sum((kn * kn).reshape(MLA_NOPE, MLA_HEADS, n), axis=0)
        ss = ss + jnp.sum(kpe * kpe, axis=0, keepdims=True)
        r = lax.rsqrt(ss * (1.0 / MLA_QK) + EPS)
        return latb, jnp.concatenate([r] * (nq // MLA_HEADS), axis=0) * raw

    def fold(state, latb, s):
        m, l, acc = state
        m_new = jnp.maximum(m, jnp.max(s, axis=1, keepdims=True))
        p = jnp.exp2(s - m_new)
        alpha = jnp.exp2(m - m_new)
        l = alpha * l + jnp.sum(p, axis=1, keepdims=True)
        acc = alpha * acc + _dot_nt(p.astype(BF16), latb)
        return m_new, l, acc

    state = (jnp.full((nq, 1), NEG, F32), jnp.zeros((nq, 1), F32),
             jnp.zeros((nq, MLA_KV_LORA), F32))
    for c in range(n_chunk):
        blks = [page_refs[c * pages_per_chunk + p][0] for p in range(pages_per_chunk)]
        lat = jnp.concatenate([b[:MLA_KV_LORA, :] for b in blks], axis=1)
        kpe = jnp.concatenate([b[MLA_KV_LORA:, :] for b in blks], axis=1)
        latb, s = scores(lat, kpe, cs_ref[:, c * ck:(c + 1) * ck])
        state = fold(state, latb, s)
    blk = new_ref[0]
    latb, s = scores(blk[:MLA_KV_LORA, :], blk[MLA_KV_LORA:, :], cs_ref[:, past:past + nn])
    key = lax.broadcasted_iota(jnp.int32, (nq, nn), 1)
    tok = lax.broadcasted_iota(jnp.int32, (nq, nn), 0) // MLA_HEADS
    _, l, acc = fold(state, latb, jnp.where(key <= tok, s, NEG))
    o_ref[0] = acc / l


def _decode(page_table, cache_t, new_t, qm, ab, wuk, cs, *, pages_per_chunk=16):
    nb, n_pages = page_table.shape
    assert n_pages % pages_per_chunk == 0
    page = cache_t.shape[2]
    nn = new_t.shape[2]
    nq = qm.shape[1]
    kern = functools.partial(_decode_kernel, n_pages=n_pages, page=page,
                             pages_per_chunk=pages_per_chunk)
    per_seq = lambda shape: pl.BlockSpec((1,) + shape, lambda b, pt: (b, 0, 0))
    page_specs = [pl.BlockSpec((1, MLA_CACHE_DIM, page),
                               lambda b, pt, j=j: (pt[b * n_pages + j], 0, 0))
                  for j in range(n_pages)]
    grid_spec = pltpu.PrefetchScalarGridSpec(
        num_scalar_prefetch=1, grid=(nb,),
        in_specs=[per_seq((MLA_CACHE_DIM, nn)), per_seq((nq, MLA_KV_LORA)), per_seq((nq, 64)),
                  pl.BlockSpec(wuk.shape, lambda b, pt: (0, 0)),
                  pl.BlockSpec(cs.shape, lambda b, pt: (0, 0))] + page_specs,
        out_specs=per_seq((nq, MLA_KV_LORA)))
    return pl.pallas_call(
        kern, grid_spec=grid_spec, out_shape=_sds((nb, nq, MLA_KV_LORA), F32),
        compiler_params=_params(1), name="decode",
    )(page_table.reshape(-1), new_t, qm, ab, wuk, cs, *([cache_t] * n_pages))


def _dec_out_kernel(ol_ref, wuv, wo, x_ref, y_ref):
    parts = [_dot(ol_ref[hd].astype(BF16), wuv[hd]) for hd in range(MLA_HEADS)]
    o = jnp.concatenate(parts, axis=1).astype(BF16)
    y_ref[...] = x_ref[...] + _dot(o, wo[...])


def _dec_out(ol, wuv, wo, x):
    r = x.shape[0]
    return pl.pallas_call(
        _dec_out_kernel, grid=(1,),
        in_specs=[_full_spec(ol.shape), _full_spec(wuv.shape),
                  _full_spec(wo.shape), _full_spec((r, D_MODEL))],
        out_specs=_full_spec((r, D_MODEL)), out_shape=_sds((r, D_MODEL), F32),
        compiler_params=_params(1), name="decode_out",
    )(ol, wuv, wo, x)


def _conv_in_kernel(x_ref, g_ref, w_ref, b_ref, gl_ref, *, tps, pad):
    tm = x_ref.shape[0]
    h = _rms(x_ref[...], g_ref[...]).astype(BF16)
    a = _dot(h, w_ref[...]) + b_ref[...]
    gl = a[:, :D_MODEL] * _sigmoid(a[:, D_MODEL:])
    if pad:
        gl = jnp.where(_valid_rows(pl.program_id(0), tm, tps, pad), gl, 0.0)
    gl_ref[...] = gl


def _conv_in(x, g, w, b, *, tm, tps, pad):
    r = x.shape[0]
    kern = functools.partial(_conv_in_kernel, tps=tps, pad=pad)
    return pl.pallas_call(
        kern, grid=(r // tm,),
        in_specs=[_row_spec(tm, D_MODEL), _full_spec((1, D_MODEL)),
                  _full_spec((D_MODEL, 2 * D_MODEL)), _full_spec((1, 2 * D_MODEL))],
        out_specs=_row_spec(tm, D_MODEL), out_shape=_sds((r, D_MODEL), F32),
        compiler_params=_params(1), name="conv_in",
    )(x, g, w, b)


def _ln_silu_proj(c, x, lng, lnb, w2, b2):
    mu = jnp.mean(c, axis=-1, keepdims=True)
    cc = c - mu
    var = jnp.mean(cc * cc, axis=-1, keepdims=True)
    y = cc * lax.rsqrt(var + EPS) * lng + lnb
    a = (y * _sigmoid(y)).astype(BF16)
    return x + _dot(a, w2) + b2


CONV_COLS = 256
CONV_ROWS = 64


def _conv_out_kernel(gl_ref, x_ref, wdw, bdw, lng, lnb, w2, b2, y_ref, ext_s, sh_s, c_s, *, tps, pad):
    i = pl.program_id(0)
    tm = x_ref.shape[0]
    hist = 32
    n_ext = tm + hist

    @pl.when(i % tps == 0)
    def _():
        ext_s[0:hist] = jnp.zeros((hist, D_MODEL), F32)

    @pl.when(i % tps != 0)
    def _():
        ext_s[0:hist] = ext_s[tm:tm + hist]

    ext_s[hist:n_ext] = gl_ref[...]
    ext_s[n_ext:n_ext + 8] = jnp.zeros((8, D_MODEL), F32)
    off = hist - CONV_HIST
    for cb in range(D_MODEL // CONV_COLS):
        cols = slice(cb * CONV_COLS, (cb + 1) * CONV_COLS)
        for s in range(8):
            sh_s[s] = ext_s[s:s + n_ext, cols]
        wblk = wdw[:, cols]
        bias = bdw[:, cols]

        def rows(rb, carry):
            r0 = pl.multiple_of(rb * CONV_ROWS, CONV_ROWS)
            acc = jnp.zeros((CONV_ROWS, CONV_COLS), F32)
            for j in range(CONV_WIDTH):
                a, s = divmod(off + j, 8)
                acc = acc + wblk[j:j + 1] * sh_s[s, pl.ds(r0 + 8 * a, CONV_ROWS), :]
            c_s[pl.ds(r0, CONV_ROWS), cols] = acc + bias
            return carry

        lax.fori_loop(0, tm // CONV_ROWS, rows, 0)
    y = _ln_silu_proj(c_s[...], x_ref[...], lng[...], lnb[...], w2[...], b2[...])
    if pad:
        y = jnp.where(_valid_rows(i, tm, tps, pad), y, 0.0)
    y_ref[...] = y


def _conv_out(gl, x, w, *, tm, tps, pad):
    r = x.shape[0]
    kern = functools.partial(_conv_out_kernel, tps=tps, pad=pad)
    vec = _full_spec((1, D_MODEL))
    return pl.pallas_call(
        kern, grid=(r // tm,),
        in_specs=[_row_spec(tm, D_MODEL), _row_spec(tm, D_MODEL), _full_spec((32, D_MODEL)),
                  vec, vec, vec, _full_spec((D_MODEL, D_MODEL)), vec],
        out_specs=_row_spec(tm, D_MODEL), out_shape=_sds((r, D_MODEL), F32),
        scratch_shapes=[pltpu.VMEM((tm + 40, D_MODEL), F32), pltpu.VMEM((8, tm + 32, CONV_COLS), F32),
                        pltpu.VMEM((tm, D_MODEL), F32)],
        compiler_params=_params(1), name="conv_out",
    )(gl, x, w["wdw"], w["bdw"], w["lng"], w["lnb"], w["w2"], w["b2"])


def _conv_out_s_kernel(hist_ref, gl_ref, x_ref, wdw, bdw, lng, lnb, w2, b2, y_ref, *, nb):
    nt = x_ref.shape[0] // nb
    outs = []
    for t in range(nt):
        acc = None
        for j in range(CONV_WIDTH):
            src = t + j
            if src < CONV_HIST:
                slab = hist_ref[src * nb:(src + 1) * nb, :]
            else:
                slab = gl_ref[(src - CONV_HIST) * nb:(src - CONV_HIST + 1) * nb, :]
            term = wdw[j:j + 1, :] * slab
            acc = term if acc is None else acc + term
        outs.append(acc)
    c = jnp.concatenate(outs, axis=0) + bdw[...]
    y_ref[...] = _ln_silu_proj(c, x_ref[...], lng[...], lnb[...], w2[...], b2[...])


def _conv_out_s(hist, gl, x, w, *, nb):
    r = x.shape[0]
    kern = functools.partial(_conv_out_s_kernel, nb=nb)
    vec = _full_spec((1, D_MODEL))
    return pl.pallas_call(
        kern, grid=(1,),
        in_specs=[_full_spec(hist.shape), _full_spec((r, D_MODEL)), _full_spec((r, D_MODEL)),
                  _full_spec((32, D_MODEL)), vec, vec, vec, _full_spec((D_MODEL, D_MODEL)), vec],
        out_specs=_full_spec((r, D_MODEL)), out_shape=_sds((r, D_MODEL), F32),
        compiler_params=_params(1), name="conv_out_sample",
    )(hist, gl, x, w["wdw"], w["bdw"], w["lng"], w["lnb"], w["w2"], w["b2"])


def _pad_lanes(a, width=LANE):
    return jnp.pad(a, [(0, 0)] * (a.ndim - 1) + [(0, width - a.shape[-1])])


def _row(a):
    return a.reshape(1, -1).astype(F32)


def _prep_mlstm(m_w_in, m_b_gates, m_head_norm_g, m_w_out):
    i0 = M_HEADS * M_DK
    i1 = 2 * i0
    i2 = i1 + M_HEADS * M_DV
    i3 = i2 + M_HEADS * M_DV
    wgate = _pad_lanes(m_w_in[:, i3:])
    wgh = wgate.astype(BF16)
    wgl = (wgate - wgh.astype(F32)).astype(BF16)
    return dict(wq=m_w_in[:, :i0].astype(BF16), wk=m_w_in[:, i0:i1].astype(BF16),
                wv=m_w_in[:, i1:i2].astype(BF16), wo=m_w_in[:, i2:i3].astype(BF16),
                wgh=wgh, wgl=wgl, bg=_pad_lanes(_row(m_b_gates)),
                hg=_row(m_head_norm_g), wout=m_w_out.astype(BF16))


def _prep_ffn(w_up, conv_w, conv_b, w_down):
    chunks = lambda a: a.reshape(D_MODEL, N_FF_CHUNK, FF_CHUNK).transpose(1, 0, 2).astype(BF16)
    cw = jnp.pad(conv_w, ((0, 8 - conv_w.shape[0]), (0, 0)))
    return dict(wg=chunks(w_up[:, :D_FF]), wu=chunks(w_up[:, D_FF:]),
                cw=cw.reshape(8, N_FF_CHUNK, FF_CHUNK).transpose(1, 0, 2).astype(F32),
                cb=conv_b.reshape(N_FF_CHUNK, 1, FF_CHUNK).astype(F32),
                wd=w_down.reshape(N_FF_CHUNK, FF_CHUNK, D_MODEL).astype(BF16))


def _prep_mla(mla_w_dq, mla_q_lora_g, mla_w_uq, mla_w_dkv, mla_kv_lora_g, mla_w_ukv,
              mla_q_norm_g, mla_k_norm_g, mla_w_o):
    pair = lambda a: (a.reshape(a.shape[0], MLA_HEADS // 2, 2 * a.shape[2]).transpose(1, 0, 2))
    half1 = slice(MLA_NOPE, MLA_NOPE + MLA_ROPE // 2)
    lanes = lambda a: _pad_lanes(jnp.concatenate([a, a[..., half1]], axis=-1))
    wuq = pair(lanes(mla_w_uq.reshape(MLA_Q_LORA, MLA_HEADS, MLA_QK)))
    ukv = mla_w_ukv.reshape(MLA_KV_LORA, MLA_HEADS, MLA_NOPE + MLA_V)
    uk = ukv[:, :, :MLA_NOPE]
    uv = ukv[:, :, MLA_NOPE:]
    wkpe = lanes(jnp.pad(mla_w_dkv[:, MLA_KV_LORA:], ((0, 0), (MLA_NOPE, 0))))
    qscale = (MLA_QK ** -0.5) * LOG2E
    return dict(
        wdq=mla_w_dq.astype(BF16), qlg=_row(mla_q_lora_g), wuq=wuq.astype(BF16),
        qng=lanes(_row(mla_q_norm_g)) * qscale,
        wlat=mla_w_dkv[:, :MLA_KV_LORA].astype(BF16), wkpe=wkpe.astype(BF16),
        kvg=_row(mla_kv_lora_g), wuk=pair(_pad_lanes(uk)).astype(BF16),
        wuv=uv.reshape(MLA_KV_LORA, MLA_HEADS // 4, 4 * MLA_V).transpose(1, 0, 2).astype(BF16),
        kng=lanes(_row(mla_k_norm_g)),
        wukt=jnp.pad(uk.transpose(1, 2, 0), ((0, 0), (0, LANE - MLA_NOPE), (0, 0))).astype(BF16),
        wuk_rows=uk.transpose(2, 1, 0).reshape(MLA_NOPE * MLA_HEADS, MLA_KV_LORA).astype(BF16),
        wuv_h=uv.transpose(1, 0, 2).astype(BF16), wo=mla_w_o.astype(BF16))


def _rope_tables(pos):
    half = MLA_ROPE // 2
    inv = ROPE_BASE ** (-jnp.arange(half, dtype=F32) / half)
    ang = pos.astype(F32)[:, None] * inv[None, :]
    cos, sin = jnp.cos(ang), jnp.sin(ang)
    r = pos.shape[0]
    ones = jnp.ones((r, MLA_NOPE), F32)
    z32 = jnp.zeros((r, LANE - MLA_QK), F32)
    z64 = jnp.zeros((r, MLA_NOPE), F32)
    tc = jnp.concatenate([ones, cos, cos, z32], axis=1)
    ts = jnp.concatenate([z64, -sin, sin, z32], axis=1)
    cs = jnp.concatenate([cos, cos, sin, sin], axis=1)
    return (tc, ts), cs


def _prompt_trunk(x_prompt, meta_tokens, norm_mix_g, norm_ffn_g, wm, wffn, pool_w, pool_scale,
                  wmla, wconv):
    n, seq, _ = x_prompt.shape
    t_real = seq + N_META
    tm = 768
    tp = -(-t_real // tm) * tm
    pad = tp - t_real
    tps = tp // tm
    meta = jnp.broadcast_to(meta_tokens[None].astype(F32), (n, N_META, D_MODEL))
    x = jnp.concatenate([jnp.zeros((n, pad, D_MODEL), F32), meta, x_prompt], axis=1)
    x = x.reshape(n * tp, D_MODEL)
    kw = dict(tm=tm, tps=tps, pad=pad)
    st = {}
    ffn_hist = []

    def ffn(x, layer):
        y, tail = _ffn(x, _row(norm_ffn_g[layer]), wffn[layer], **kw)
        tail = tail.reshape(n, tps, N_FF_CHUNK, 8, FF_CHUNK)[:, -1, :, 8 - FFN_HIST:, :]
        ffn_hist.append(tail.transpose(0, 2, 1, 3).reshape(n, FFN_HIST, D_FF))
        return y

    q, k, v, o, gt = _m_in(x, _row(norm_mix_g[0]), wm, **kw)
    grow = gt[:, :8].reshape(n, tp, 8).transpose(0, 2, 1)
    zc = jnp.zeros((n, M_HEADS, M_DK, M_DV), F32)
    zn = jnp.zeros((n, M_HEADS, 1, M_DK), F32)
    zm = jnp.zeros((n, M_HEADS, 1, 1), F32)
    hh, c_new, n_new, m_new = _m_chunk(q, k, v, gt, grow, zc, zn, zm, ln=tm // 2)
    st["mlstm_C"] = c_new
    st["mlstm_n"] = n_new.reshape(n, M_HEADS, M_DK)
    st["mlstm_m"] = m_new.reshape(n, M_HEADS)
    x = _m_out(hh, o, x, wm["hg"], wm["wout"], **kw)
    x = ffn(x, 0)

    x, tail = _pool(x, _row(norm_mix_g[1]), pool_w.astype(BF16), _row(pool_scale), **kw)
    st["pool"] = tail.reshape(n, tps, 16, D_MODEL)[:, -1, 16 - POOL_HIST:, :]
    x = ffn(x, 1)

    pos = jnp.maximum(jnp.arange(tp, dtype=jnp.int32) - pad, 0)
    tabs, _ = _rope_tables(pos)
    qh, kh, vp, lat, kpe = _mla_in(x, _row(norm_mix_g[2]), wmla, tabs, tm=tm, qdtype=BF16)
    rows = jnp.concatenate([lat, kpe[:, MLA_NOPE:MLA_QK]], axis=1)
    st["mla"] = rows.reshape(n, tp, MLA_CACHE_DIM)[:, pad:, :]
    o = _attn(qh, kh, vp, n=n, tp=tp, tq=tm, pad=pad)
    x = _proj_out(o, x, wmla["wo"], **kw)
    x = ffn(x, 2)

    gl = _conv_in(x, _row(norm_mix_g[3]), wconv["w1"], wconv["b1"], **kw)
    st["conv"] = gl.reshape(n, tp, D_MODEL)[:, tp - CONV_HIST:, :]
    x = _conv_out(gl, x, wconv, **kw)
    x = ffn(x, 3)

    st["ffn"] = jnp.stack(ffn_hist, axis=0)
    y = x.reshape(n, tp, D_MODEL)[:, pad + N_META:, :]
    return y, st


def _sample_trunk(x_sample, state_mlstm_C, state_mlstm_n, state_mlstm_m, state_pool, cache_mla,
                  page_table, state_conv, state_ffn, norm_mix_g, norm_ffn_g, wm, wffn, pool_w,
                  pool_scale, wmla, wconv):
    nb, nt, _ = x_sample.shape
    r = nb * nt
    past_len = page_table.shape[1] * cache_mla.shape[1]
    tmaj = lambda a: a.transpose(1, 0, 2).reshape(a.shape[1] * nb, a.shape[2])
    smaj = lambda a: a.reshape(-1, nb, a.shape[-1]).transpose(1, 0, 2)
    x = tmaj(x_sample.astype(F32))
    kw = dict(tm=r, tps=1, pad=0)
    st = {}
    ffn_hist = []

    def ffn(x, layer):
        y, tail = _ffn_s(x, _row(norm_ffn_g[layer]), tmaj(state_ffn[layer]), wffn[layer], nb=nb)
        ffn_hist.append(smaj(tail))
        return y

    ln = 16
    q, k, v, o, gt = _m_in(x, _row(norm_mix_g[0]), wm, **kw)

    def seq_rows(a, fill=None):
        a = smaj(a)
        if fill is None:
            a = jnp.pad(a, ((0, 0), (0, ln - nt), (0, 0)))
        else:
            a = jnp.concatenate([a, jnp.broadcast_to(fill, (nb, ln - nt, a.shape[2]))], axis=1)
        return a.reshape(nb * ln, a.shape[2])

    lane = jnp.arange(LANE)
    neutral = jnp.where(lane < M_HEADS, NEG, 0.0).astype(F32)
    gts = seq_rows(gt, fill=neutral)
    grow = gts[:, :8].reshape(nb, ln, 8).transpose(0, 2, 1)
    hh, c_new, n_new, m_new = _m_chunk(
        seq_rows(q), seq_rows(k), seq_rows(v), gts, grow, state_mlstm_C.astype(F32),
        state_mlstm_n.astype(F32).reshape(nb, M_HEADS, 1, M_DK),
        state_mlstm_m.astype(F32).reshape(nb, M_HEADS, 1, 1), ln=ln, nseq=4)
    st["mlstm_C"] = c_new
    st["mlstm_n"] = n_new.reshape(nb, M_HEADS, M_DK)
    st["mlstm_m"] = m_new.reshape(nb, M_HEADS)
    hh = tmaj(hh.reshape(nb, ln, M_HEADS * M_DV)[:, :nt, :])
    x = _m_out(hh, o, x, wm["hg"], wm["wout"], **kw)
    x = ffn(x, 0)

    x, u = _pool_s(x, _row(norm_mix_g[1]), tmaj(state_pool.astype(F32)), pool_w.astype(BF16),
                   _row(pool_scale), nb=nb, past_len=past_len)
    st["pool"] = jnp.concatenate([state_pool.astype(F32), smaj(u)], axis=1)[:, -POOL_HIST:, :]
    x = ffn(x, 1)

    pos_new = past_len + jnp.repeat(jnp.arange(nt, dtype=jnp.int32), nb)
    tabs, _ = _rope_tables(pos_new)
    _, cs = _rope_tables(jnp.arange(past_len + LANE, dtype=jnp.int32))
    qh, _, _, lat, kpe = _mla_in(x, _row(norm_mix_g[2]), wmla, tabs, tm=r, qdtype=F32)
    rows = smaj(jnp.concatenate([lat, kpe[:, MLA_NOPE:MLA_QK]], axis=1))
    st["mla"] = rows
    qabs, ab = _dec_prep(qh, wmla["kng"], wmla["wukt"])
    seq_q = lambda a: (a.reshape(MLA_HEADS, nt, nb, a.shape[-1]).transpose(2, 1, 0, 3)
                       .reshape(nb, nt * MLA_HEADS, a.shape[-1]))
    new_t = _pad_lanes(rows.transpose(0, 2, 1))
    cache_t = cache_mla.astype(F32).transpose(0, 2, 1)
    ol = _decode(page_table, cache_t, new_t, seq_q(qabs), seq_q(ab), wmla["wuk_rows"], cs.T)
    ol = ol.reshape(nb, nt, MLA_HEADS, MLA_KV_LORA).transpose(2, 1, 0, 3).reshape(MLA_HEADS, r, MLA_KV_LORA)
    x = _dec_out(ol, wmla["wuv_h"], wmla["wo"], x)
    x = ffn(x, 2)

    gl = _conv_in(x, _row(norm_mix_g[3]), wconv["w1"], wconv["b1"], **kw)
    st["conv"] = jnp.concatenate([state_conv.astype(F32), smaj(gl)], axis=1)[:, -CONV_HIST:, :]
    x = _conv_out_s(tmaj(state_conv.astype(F32)), gl, x, wconv, nb=nb)
    x = ffn(x, 3)

    st["ffn"] = jnp.stack(ffn_hist, axis=0)
    return smaj(x), st


def kernel(x_prompt, x_sample, state_mlstm_C, state_mlstm_n, state_mlstm_m, state_pool, cache_mla, page_table, state_conv, state_ffn, meta_tokens, norm_mix_g, norm_ffn_g, m_w_in, m_b_gates, m_head_norm_g, m_w_out, pool_w, pool_scale, mla_w_dq, mla_q_lora_g, mla_w_uq, mla_w_dkv, mla_kv_lora_g, mla_w_ukv, mla_q_norm_g, mla_k_norm_g, mla_w_o, conv_w_pw1, conv_b_pw1, conv_w_dw, conv_b_dw, conv_ln_g, conv_ln_b, conv_w_pw2, conv_b_pw2, ffn_w_up, ffn_conv_w, ffn_conv_b, ffn_w_down):
    depth = ffn_w_up.shape[0]
    wm = _prep_mlstm(m_w_in, m_b_gates, m_head_norm_g, m_w_out)
    wffn = [_prep_ffn(ffn_w_up[i], ffn_conv_w[i], ffn_conv_b[i], ffn_w_down[i]) for i in range(depth)]
    wmla = _prep_mla(mla_w_dq, mla_q_lora_g, mla_w_uq, mla_w_dkv, mla_kv_lora_g, mla_w_ukv,
                     mla_q_norm_g, mla_k_norm_g, mla_w_o)
    wconv = dict(w1=conv_w_pw1.astype(BF16), b1=_row(conv_b_pw1),
                 wdw=jnp.pad(conv_w_dw, ((0, 32 - CONV_WIDTH), (0, 0))).astype(F32),
                 bdw=_row(conv_b_dw), lng=_row(conv_ln_g), lnb=_row(conv_ln_b),
                 w2=conv_w_pw2.astype(BF16), b2=_row(conv_b_pw2))
    yp, sp = _prompt_trunk(x_prompt, meta_tokens, norm_mix_g, norm_ffn_g, wm, wffn, pool_w,
                           pool_scale, wmla, wconv)
    ys, ss = _sample_trunk(x_sample, state_mlstm_C, state_mlstm_n, state_mlstm_m, state_pool,
                           cache_mla, page_table, state_conv, state_ffn, norm_mix_g, norm_ffn_g,
                           wm, wffn, pool_w, pool_scale, wmla, wconv)
    names = ("mlstm_C", "mlstm_n", "mlstm_m", "pool", "mla", "conv", "ffn")
    return (yp, ys) + tuple(sp[k] for k in names) + tuple(ss[k] for k in names)
```

```python
import functools
import math

import jax
import jax.numpy as jnp
from jax import lax
from jax.experimental import pallas as pl
from jax.experimental.pallas import tpu as pltpu

F32 = jnp.float32
BF16 = jnp.bfloat16

D_MODEL = 1024
N_META = 16
EPS = 1e-6
M_HEADS = 4
M_DK = 128
M_DV = 256
POOL_WINDOWS = (2, 4, 8, 16)
POOL_GDIM = 256
POOL_HIST = 15
MLA_HEADS = 16
MLA_NOPE = 64
MLA_ROPE = 32
MLA_QK = 96
MLA_V = 64
MLA_Q_LORA = 512
MLA_KV_LORA = 256
MLA_CACHE_DIM = 288
ROPE_BASE = 10000.0
CONV_WIDTH = 31
CONV_HIST = 30
D_FF = 2816
FFN_HIST = 2

LANE = 128
FF_CHUNK = 256
N_FF_CHUNK = D_FF // FF_CHUNK
assert N_FF_CHUNK % 2 == 1 and N_FF_CHUNK >= 3
NEG = -1e30
LOG2E = 1.4426950408889634
VMEM_LIMIT = 56 * 1024 * 1024


def _dot(a, b):
    return jnp.dot(a, b, preferred_element_type=F32)


def _dot_nt(a, b):
    return lax.dot_general(a, b, (((1,), (1,)), ((), ())), preferred_element_type=F32)


def _dot_tn(a, b):
    return lax.dot_general(a, b, (((0,), (0,)), ((), ())), preferred_element_type=F32)


def _rms(xf, g):
    return xf * lax.rsqrt(jnp.mean(xf * xf, axis=-1, keepdims=True) + EPS) * g


def _sigmoid(x):
    return 1.0 / (1.0 + jnp.exp(-x))


def _valid_rows(i, tm, tps, pad):
    r = (i % tps) * tm + lax.broadcasted_iota(jnp.int32, (tm, 1), 0)
    return r >= pad


def _params(n_axes):
    return pltpu.CompilerParams(dimension_semantics=("arbitrary",) * n_axes,
                                vmem_limit_bytes=VMEM_LIMIT)


def _row_spec(tm, c):
    return pl.BlockSpec((tm, c), lambda i: (i, 0))


def _full_spec(shape):
    zeros = (0,) * len(shape)
    return pl.BlockSpec(shape, lambda *_: zeros)


def _sds(shape, dtype):
    return jax.ShapeDtypeStruct(shape, dtype)


def _m_in_kernel(x_ref, g_ref, wq, wk, wv, wo, wgh, wgl, bg,
                 q_o, k_o, v_o, o_o, gt_o, *, tps, pad):
    tm = x_ref.shape[0]
    hf = _rms(x_ref[...], g_ref[...])
    h = hf.astype(BF16)
    q_o[...] = (_dot(h, wq[...]) * (M_DK ** -0.5)).astype(BF16)
    k_o[...] = _dot(h, wk[...]).astype(BF16)
    v_o[...] = _dot(h, wv[...]).astype(BF16)
    o_o[...] = _sigmoid(_dot(h, wo[...])).astype(BF16)
    hl = (hf - h.astype(F32)).astype(BF16)
    gts = _dot(h, wgh[...]) + _dot(hl, wgh[...]) + _dot(h, wgl[...]) + bg[...]
    lane = lax.broadcasted_iota(jnp.int32, gts.shape, 1)
    lf = jnp.minimum(gts, 0.0) - jnp.log(1.0 + jnp.exp(-jnp.abs(gts)))
    out = jnp.where(lane < M_HEADS, gts, lf)
    if pad:
        valid = _valid_rows(pl.program_id(0), tm, tps, pad)
        out = jnp.where(valid, out, jnp.where(lane < M_HEADS, NEG, 0.0))
    gt_o[...] = out


def _m_in(x, g, w, *, tm, tps, pad):
    r = x.shape[0]
    kern = functools.partial(_m_in_kernel, tps=tps, pad=pad)
    return pl.pallas_call(
        kern, grid=(r // tm,),
        in_specs=[_row_spec(tm, D_MODEL), _full_spec((1, D_MODEL)),
                  _full_spec(w["wq"].shape), _full_spec(w["wk"].shape), _full_spec(w["wv"].shape),
                  _full_spec(w["wo"].shape), _full_spec(w["wgh"].shape), _full_spec(w["wgl"].shape),
                  _full_spec((1, LANE))],
        out_specs=[_row_spec(tm, 512), _row_spec(tm, 512), _row_spec(tm, 1024),
                   _row_spec(tm, 1024), _row_spec(tm, LANE)],
        out_shape=[_sds((r, 512), BF16), _sds((r, 512), BF16), _sds((r, 1024), BF16),
                   _sds((r, 1024), BF16), _sds((r, LANE), F32)],
        compiler_params=_params(1), name="mlstm_in",
    )(x, g, w["wq"], w["wk"], w["wv"], w["wo"], w["wgh"], w["wgl"], w["bg"])


def _split3(a):
    a1 = a.astype(BF16)
    r1 = a - a1.astype(F32)
    a2 = r1.astype(BF16)
    a3 = (r1 - a2.astype(F32)).astype(BF16)
    return a1, a2, a3


def _m_chunk_kernel(q_ref, k_ref, v_ref, gc_ref, gr_ref, c0_ref, n0_ref, m0_ref,
                    h_ref, c_o, n_o, m_o, c_s, n_s, m_s, *, ln):
    c = pl.program_id(1)
    nseq = q_ref.shape[0] // ln

    @pl.when(c == 0)
    def _():
        c_s[...] = c0_ref[...]
        n_s[...] = n0_ref[...]
        m_s[...] = m0_ref[...]

    row = lax.broadcasted_iota(jnp.int32, (ln, ln), 0)
    col = lax.broadcasted_iota(jnp.int32, (ln, ln), 1)
    causal = col <= row
    tril = jnp.where(causal, 1.0, 0.0).astype(BF16)
    triu = jnp.where(row <= col, 1.0, 0.0).astype(BF16)
    for sq in range(nseq):
        rows = slice(sq * ln, (sq + 1) * ln)
        gc = gc_ref[rows, :]
        gr = gr_ref[sq]
        c1, c2, c3 = _split3(gc)
        b_col = _dot(tril, c1) + _dot(tril, c2) + _dot(tril, c3)
        r1, r2, r3 = _split3(gr)
        b_row = _dot(r1, triu) + _dot(r2, triu) + _dot(r3, triu)

        for hd in range(M_HEADS):
            ig_c = gc[:, hd:hd + 1]
            b_c = b_col[:, M_HEADS + hd:M_HEADS + hd + 1]
            ig_r = gr[hd:hd + 1, :]
            b_r = b_row[M_HEADS + hd:M_HEADS + hd + 1, :]
            m0 = m_s[sq, hd]
            c0 = c_s[sq, hd]
            n0 = n_s[sq, hd]
            dmat = jnp.where(causal, b_c - b_r + ig_r, NEG)
            inter = b_c + m0
            m_t = jnp.maximum(inter, jnp.max(dmat, axis=1, keepdims=True))
            w_intra = jnp.exp(dmat - m_t)
            w_inter = jnp.exp(inter - m_t)
            qh = q_ref[rows, hd * M_DK:(hd + 1) * M_DK]
            kh = k_ref[rows, hd * M_DK:(hd + 1) * M_DK]
            vh = v_ref[rows, hd * M_DV:(hd + 1) * M_DV]
            s = _dot_nt(qh, kh) * w_intra
            num = _dot(s.astype(BF16), vh) + w_inter * _dot(qh, c0.astype(BF16))
            den = (jnp.sum(s, axis=1, keepdims=True)
                   + w_inter * jnp.sum(qh.astype(F32) * n0, axis=1, keepdims=True))
            h_ref[rows, hd * M_DV:(hd + 1) * M_DV] = num / jnp.maximum(jnp.abs(den), jnp.exp(-m_t))
            b_last = b_c[ln - 1:ln, :]
            m_new = m_t[ln - 1:ln, :]
            wk_c = jnp.exp(b_last - b_c + ig_c - m_new)
            decay = jnp.exp(b_last + m0 - m_new)
            vw = (vh.astype(F32) * wk_c).astype(BF16)
            c_s[sq, hd] = decay * c0 + _dot_tn(kh, vw)
            n_s[sq, hd] = decay * n0 + jnp.sum(kh.astype(F32) * wk_c, axis=0, keepdims=True)
            m_s[sq, hd] = m_new

    @pl.when(c == pl.num_programs(1) - 1)
    def _():
        c_o[...] = c_s[...]
        n_o[...] = n_s[...]
        m_o[...] = m_s[...]


def _m_chunk(q, k, v, gcol, grow, c0, n0, m0, *, ln, nseq=1):
    n = c0.shape[0]
    t = q.shape[0] // n
    nc = t // ln
    assert n % nseq == 0 and (nseq == 1 or nc == 1)
    rows = lambda w: pl.BlockSpec((nseq * ln, w), lambda i, c: (i * nc + c, 0))
    st_c = pl.BlockSpec((nseq, M_HEADS, M_DK, M_DV), lambda i, c: (i, 0, 0, 0))
    st_n = pl.BlockSpec((nseq, M_HEADS, 1, M_DK), lambda i, c: (i, 0, 0, 0))
    st_m = pl.BlockSpec((nseq, M_HEADS, 1, 1), lambda i, c: (i, 0, 0, 0))
    return pl.pallas_call(
        functools.partial(_m_chunk_kernel, ln=ln), grid=(n // nseq, nc),
        in_specs=[rows(512), rows(512), rows(1024), rows(LANE),
                  pl.BlockSpec((nseq, 8, ln), lambda i, c: (i, 0, c)), st_c, st_n, st_m],
        out_specs=[rows(1024), st_c, st_n, st_m],
        out_shape=[_sds((n * t, 1024), F32), _sds((n, M_HEADS, M_DK, M_DV), F32),
                   _sds((n, M_HEADS, 1, M_DK), F32), _sds((n, M_HEADS, 1, 1), F32)],
        scratch_shapes=[pltpu.VMEM((nseq, M_HEADS, M_DK, M_DV), F32),
                        pltpu.VMEM((nseq, M_HEADS, 1, M_DK), F32),
                        pltpu.VMEM((nseq, M_HEADS, 1, 1), F32)],
        compiler_params=_params(2), name="mlstm_chunk",
    )(q, k, v, gcol, grow, c0, n0, m0)


def _m_out_kernel(hh_ref, o_ref, x_ref, g_ref, w_ref, y_ref, *, tps, pad):
    tm = x_ref.shape[0]
    parts = []
    for hd in range(M_HEADS):
        sl = slice(hd * M_DV, (hd + 1) * M_DV)
        hn = _rms(hh_ref[:, sl], g_ref[:, sl])
        parts.append((hn * o_ref[:, sl].astype(F32)).astype(BF16))
    y = x_ref[...] + _dot(jnp.concatenate(parts, axis=1), w_ref[...])
    if pad:
        y = jnp.where(_valid_rows(pl.program_id(0), tm, tps, pad), y, 0.0)
    y_ref[...] = y


def _m_out(hh, o, x, g, w, *, tm, tps, pad):
    r = x.shape[0]
    kern = functools.partial(_m_out_kernel, tps=tps, pad=pad)
    return pl.pallas_call(
        kern, grid=(r // tm,),
        in_specs=[_row_spec(tm, 1024), _row_spec(tm, 1024), _row_spec(tm, D_MODEL),
                  _full_spec((1, 1024)), _full_spec((1024, D_MODEL))],
        out_specs=_row_spec(tm, D_MODEL), out_shape=_sds((r, D_MODEL), F32),
        compiler_params=_params(1), name="mlstm_out",
    )(hh, o, x, g, w)


def _ffn_kernel(x_ref, g_ref, wg_ref, wu_ref, cw_ref, cb_ref, wd_ref,
                y_ref, tail_ref, carry_s, gu_s, act_s, *, tps, pad):
    i = pl.program_id(0)
    tm = x_ref.shape[0]
    x = x_ref[...]
    h = _rms(x, g_ref[...]).astype(BF16)

    @pl.when(i % tps == 0)
    def _():
        carry_s[...] = jnp.zeros_like(carry_s)

    rowid = lax.broadcasted_iota(jnp.int32, (tm, 1), 0)

    def up(c, slot):
        gu_s[slot, 0] = _dot(h, wg_ref[c])
        gu_s[slot, 1] = _dot(h, wu_ref[c])

    def gate(c, slot):
        g = gu_s[slot, 0]
        u = gu_s[slot, 1]
        prev = carry_s[c]
        g1 = jnp.where(rowid == 0, prev[7:8], pltpu.roll(g, 1, 0))
        g2 = jnp.where(rowid == 0, prev[6:7], jnp.where(rowid == 1, prev[7:8], pltpu.roll(g, 2, 0)))
        cw = cw_ref[c]
        gc = cw[0:1] * g2 + cw[1:2] * g1 + cw[2:3] * g + cb_ref[c]
        act_s[slot] = (gc * _sigmoid(gc) * u).astype(BF16)
        last = g[tm - 8:tm]
        carry_s[c] = last
        tail_ref[0, c] = last

    def down(c, slot):
        return _dot(act_s[slot], wd_ref[c])

    y_ref[...] = x
    up(0, 0)
    up(1, 1)
    gate(0, 0)
    for t in range(2, N_FF_CHUNK - 1, 2):
        up(t, 0)
        gate(t - 1, 1)
        d0 = down(t - 2, 0)
        up(t + 1, 1)
        gate(t, 0)
        d1 = down(t - 1, 1)
        y_ref[...] += d0 + d1
    last = N_FF_CHUNK - 1
    up(last, 0)
    gate(last - 1, 1)
    d0 = down(last - 2, 0)
    gate(last, 0)
    d1 = down(last - 1, 1)
    y = y_ref[...] + d0 + d1 + down(last, 0)
    if pad:
        y = jnp.where(_valid_rows(i, tm, tps, pad), y, 0.0)
    y_ref[...] = y


def _ffn(x, g, w, *, tm, tps, pad):
    r = x.shape[0]
    nt = r // tm
    kern = functools.partial(_ffn_kernel, tps=tps, pad=pad)
    return pl.pallas_call(
        kern, grid=(nt,),
        in_specs=[_row_spec(tm, D_MODEL), _full_spec((1, D_MODEL)),
                  _full_spec(w["wg"].shape), _full_spec(w["wu"].shape), _full_spec(w["cw"].shape),
                  _full_spec(w["cb"].shape), _full_spec(w["wd"].shape)],
        out_specs=[_row_spec(tm, D_MODEL),
                   pl.BlockSpec((1, N_FF_CHUNK, 8, FF_CHUNK), lambda i: (i, 0, 0, 0))],
        out_shape=[_sds((r, D_MODEL), F32), _sds((nt, N_FF_CHUNK, 8, FF_CHUNK), F32)],
        scratch_shapes=[pltpu.VMEM((N_FF_CHUNK, 8, FF_CHUNK), F32),
                        pltpu.VMEM((2, 2, tm, FF_CHUNK), F32), pltpu.VMEM((2, tm, FF_CHUNK), BF16)],
        compiler_params=_params(1), name="ffn",
    )(x, g, w["wg"], w["wu"], w["cw"], w["cb"], w["wd"])


def _ffn_s_kernel(x_ref, g_ref, hist_ref, wg_ref, wu_ref, cw_ref, cb_ref, wd_ref,
                  y_ref, tail_ref, acc_s, *, nb):
    c = pl.program_id(0)
    x = x_ref[...]
    h = _rms(x, g_ref[...]).astype(BF16)

    @pl.when(c == 0)
    def _():
        acc_s[...] = jnp.zeros_like(acc_s)

    g = _dot(h, wg_ref[0])
    u = _dot(h, wu_ref[0])
    hist = hist_ref[...]
    t = g.shape[0]
    ext0 = jnp.concatenate([hist, g[:t - 2 * nb]], axis=0)
    ext1 = jnp.concatenate([hist[nb:], g[:t - nb]], axis=0)
    cw = cw_ref[0]
    gc = cw[0:1] * ext0 + cw[1:2] * ext1 + cw[2:3] * g + cb_ref[0]
    act = (gc * _sigmoid(gc) * u).astype(BF16)
    acc_s[...] += _dot(act, wd_ref[0])
    tail_ref[...] = g[t - 2 * nb:]

    @pl.when(c == pl.num_programs(0) - 1)
    def _():
        y_ref[...] = x + acc_s[...]


def _ffn_s(x, g, hist, w, *, nb):
    r = x.shape[0]
    kern = functools.partial(_ffn_s_kernel, nb=nb)
    chunk3 = lambda a, b: pl.BlockSpec((1, a, b), lambda c: (c, 0, 0))
    return pl.pallas_call(
        kern, grid=(N_FF_CHUNK,),
        in_specs=[_full_spec((r, D_MODEL)), _full_spec((1, D_MODEL)),
                  pl.BlockSpec((2 * nb, FF_CHUNK), lambda c: (0, c)),
                  chunk3(D_MODEL, FF_CHUNK), chunk3(D_MODEL, FF_CHUNK), chunk3(8, FF_CHUNK),
                  chunk3(1, FF_CHUNK), chunk3(FF_CHUNK, D_MODEL)],
        out_specs=[_full_spec((r, D_MODEL)), pl.BlockSpec((2 * nb, FF_CHUNK), lambda c: (0, c))],
        out_shape=[_sds((r, D_MODEL), F32), _sds((2 * nb, D_FF), F32)],
        scratch_shapes=[pltpu.VMEM((r, D_MODEL), F32)],
        compiler_params=_params(1), name="ffn_sample",
    )(x, g, hist, w["wg"], w["wu"], w["cw"], w["cb"], w["wd"])


def _pool_kernel(x_ref, g_ref, pw_ref, ps_ref, y_ref, tail_ref, ext_s, *, tps, pad):
    i = pl.program_id(0)
    tm = x_ref.shape[0]
    x = x_ref[...]
    u = _rms(x, g_ref[...])
    hist = POOL_HIST + 1

    @pl.when(i % tps == 0)
    def _():
        ext_s[0:hist] = jnp.zeros((hist, D_MODEL), F32)

    @pl.when(i % tps != 0)
    def _():
        ext_s[0:hist] = ext_s[tm:tm + hist]

    ext_s[hist:hist + tm] = u
    tail_ref[0] = u[tm - hist:]
    pos = (i % tps) * tm + lax.broadcasted_iota(jnp.int32, (tm, 1), 0) - pad
    valid = pos >= 0
    for gi, w in enumerate(POOL_WINDOWS):
        sl = slice(gi * POOL_GDIM, (gi + 1) * POOL_GDIM)
        ug = u[:, sl]
        win = ug
        for j in range(1, w):
            win = win + ext_s[hist - j:hist - j + tm, sl]
        cnt = jnp.maximum(jnp.minimum(pos + 1, w), 1).astype(F32)
        d = (win / cnt - ug).astype(BF16)
        y = x[:, sl] + _dot(d, pw_ref[gi]) * ps_ref[:, sl]
        y_ref[:, sl] = jnp.where(valid, y, 0.0)


def _pool(x, g, pw, ps, *, tm, tps, pad):
    r = x.shape[0]
    nt = r // tm
    kern = functools.partial(_pool_kernel, tps=tps, pad=pad)
    return pl.pallas_call(
        kern, grid=(nt,),
        in_specs=[_row_spec(tm, D_MODEL), _full_spec((1, D_MODEL)),
                  _full_spec((4, POOL_GDIM, POOL_GDIM)), _full_spec((1, D_MODEL))],
        out_specs=[_row_spec(tm, D_MODEL), pl.BlockSpec((1, 16, D_MODEL), lambda i: (i, 0, 0))],
        out_shape=[_sds((r, D_MODEL), F32), _sds((nt, 16, D_MODEL), F32)],
        scratch_shapes=[pltpu.VMEM((tm + 16, D_MODEL), F32)],
        compiler_params=_params(1), name="pool",
    )(x, g, pw, ps)


def _pool_s_kernel(x_ref, g_ref, hist_ref, pw_ref, ps_ref, y_ref, u_ref, *, nb, past_len):
    x = x_ref[...]
    u = _rms(x, g_ref[...])
    u_ref[...] = u
    nt = x.shape[0] // nb
    for gi, w in enumerate(POOL_WINDOWS):
        sl = slice(gi * POOL_GDIM, (gi + 1) * POOL_GDIM)
        outs = []
        for t in range(nt):
            win = u[t * nb:(t + 1) * nb, sl]
            for j in range(1, w):
                src = t - j
                if src >= 0:
                    win = win + u[src * nb:(src + 1) * nb, sl]
                else:
                    hrow = POOL_HIST + src
                    win = win + hist_ref[hrow * nb:(hrow + 1) * nb, sl]
            cnt = float(min(past_len + t + 1, w))
            outs.append(win / cnt - u[t * nb:(t + 1) * nb, sl])
        d = jnp.concatenate(outs, axis=0).astype(BF16)
        y_ref[:, sl] = x[:, sl] + _dot(d, pw_ref[gi]) * ps_ref[:, sl]


def _pool_s(x, g, hist, pw, ps, *, nb, past_len):
    r = x.shape[0]
    kern = functools.partial(_pool_s_kernel, nb=nb, past_len=past_len)
    return pl.pallas_call(
        kern, grid=(1,),
        in_specs=[_full_spec((r, D_MODEL)), _full_spec((1, D_MODEL)), _full_spec(hist.shape),
                  _full_spec((4, POOL_GDIM, POOL_GDIM)), _full_spec((1, D_MODEL))],
        out_specs=[_full_spec((r, D_MODEL)), _full_spec((r, D_MODEL))],
        out_shape=[_sds((r, D_MODEL), F32), _sds((r, D_MODEL), F32)],
        compiler_params=_params(1), name="pool_sample",
    )(x, g, hist, pw, ps)


def _rope_norm(z, g, tc, ts, live):
    zn = z * lax.rsqrt(jnp.sum(z * z * live, axis=-1, keepdims=True) * (1.0 / MLA_QK) + EPS) * g
    return zn * tc + pltpu.roll(zn, LANE - 16, 1) * ts


def _mla_in_kernel(x_ref, g_ref, wdq, qlg, wuq, qng, wlat, wkpe, kvg, wuk, wuv, kng,
                   tc_ref, ts_ref, qh_o, kh_o, vp_o, lat_o, kpe_o):
    h = _rms(x_ref[...], g_ref[...]).astype(BF16)
    cqn = _rms(_dot(h, wdq[...]), qlg[...]).astype(BF16)
    latn = _rms(_dot(h, wlat[...]), kvg[...])
    lat_o[...] = latn
    latb = latn.astype(BF16)
    kpe = _dot(h, wkpe[...])
    kpe_o[...] = kpe
    tc, ts = tc_ref[...], ts_ref[...]
    live = jnp.where(lax.broadcasted_iota(jnp.int32, (1, LANE), 1) < MLA_QK, 1.0, 0.0)
    for j in range(MLA_HEADS // 2):
        q2 = _dot(cqn, wuq[j])
        k2 = _dot(latb, wuk[j])
        for half in range(2):
            sl = slice(half * LANE, (half + 1) * LANE)
            qh_o[2 * j + half] = _rope_norm(q2[:, sl], qng[...], tc, ts, live).astype(qh_o.dtype)
            kh_o[2 * j + half] = _rope_norm(k2[:, sl] + kpe, kng[...], tc, ts, live).astype(BF16)
    ones = jnp.ones((latb.shape[0], LANE), BF16)
    for j in range(MLA_HEADS // 4):
        v4 = _dot(latb, wuv[j]).astype(BF16)
        vp_o[2 * j] = jnp.concatenate([v4[:, :LANE], ones], axis=1)
        vp_o[2 * j + 1] = jnp.concatenate([v4[:, LANE:], ones], axis=1)


def _mla_in(x, g, w, tabs, *, tm, qdtype):
    r = x.shape[0]
    heads = lambda n, w=LANE: pl.BlockSpec((n, tm, w), lambda i: (0, i, 0))
    tps = tabs[0].shape[0] // tm
    tab = pl.BlockSpec((tm, LANE), lambda i: (i % tps, 0))
    return pl.pallas_call(
        _mla_in_kernel, grid=(r // tm,),
        in_specs=[_row_spec(tm, D_MODEL), _full_spec((1, D_MODEL)),
                  _full_spec(w["wdq"].shape), _full_spec((1, MLA_Q_LORA)), _full_spec(w["wuq"].shape),
                  _full_spec((1, LANE)), _full_spec(w["wlat"].shape), _full_spec(w["wkpe"].shape),
                  _full_spec((1, MLA_KV_LORA)), _full_spec(w["wuk"].shape), _full_spec(w["wuv"].shape),
                  _full_spec((1, LANE)), tab, tab],
        out_specs=[heads(MLA_HEADS), heads(MLA_HEADS), heads(MLA_HEADS // 2, 2 * LANE),
                   _row_spec(tm, MLA_KV_LORA), _row_spec(tm, LANE)],
        out_shape=[_sds((MLA_HEADS, r, LANE), qdtype), _sds((MLA_HEADS, r, LANE), BF16),
                   _sds((MLA_HEADS // 2, r, 2 * LANE), BF16), _sds((r, MLA_KV_LORA), F32),
                   _sds((r, LANE), F32)],
        compiler_params=_params(1), name="mla_in",
    )(x, g, w["wdq"], w["qlg"], w["wuq"], w["qng"], w["wlat"], w["wkpe"], w["kvg"],
      w["wuk"], w["wuv"], w["kng"], *tabs)


def _attn_kernel(q_ref, k_ref, v_ref, o_ref, acc_s, m_s, p_s, a_s, *, pad, nsub):
    qi = pl.program_id(2)
    tq = q_ref.shape[1]
    tk = tq
    tqs = tq // nsub
    for hh in range(2):
        m_s[hh] = jnp.full((tq, LANE), NEG, F32)
        acc_s[hh] = jnp.zeros((tq, 2 * LANE), F32)

    def score(ki, masked):
        ks = pl.multiple_of(ki * tk, tk)
        for qs in range(nsub):
            rows = slice(qs * tqs, (qs + 1) * tqs)
            for hh in range(2):
                s = _dot_nt(q_ref[hh, rows, :], k_ref[hh, pl.ds(ks, tk), :])
                if masked:
                    qpos = qi * tq + qs * tqs + lax.broadcasted_iota(jnp.int32, (tqs, tk), 0)
                    kpos = ki * tk + lax.broadcasted_iota(jnp.int32, (tqs, tk), 1)
                    s = jnp.where(kpos <= qpos, jnp.where(kpos >= pad, s, NEG), NEG)
                m_old = m_s[hh, rows, :]
                m_new = jnp.maximum(m_old, jnp.max(s, axis=1, keepdims=True))
                p_s[hh, rows, :] = jnp.exp2(s - jnp.tile(m_new, (1, tk // LANE))).astype(BF16)
                a_s[hh, rows, :] = jnp.exp2(m_old - m_new)
                m_s[hh, rows, :] = m_new

    def value(ki):
        ks = pl.multiple_of(ki * tk, tk)
        vblk = v_ref[0, pl.ds(ks, tk), :]
        for qs in range(nsub):
            rows = slice(qs * tqs, (qs + 1) * tqs)
            for hh in range(2):
                acc_s[hh, rows, :] = (jnp.tile(a_s[hh, rows, :], (1, 2)) * acc_s[hh, rows, :]
                                      + _dot(p_s[hh, rows, :], vblk))

    score(0, True)
    value(0)

    def body(ki, carry):
        score(ki, False)
        value(ki)
        return carry

    lax.fori_loop(1, qi, body, 0)

    @pl.when(qi > 0)
    def _():
        score(qi, True)
        value(qi)

    lane = lax.broadcasted_iota(jnp.int32, (tq, LANE), 1)
    a0 = acc_s[0]
    a1 = acc_s[1]
    o_ref[...] = jnp.where(lane < MLA_V, a0[:, :LANE] / a0[:, LANE:],
                           a1[:, :LANE] / a1[:, LANE:]).astype(BF16)


def _attn(qh, kh, vp, *, n, tp, tq, pad):
    nq = tp // tq
    kern = functools.partial(_attn_kernel, pad=pad, nsub=4)
    return pl.pallas_call(
        kern, grid=(n, MLA_HEADS // 2, nq),
        in_specs=[pl.BlockSpec((2, tq, LANE), lambda b, hp, qi: (hp, b * nq + qi, 0)),
                  pl.BlockSpec((2, tp, LANE), lambda b, hp, qi: (hp, b, 0)),
                  pl.BlockSpec((1, tp, 2 * LANE), lambda b, hp, qi: (hp, b, 0))],
        out_specs=pl.BlockSpec((tq, LANE), lambda b, hp, qi: (b * nq + qi, hp)),
        out_shape=_sds((n * tp, MLA_HEADS * MLA_V), BF16),
        scratch_shapes=[pltpu.VMEM((2, tq, 2 * LANE), F32), pltpu.VMEM((2, tq, LANE), F32),
                        pltpu.VMEM((2, tq, tq), BF16), pltpu.VMEM((2, tq, LANE), F32)],
        compiler_params=_params(3), name="attn",
    )(qh, kh, vp)


def _proj_out_kernel(o_ref, x_ref, w_ref, y_ref, *, tps, pad):
    tm = x_ref.shape[0]
    y = x_ref[...] + _dot(o_ref[...], w_ref[...])
    if pad:
        y = jnp.where(_valid_rows(pl.program_id(0), tm, tps, pad), y, 0.0)
    y_ref[...] = y


def _proj_out(o, x, w, *, tm, tps, pad):
    r = x.shape[0]
    kern = functools.partial(_proj_out_kernel, tps=tps, pad=pad)
    return pl.pallas_call(
        kern, grid=(r // tm,),
        in_specs=[_row_spec(tm, 1024), _row_spec(tm, D_MODEL), _full_spec((1024, D_MODEL))],
        out_specs=_row_spec(tm, D_MODEL), out_shape=_sds((r, D_MODEL), F32),
        compiler_params=_params(1), name="mla_out",
    )(o, x, w)


def _dec_prep_kernel(qh_ref, kng, wukt, qabs_o, ab_o):
    g = kng[...]
    g1 = g[:, MLA_NOPE:MLA_NOPE + 16]
    g2 = g[:, MLA_NOPE + 16:MLA_QK]
    for hd in range(MLA_HEADS):
        qh = qh_ref[hd]
        qabs_o[hd] = _dot((qh * g).astype(BF16), wukt[hd]).astype(BF16)
        q1 = qh[:, MLA_NOPE:MLA_NOPE + 16]
        q2 = qh[:, MLA_NOPE + 16:MLA_QK]
        ab_o[hd] = jnp.concatenate([g1 * q1, g2 * q2, g1 * q2, -(g2 * q1)], axis=1).astype(BF16)


def _dec_prep(qh, kng, wukt):
    r = qh.shape[1]
    return pl.pallas_call(
        _dec_prep_kernel, grid=(1,),
        in_specs=[_full_spec(qh.shape), _full_spec((1, LANE)), _full_spec(wukt.shape)],
        out_specs=[_full_spec((MLA_HEADS, r, MLA_KV_LORA)), _full_spec((MLA_HEADS, r, 64))],
        out_shape=[_sds((MLA_HEADS, r, MLA_KV_LORA), BF16), _sds((MLA_HEADS, r, 64), BF16)],
        compiler_params=_params(1), name="decode_prep",
    )(qh, kng, wukt)


def _decode_kernel(pt_ref, new_ref, qm_ref, ab_ref, wuk_ref, cs_ref, *rest,
                   n_pages, page, pages_per_chunk):
    del pt_ref
    page_refs = rest[:n_pages]
    (o_ref,) = rest[n_pages:]
    nq = qm_ref.shape[1]
    lhs = jnp.concatenate([qm_ref[0], wuk_ref[...]], axis=0)
    ab = ab_ref[0]
    ck = pages_per_chunk * page
    n_chunk = n_pages // pages_per_chunk
    past = n_pages * page
    nn = new_ref.shape[2]

    def scores(lat, kpe, cs):
        n = lat.shape[1]
        latb = lat.astype(BF16)
        kcs = (jnp.concatenate([kpe, kpe], axis=0) * cs).astype(BF16)
        big = _dot(lhs, latb)
        raw = big[:nq] + _dot(ab, kcs)
        kn = big[nq:]
        ss = jnp.sum((kn * kn).reshape(MLA_NOPE, MLA_HEADS, n), axis=0)
        ss = ss + jnp.sum(kpe * kpe, axis=0, keepdims=True)
        r = lax.rsqrt(ss * (1.0 / MLA_QK) + EPS)
        return latb, jnp.concatenate([r] * (nq // MLA_HEADS), axis=0) * raw

    def fold(state, latb, s):
        m, l, acc = state
        m_new = jnp.maximum(m, jnp.max(s, axis=1, keepdims=True))
        p = jnp.exp2(s - m_new)
        alpha = jnp.exp2(m - m_new)
        l = alpha * l + jnp.sum(p, axis=1, keepdims=True)
        acc = alpha * acc + _dot_nt(p.astype(BF16), latb)
        return m_new, l, acc

    state = (jnp.full((nq, 1), NEG, F32), jnp.zeros((nq, 1), F32),
             jnp.zeros((nq, MLA_KV_LORA), F32))
    for c in range(n_chunk):
        blks = [page_refs[c * pages_per_chunk + p][0] for p in range(pages_per_chunk)]
        lat = jnp.concatenate([b[:MLA_KV_LORA, :] for b in blks], axis=1)
        kpe = jnp.concatenate([b[MLA_KV_LORA:, :] for b in blks], axis=1)
        latb, s = scores(lat, kpe, cs_ref[:, c * ck:(c + 1) * ck])
        state = fold(state, latb, s)
    blk = new_ref[0]
    latb, s = scores(blk[:MLA_KV_LORA, :], blk[MLA_KV_LORA:, :], cs_ref[:, past:past + nn])
    key = lax.broadcasted_iota(jnp.int32, (nq, nn), 1)
    tok = lax.broadcasted_iota(jnp.int32, (nq, nn), 0) // MLA_HEADS
    _, l, acc = fold(state, latb, jnp.where(key <= tok, s, NEG))
    o_ref[0] = acc / l


def _decode(page_table, cache_t, new_t, qm, ab, wuk, cs, *, pages_per_chunk=16):
    nb, n_pages = page_table.shape
    assert n_pages % pages_per_chunk == 0
    page = cache_t.shape[2]
    nn = new_t.shape[2]
    nq = qm.shape[1]
    kern = functools.partial(_decode_kernel, n_pages=n_pages, page=page,
                             pages_per_chunk=pages_per_chunk)
    per_seq = lambda shape: pl.BlockSpec((1,) + shape, lambda b, pt: (b, 0, 0))
    page_specs = [pl.BlockSpec((1, MLA_CACHE_DIM, page),
                               lambda b, pt, j=j: (pt[b * n_pages + j], 0, 0))
                  for j in range(n_pages)]
    grid_spec = pltpu.PrefetchScalarGridSpec(
        num_scalar_prefetch=1, grid=(nb,),
        in_specs=[per_seq((MLA_CACHE_DIM, nn)), per_seq((nq, MLA_KV_LORA)), per_seq((nq, 64)),
                  pl.BlockSpec(wuk.shape, lambda b, pt: (0, 0)),
                  pl.BlockSpec(cs.shape, lambda b, pt: (0, 0))] + page_specs,
        out_specs=per_seq((nq, MLA_KV_LORA)))
    return pl.pallas_call(
        kern, grid_spec=grid_spec, out_shape=_sds((nb, nq, MLA_KV_LORA), F32),
        compiler_params=_params(1), name="decode",
    )(page_table.reshape(-1), new_t, qm, ab, wuk, cs, *([cache_t] * n_pages))


def _dec_out_kernel(ol_ref, wuv, wo, x_ref, y_ref):
    parts = [_dot(ol_ref[hd].astype(BF16), wuv[hd]) for hd in range(MLA_HEADS)]
    o = jnp.concatenate(parts, axis=1).astype(BF16)
    y_ref[...] = x_ref[...] + _dot(o, wo[...])


def _dec_out(ol, wuv, wo, x):
    r = x.shape[0]
    return pl.pallas_call(
        _dec_out_kernel, grid=(1,),
        in_specs=[_full_spec(ol.shape), _full_spec(wuv.shape),
                  _full_spec(wo.shape), _full_spec((r, D_MODEL))],
        out_specs=_full_spec((r, D_MODEL)), out_shape=_sds((r, D_MODEL), F32),
        compiler_params=_params(1), name="decode_out",
    )(ol, wuv, wo, x)


def _conv_in_kernel(x_ref, g_ref, w_ref, b_ref, gl_ref, *, tps, pad):
    tm = x_ref.shape[0]
    h = _rms(x_ref[...], g_ref[...]).astype(BF16)
    a = _dot(h, w_ref[...]) + b_ref[...]
    gl = a[:, :D_MODEL] * _sigmoid(a[:, D_MODEL:])
    if pad:
        gl = jnp.where(_valid_rows(pl.program_id(0), tm, tps, pad), gl, 0.0)
    gl_ref[...] = gl


def _conv_in(x, g, w, b, *, tm, tps, pad):
    r = x.shape[0]
    kern = functools.partial(_conv_in_kernel, tps=tps, pad=pad)
    return pl.pallas_call(
        kern, grid=(r // tm,),
        in_specs=[_row_spec(tm, D_MODEL), _full_spec((1, D_MODEL)),
                  _full_spec((D_MODEL, 2 * D_MODEL)), _full_spec((1, 2 * D_MODEL))],
        out_specs=_row_spec(tm, D_MODEL), out_shape=_sds((r, D_MODEL), F32),
        compiler_params=_params(1), name="conv_in",
    )(x, g, w, b)


def _ln_silu_proj(c, x, lng, lnb, w2, b2):
    mu = jnp.mean(c, axis=-1, keepdims=True)
    cc = c - mu
    var = jnp.mean(cc * cc, axis=-1, keepdims=True)
    y = cc * lax.rsqrt(var + EPS) * lng + lnb
    a = (y * _sigmoid(y)).astype(BF16)
    return x + _dot(a, w2) + b2


CONV_COLS = 256
CONV_ROWS = 64


def _conv_out_kernel(gl_ref, x_ref, wdw, bdw, lng, lnb, w2, b2, y_ref, ext_s, sh_s, c_s, *, tps, pad):
    i = pl.program_id(0)
    tm = x_ref.shape[0]
    hist = 32
    n_ext = tm + hist

    @pl.when(i % tps == 0)
    def _():
        ext_s[0:hist] = jnp.zeros((hist, D_MODEL), F32)

    @pl.when(i % tps != 0)
    def _():
        ext_s[0:hist] = ext_s[tm:tm + hist]

    ext_s[hist:n_ext] = gl_ref[...]
    ext_s[n_ext:n_ext + 8] = jnp.zeros((8, D_MODEL), F32)
    off = hist - CONV_HIST
    for cb in range(D_MODEL // CONV_COLS):
        cols = slice(cb * CONV_COLS, (cb + 1) * CONV_COLS)
        for s in range(8):
            sh_s[s] = ext_s[s:s + n_ext, cols]
        wblk = wdw[:, cols]
        bias = bdw[:, cols]

        def rows(rb, carry):
            r0 = pl.multiple_of(rb * CONV_ROWS, CONV_ROWS)
            acc = jnp.zeros((CONV_ROWS, CONV_COLS), F32)
            for j in range(CONV_WIDTH):
                a, s = divmod(off + j, 8)
                acc = acc + wblk[j:j + 1] * sh_s[s, pl.ds(r0 + 8 * a, CONV_ROWS), :]
            c_s[pl.ds(r0, CONV_ROWS), cols] = acc + bias
            return carry

        lax.fori_loop(0, tm // CONV_ROWS, rows, 0)
    y = _ln_silu_proj(c_s[...], x_ref[...], lng[...], lnb[...], w2[...], b2[...])
    if pad:
        y = jnp.where(_valid_rows(i, tm, tps, pad), y, 0.0)
    y_ref[...] = y


def _conv_out(gl, x, w, *, tm, tps, pad):
    r = x.shape[0]
    kern = functools.partial(_conv_out_kernel, tps=tps, pad=pad)
    vec = _full_spec((1, D_MODEL))
    return pl.pallas_call(
        kern, grid=(r // tm,),
        in_specs=[_row_spec(tm, D_MODEL), _row_spec(tm, D_MODEL), _full_spec((32, D_MODEL)),
                  vec, vec, vec, _full_spec((D_MODEL, D_MODEL)), vec],
        out_specs=_row_spec(tm, D_MODEL), out_shape=_sds((r, D_MODEL), F32),
        scratch_shapes=[pltpu.VMEM((tm + 40, D_MODEL), F32), pltpu.VMEM((8, tm + 32, CONV_COLS), F32),
                        pltpu.VMEM((tm, D_MODEL), F32)],
        compiler_params=_params(1), name="conv_out",
    )(gl, x, w["wdw"], w["bdw"], w["lng"], w["lnb"], w["w2"], w["b2"])


def _conv_out_s_kernel(hist_ref, gl_ref, x_ref, wdw, bdw, lng, lnb, w2, b2, y_ref, *, nb):
    nt = x_ref.shape[0] // nb
    outs = []
    for t in range(nt):
        acc = None
        for j in range(CONV_WIDTH):
            src = t + j
            if src < CONV_HIST:
                slab = hist_ref[src * nb:(src + 1) * nb, :]
            else:
                slab = gl_ref[(src - CONV_HIST) * nb:(src - CONV_HIST + 1) * nb, :]
            term = wdw[j:j + 1, :] * slab
            acc = term if acc is None else acc + term
        outs.append(acc)
    c = jnp.concatenate(outs, axis=0) + bdw[...]
    y_ref[...] = _ln_silu_proj(c, x_ref[...], lng[...], lnb[...], w2[...], b2[...])


def _conv_out_s(hist, gl, x, w, *, nb):
    r = x.shape[0]
    kern = functools.partial(_conv_out_s_kernel, nb=nb)
    vec = _full_spec((1, D_MODEL))
    return pl.pallas_call(
        kern, grid=(1,),
        in_specs=[_full_spec(hist.shape), _full_spec((r, D_MODEL)), _full_spec((r, D_MODEL)),
                  _full_spec((32, D_MODEL)), vec, vec, vec, _full_spec((D_MODEL, D_MODEL)), vec],
        out_specs=_full_spec((r, D_MODEL)), out_shape=_sds((r, D_MODEL), F32),
        compiler_params=_params(1), name="conv_out_sample",
    )(hist, gl, x, w["wdw"], w["bdw"], w["lng"], w["lnb"], w["w2"], w["b2"])


def _pad_lanes(a, width=LANE):
    return jnp.pad(a, [(0, 0)] * (a.ndim - 1) + [(0, width - a.shape[-1])])


def _row(a):
    return a.reshape(1, -1).astype(F32)


def _prep_mlstm(m_w_in, m_b_gates, m_head_norm_g, m_w_out):
    i0 = M_HEADS * M_DK
    i1 = 2 * i0
    i2 = i1 + M_HEADS * M_DV
    i3 = i2 + M_HEADS * M_DV
    wgate = _pad_lanes(m_w_in[:, i3:])
    wgh = wgate.astype(BF16)
    wgl = (wgate - wgh.astype(F32)).astype(BF16)
    return dict(wq=m_w_in[:, :i0].astype(BF16), wk=m_w_in[:, i0:i1].astype(BF16),
                wv=m_w_in[:, i1:i2].astype(BF16), wo=m_w_in[:, i2:i3].astype(BF16),
                wgh=wgh, wgl=wgl, bg=_pad_lanes(_row(m_b_gates)),
                hg=_row(m_head_norm_g), wout=m_w_out.astype(BF16))


def _prep_ffn(w_up, conv_w, conv_b, w_down):
    chunks = lambda a: a.reshape(D_MODEL, N_FF_CHUNK, FF_CHUNK).transpose(1, 0, 2).astype(BF16)
    cw = jnp.pad(conv_w, ((0, 8 - conv_w.shape[0]), (0, 0)))
    return dict(wg=chunks(w_up[:, :D_FF]), wu=chunks(w_up[:, D_FF:]),
                cw=cw.reshape(8, N_FF_CHUNK, FF_CHUNK).transpose(1, 0, 2).astype(F32),
                cb=conv_b.reshape(N_FF_CHUNK, 1, FF_CHUNK).astype(F32),
                wd=w_down.reshape(N_FF_CHUNK, FF_CHUNK, D_MODEL).astype(BF16))


def _prep_mla(mla_w_dq, mla_q_lora_g, mla_w_uq, mla_w_dkv, mla_kv_lora_g, mla_w_ukv,
              mla_q_norm_g, mla_k_norm_g, mla_w_o):
    pair = lambda a: (a.reshape(a.shape[0], MLA_HEADS // 2, 2 * a.shape[2]).transpose(1, 0, 2))
    half1 = slice(MLA_NOPE, MLA_NOPE + MLA_ROPE // 2)
    lanes = lambda a: _pad_lanes(jnp.concatenate([a, a[..., half1]], axis=-1))
    wuq = pair(lanes(mla_w_uq.reshape(MLA_Q_LORA, MLA_HEADS, MLA_QK)))
    ukv = mla_w_ukv.reshape(MLA_KV_LORA, MLA_HEADS, MLA_NOPE + MLA_V)
    uk = ukv[:, :, :MLA_NOPE]
    uv = ukv[:, :, MLA_NOPE:]
    wkpe = lanes(jnp.pad(mla_w_dkv[:, MLA_KV_LORA:], ((0, 0), (MLA_NOPE, 0))))
    qscale = (MLA_QK ** -0.5) * LOG2E
    return dict(
        wdq=mla_w_dq.astype(BF16), qlg=_row(mla_q_lora_g), wuq=wuq.astype(BF16),
        qng=lanes(_row(mla_q_norm_g)) * qscale,
        wlat=mla_w_dkv[:, :MLA_KV_LORA].astype(BF16), wkpe=wkpe.astype(BF16),
        kvg=_row(mla_kv_lora_g), wuk=pair(_pad_lanes(uk)).astype(BF16),
        wuv=uv.reshape(MLA_KV_LORA, MLA_HEADS // 4, 4 * MLA_V).transpose(1, 0, 2).astype(BF16),
        kng=lanes(_row(mla_k_norm_g)),
        wukt=jnp.pad(uk.transpose(1, 2, 0), ((0, 0), (0, LANE - MLA_NOPE), (0, 0))).astype(BF16),
        wuk_rows=uk.transpose(2, 1, 0).reshape(MLA_NOPE * MLA_HEADS, MLA_KV_LORA).astype(BF16),
        wuv_h=uv.transpose(1, 0, 2).astype(BF16), wo=mla_w_o.astype(BF16))


def _rope_tables(pos):
    half = MLA_ROPE // 2
    inv = ROPE_BASE ** (-jnp.arange(half, dtype=F32) / half)
    ang = pos.astype(F32)[:, None] * inv[None, :]
    cos, sin = jnp.cos(ang), jnp.sin(ang)
    r = pos.shape[0]
    ones = jnp.ones((r, MLA_NOPE), F32)
    z32 = jnp.zeros((r, LANE - MLA_QK), F32)
    z64 = jnp.zeros((r, MLA_NOPE), F32)
    tc = jnp.concatenate([ones, cos, cos, z32], axis=1)
    ts = jnp.concatenate([z64, -sin, sin, z32], axis=1)
    cs = jnp.concatenate([cos, cos, sin, sin], axis=1)
    return (tc, ts), cs


def _prompt_trunk(x_prompt, meta_tokens, norm_mix_g, norm_ffn_g, wm, wffn, pool_w, pool_scale,
                  wmla, wconv):
    n, seq, _ = x_prompt.shape
    t_real = seq + N_META
    tm = 768
    tp = -(-t_real // tm) * tm
    pad = tp - t_real
    tps = tp // tm
    meta = jnp.broadcast_to(meta_tokens[None].astype(F32), (n, N_META, D_MODEL))
    x = jnp.concatenate([jnp.zeros((n, pad, D_MODEL), F32), meta, x_prompt], axis=1)
    x = x.reshape(n * tp, D_MODEL)
    kw = dict(tm=tm, tps=tps, pad=pad)
    st = {}
    ffn_hist = []

    def ffn(x, layer):
        y, tail = _ffn(x, _row(norm_ffn_g[layer]), wffn[layer], **kw)
        tail = tail.reshape(n, tps, N_FF_CHUNK, 8, FF_CHUNK)[:, -1, :, 8 - FFN_HIST:, :]
        ffn_hist.append(tail.transpose(0, 2, 1, 3).reshape(n, FFN_HIST, D_FF))
        return y

    q, k, v, o, gt = _m_in(x, _row(norm_mix_g[0]), wm, **kw)
    grow = gt[:, :8].reshape(n, tp, 8).transpose(0, 2, 1)
    zc = jnp.zeros((n, M_HEADS, M_DK, M_DV), F32)
    zn = jnp.zeros((n, M_HEADS, 1, M_DK), F32)
    zm = jnp.zeros((n, M_HEADS, 1, 1), F32)
    hh, c_new, n_new, m_new = _m_chunk(q, k, v, gt, grow, zc, zn, zm, ln=tm // 2)
    st["mlstm_C"] = c_new
    st["mlstm_n"] = n_new.reshape(n, M_HEADS, M_DK)
    st["mlstm_m"] = m_new.reshape(n, M_HEADS)
    x = _m_out(hh, o, x, wm["hg"], wm["wout"], **kw)
    x = ffn(x, 0)

    x, tail = _pool(x, _row(norm_mix_g[1]), pool_w.astype(BF16), _row(pool_scale), **kw)
    st["pool"] = tail.reshape(n, tps, 16, D_MODEL)[:, -1, 16 - POOL_HIST:, :]
    x = ffn(x, 1)

    pos = jnp.maximum(jnp.arange(tp, dtype=jnp.int32) - pad, 0)
    tabs, _ = _rope_tables(pos)
    qh, kh, vp, lat, kpe = _mla_in(x, _row(norm_mix_g[2]), wmla, tabs, tm=tm, qdtype=BF16)
    rows = jnp.concatenate([lat, kpe[:, MLA_NOPE:MLA_QK]], axis=1)
    st["mla"] = rows.reshape(n, tp, MLA_CACHE_DIM)[:, pad:, :]
    o = _attn(qh, kh, vp, n=n, tp=tp, tq=tm, pad=pad)
    x = _proj_out(o, x, wmla["wo"], **kw)
    x = ffn(x, 2)

    gl = _conv_in(x, _row(norm_mix_g[3]), wconv["w1"], wconv["b1"], **kw)
    st["conv"] = gl.reshape(n, tp, D_MODEL)[:, tp - CONV_HIST:, :]
    x = _conv_out(gl, x, wconv, **kw)
    x = ffn(x, 3)

    st["ffn"] = jnp.stack(ffn_hist, axis=0)
    y = x.reshape(n, tp, D_MODEL)[:, pad + N_META:, :]
    return y, st


def _sample_trunk(x_sample, state_mlstm_C, state_mlstm_n, state_mlstm_m, state_pool, cache_mla,
                  page_table, state_conv, state_ffn, norm_mix_g, norm_ffn_g, wm, wffn, pool_w,
                  pool_scale, wmla, wconv):
    nb, nt, _ = x_sample.shape
    r = nb * nt
    past_len = page_table.shape[1] * cache_mla.shape[1]
    tmaj = lambda a: a.transpose(1, 0, 2).reshape(a.shape[1] * nb, a.shape[2])
    smaj = lambda a: a.reshape(-1, nb, a.shape[-1]).transpose(1, 0, 2)
    x = tmaj(x_sample.astype(F32))
    kw = dict(tm=r, tps=1, pad=0)
    st = {}
    ffn_hist = []

    def ffn(x, layer):
        y, tail = _ffn_s(x, _row(norm_ffn_g[layer]), tmaj(state_ffn[layer]), wffn[layer], nb=nb)
        ffn_hist.append(smaj(tail))
        return y

    ln = 16
    q, k, v, o, gt = _m_in(x, _row(norm_mix_g[0]), wm, **kw)

    def seq_rows(a, fill=None):
        a = smaj(a)
        if fill is None:
            a = jnp.pad(a, ((0, 0), (0, ln - nt), (0, 0)))
        else:
            a = jnp.concatenate([a, jnp.broadcast_to(fill, (nb, ln - nt, a.shape[2]))], axis=1)
        return a.reshape(nb * ln, a.shape[2])

    lane = jnp.arange(LANE)
    neutral = jnp.where(lane < M_HEADS, NEG, 0.0).astype(F32)
    gts = seq_rows(gt, fill=neutral)
    grow = gts[:, :8].reshape(nb, ln, 8).transpose(0, 2, 1)
    hh, c_new, n_new, m_new = _m_chunk(
        seq_rows(q), seq_rows(k), seq_rows(v), gts, grow, state_mlstm_C.astype(F32),
        state_mlstm_n.astype(F32).reshape(nb, M_HEADS, 1, M_DK),
        state_mlstm_m.astype(F32).reshape(nb, M_HEADS, 1, 1), ln=ln, nseq=4)
    st["mlstm_C"] = c_new
    st["mlstm_n"] = n_new.reshape(nb, M_HEADS, M_DK)
    st["mlstm_m"] = m_new.reshape(nb, M_HEADS)
    hh = tmaj(hh.reshape(nb, ln, M_HEADS * M_DV)[:, :nt, :])
    x = _m_out(hh, o, x, wm["hg"], wm["wout"], **kw)
    x = ffn(x, 0)

    x, u = _pool_s(x, _row(norm_mix_g[1]), tmaj(state_pool.astype(F32)), pool_w.astype(BF16),
                   _row(pool_scale), nb=nb, past_len=past_len)
    st["pool"] = jnp.concatenate([state_pool.astype(F32), smaj(u)], axis=1)[:, -POOL_HIST:, :]
    x = ffn(x, 1)

    pos_new = past_len + jnp.repeat(jnp.arange(nt, dtype=jnp.int32), nb)
    tabs, _ = _rope_tables(pos_new)
    _, cs = _rope_tables(jnp.arange(past_len + LANE, dtype=jnp.int32))
    qh, _, _, lat, kpe = _mla_in(x, _row(norm_mix_g[2]), wmla, tabs, tm=r, qdtype=F32)
    rows = smaj(jnp.concatenate([lat, kpe[:, MLA_NOPE:MLA_QK]], axis=1))
    st["mla"] = rows
    qabs, ab = _dec_prep(qh, wmla["kng"], wmla["wukt"])
    seq_q = lambda a: (a.reshape(MLA_HEADS, nt, nb, a.shape[-1]).transpose(2, 1, 0, 3)
                       .reshape(nb, nt * MLA_HEADS, a.shape[-1]))
    new_t = _pad_lanes(rows.transpose(0, 2, 1))
    cache_t = cache_mla.astype(F32).transpose(0, 2, 1)
    ol = _decode(page_table, cache_t, new_t, seq_q(qabs), seq_q(ab), wmla["wuk_rows"], cs.T)
    ol = ol.reshape(nb, nt, MLA_HEADS, MLA_KV_LORA).transpose(2, 1, 0, 3).reshape(MLA_HEADS, r, MLA_KV_LORA)
    x = _dec_out(ol, wmla["wuv_h"], wmla["wo"], x)
    x = ffn(x, 2)

    gl = _conv_in(x, _row(norm_mix_g[3]), wconv["w1"], wconv["b1"], **kw)
    st["conv"] = jnp.concatenate([state_conv.astype(F32), smaj(gl)], axis=1)[:, -CONV_HIST:, :]
    x = _conv_out_s(tmaj(state_conv.astype(F32)), gl, x, wconv, nb=nb)
    x = ffn(x, 3)

    st["ffn"] = jnp.stack(ffn_hist, axis=0)
    return smaj(x), st


def kernel(x_prompt, x_sample, state_mlstm_C, state_mlstm_n, state_mlstm_m, state_pool, cache_mla, page_table, state_conv, state_ffn, meta_tokens, norm_mix_g, norm_ffn_g, m_w_in, m_b_gates, m_head_norm_g, m_w_out, pool_w, pool_scale, mla_w_dq, mla_q_lora_g, mla_w_uq, mla_w_dkv, mla_kv_lora_g, mla_w_ukv, mla_q_norm_g, mla_k_norm_g, mla_w_o, conv_w_pw1, conv_b_pw1, conv_w_dw, conv_b_dw, conv_ln_g, conv_ln_b, conv_w_pw2, conv_b_pw2, ffn_w_up, ffn_conv_w, ffn_conv_b, ffn_w_down):
    depth = ffn_w_up.shape[0]
    wm = _prep_mlstm(m_w_in, m_b_gates, m_head_norm_g, m_w_out)
    wffn = [_prep_ffn(ffn_w_up[i], ffn_conv_w[i], ffn_conv_b[i], ffn_w_down[i]) for i in range(depth)]
    wmla = _prep_mla(mla_w_dq, mla_q_lora_g, mla_w_uq, mla_w_dkv, mla_kv_lora_g, mla_w_ukv,
                     mla_q_norm_g, mla_k_norm_g, mla_w_o)
    wconv = dict(w1=conv_w_pw1.astype(BF16), b1=_row(conv_b_pw1),
                 wdw=jnp.pad(conv_w_dw, ((0, 32 - CONV_WIDTH), (0, 0))).astype(F32),
                 bdw=_row(conv_b_dw), lng=_row(conv_ln_g), lnb=_row(conv_ln_b),
                 w2=conv_w_pw2.astype(BF16), b2=_row(conv_b_pw2))
    yp, sp = _prompt_trunk(x_prompt, meta_tokens, norm_mix_g, norm_ffn_g, wm, wffn, pool_w,
                           pool_scale, wmla, wconv)
    ys, ss = _sample_trunk(x_sample, state_mlstm_C, state_mlstm_n, state_mlstm_m, state_pool,
                           cache_mla, page_table, state_conv, state_ffn, norm_mix_g, norm_ffn_g,
                           wm, wffn, pool_w, pool_scale, wmla, wconv)
    names = ("mlstm_C", "mlstm_n", "mlstm_m", "pool", "mla", "conv", "ffn")
    return (yp, ys) + tuple(sp[k] for k in names) + tuple(ss[k] for k in names)
```

```python
import functools
import math

import jax
import jax.numpy as jnp
from jax import lax
from jax.experimental import pallas as pl
from jax.experimental.pallas import tpu as pltpu

F32 = jnp.float32
BF16 = jnp.bfloat16

D_MODEL = 1024
N_META = 16
EPS = 1e-6
M_HEADS = 4
M_DK = 128
M_DV = 256
POOL_WINDOWS = (2, 4, 8, 16)
POOL_GDIM = 256
POOL_HIST = 15
MLA_HEADS = 16
MLA_NOPE = 64
MLA_ROPE = 32
MLA_QK = 96
MLA_V = 64
MLA_Q_LORA = 512
MLA_KV_LORA = 256
MLA_CACHE_DIM = 288
ROPE_BASE = 10000.0
CONV_WIDTH = 31
CONV_HIST = 30
D_FF = 2816
FFN_HIST = 2

LANE = 128
FF_CHUNK = 256
N_FF_CHUNK = D_FF // FF_CHUNK
assert N_FF_CHUNK % 2 == 1 and N_FF_CHUNK >= 3
NEG = -1e30
LOG2E = 1.4426950408889634
VMEM_LIMIT = 56 * 1024 * 1024


def _dot(a, b):
    return jnp.dot(a, b, preferred_element_type=F32)


def _dot_nt(a, b):
    return lax.dot_general(a, b, (((1,), (1,)), ((), ())), preferred_element_type=F32)


def _dot_tn(a, b):
    return lax.dot_general(a, b, (((0,), (0,)), ((), ())), preferred_element_type=F32)


def _rms(xf, g):
    return xf * lax.rsqrt(jnp.mean(xf * xf, axis=-1, keepdims=True) + EPS) * g


def _sigmoid(x):
    return 1.0 / (1.0 + jnp.exp(-x))


def _valid_rows(i, tm, tps, pad):
    r = (i % tps) * tm + lax.broadcasted_iota(jnp.int32, (tm, 1), 0)
    return r >= pad


def _params(n_axes):
    return pltpu.CompilerParams(dimension_semantics=("arbitrary",) * n_axes,
                                vmem_limit_bytes=VMEM_LIMIT)


def _row_spec(tm, c):
    return pl.BlockSpec((tm, c), lambda i: (i, 0))


def _full_spec(shape):
    zeros = (0,) * len(shape)
    return pl.BlockSpec(shape, lambda *_: zeros)


def _sds(shape, dtype):
    return jax.ShapeDtypeStruct(shape, dtype)


def _m_in_kernel(x_ref, g_ref, wq, wk, wv, wo, wgh, wgl, bg,
                 q_o, k_o, v_o, o_o, gt_o, *, tps, pad):
    tm = x_ref.shape[0]
    hf = _rms(x_ref[...], g_ref[...])
    h = hf.astype(BF16)
    q_o[...] = (_dot(h, wq[...]) * (M_DK ** -0.5)).astype(BF16)
    k_o[...] = _dot(h, wk[...]).astype(BF16)
    v_o[...] = _dot(h, wv[...]).astype(BF16)
    o_o[...] = _sigmoid(_dot(h, wo[...])).astype(BF16)
    hl = (hf - h.astype(F32)).astype(BF16)
    gts = _dot(h, wgh[...]) + _dot(hl, wgh[...]) + _dot(h, wgl[...]) + bg[...]
    lane = lax.broadcasted_iota(jnp.int32, gts.shape, 1)
    lf = jnp.minimum(gts, 0.0) - jnp.log(1.0 + jnp.exp(-jnp.abs(gts)))
    out = jnp.where(lane < M_HEADS, gts, lf)
    if pad:
        valid = _valid_rows(pl.program_id(0), tm, tps, pad)
        out = jnp.where(valid, out, jnp.where(lane < M_HEADS, NEG, 0.0))
    gt_o[...] = out


def _m_in(x, g, w, *, tm, tps, pad):
    r = x.shape[0]
    kern = functools.partial(_m_in_kernel, tps=tps, pad=pad)
    return pl.pallas_call(
        kern, grid=(r // tm,),
        in_specs=[_row_spec(tm, D_MODEL), _full_spec((1, D_MODEL)),
                  _full_spec(w["wq"].shape), _full_spec(w["wk"].shape), _full_spec(w["wv"].shape),
                  _full_spec(w["wo"].shape), _full_spec(w["wgh"].shape), _full_spec(w["wgl"].shape),
                  _full_spec((1, LANE))],
        out_specs=[_row_spec(tm, 512), _row_spec(tm, 512), _row_spec(tm, 1024),
                   _row_spec(tm, 1024), _row_spec(tm, LANE)],
        out_shape=[_sds((r, 512), BF16), _sds((r, 512), BF16), _sds((r, 1024), BF16),
                   _sds((r, 1024), BF16), _sds((r, LANE), F32)],
        compiler_params=_params(1), name="mlstm_in",
    )(x, g, w["wq"], w["wk"], w["wv"], w["wo"], w["wgh"], w["wgl"], w["bg"])


def _split3(a):
    a1 = a.astype(BF16)
    r1 = a - a1.astype(F32)
    a2 = r1.astype(BF16)
    a3 = (r1 - a2.astype(F32)).astype(BF16)
    return a1, a2, a3


def _m_chunk_kernel(q_ref, k_ref, v_ref, gc_ref, gr_ref, c0_ref, n0_ref, m0_ref,
                    h_ref, c_o, n_o, m_o, c_s, n_s, m_s, *, ln):
    c = pl.program_id(1)
    nseq = q_ref.shape[0] // ln

    @pl.when(c == 0)
    def _():
        c_s[...] = c0_ref[...]
        n_s[...] = n0_ref[...]
        m_s[...] = m0_ref[...]

    row = lax.broadcasted_iota(jnp.int32, (ln, ln), 0)
    col = lax.broadcasted_iota(jnp.int32, (ln, ln), 1)
    causal = col <= row
    tril = jnp.where(causal, 1.0, 0.0).astype(BF16)
    triu = jnp.where(row <= col, 1.0, 0.0).astype(BF16)
    for sq in range(nseq):
        rows = slice(sq * ln, (sq + 1) * ln)
        gc = gc_ref[rows, :]
        gr = gr_ref[sq]
        c1, c2, c3 = _split3(gc)
        b_col = _dot(tril, c1) + _dot(tril, c2) + _dot(tril, c3)
        r1, r2, r3 = _split3(gr)
        b_row = _dot(r1, triu) + _dot(r2, triu) + _dot(r3, triu)

        for hd in range(M_HEADS):
            ig_c = gc[:, hd:hd + 1]
            b_c = b_col[:, M_HEADS + hd:M_HEADS + hd + 1]
            ig_r = gr[hd:hd + 1, :]
            b_r = b_row[M_HEADS + hd:M_HEADS + hd + 1, :]
            m0 = m_s[sq, hd]
            c0 = c_s[sq, hd]
            n0 = n_s[sq, hd]
            dmat = jnp.where(causal, b_c - b_r + ig_r, NEG)
            inter = b_c + m0
            m_t = jnp.maximum(inter, jnp.max(dmat, axis=1, keepdims=True))
            w_intra = jnp.exp(dmat - m_t)
            w_inter = jnp.exp(inter - m_t)
            qh = q_ref[rows, hd * M_DK:(hd + 1) * M_DK]
            kh = k_ref[rows, hd * M_DK:(hd + 1) * M_DK]
            vh = v_ref[rows, hd * M_DV:(hd + 1) * M_DV]
            s = _dot_nt(qh, kh) * w_intra
            num = _dot(s.astype(BF16), vh) + w_inter * _dot(qh, c0.astype(BF16))
            den = (jnp.sum(s, axis=1, keepdims=True)
                   + w_inter * jnp.sum(qh.astype(F32) * n0, axis=1, keepdims=True))
            h_ref[rows, hd * M_DV:(hd + 1) * M_DV] = num / jnp.maximum(jnp.abs(den), jnp.exp(-m_t))
            b_last = b_c[ln - 1:ln, :]
            m_new = m_t[ln - 1:ln, :]
            wk_c = jnp.exp(b_last - b_c + ig_c - m_new)
            decay = jnp.exp(b_last + m0 - m_new)
            vw = (vh.astype(F32) * wk_c).astype(BF16)
            c_s[sq, hd] = decay * c0 + _dot_tn(kh, vw)
            n_s[sq, hd] = decay * n0 + jnp.sum(kh.astype(F32) * wk_c, axis=0, keepdims=True)
            m_s[sq, hd] = m_new

    @pl.when(c == pl.num_programs(1) - 1)
    def _():
        c_o[...] = c_s[...]
        n_o[...] = n_s[...]
        m_o[...] = m_s[...]


def _m_chunk(q, k, v, gcol, grow, c0, n0, m0, *, ln, nseq=1):
    n = c0.shape[0]
    t = q.shape[0] // n
    nc = t // ln
    assert n % nseq == 0 and (nseq == 1 or nc == 1)
    rows = lambda w: pl.BlockSpec((nseq * ln, w), lambda i, c: (i * nc + c, 0))
    st_c = pl.BlockSpec((nseq, M_HEADS, M_DK, M_DV), lambda i, c: (i, 0, 0, 0))
    st_n = pl.BlockSpec((nseq, M_HEADS, 1, M_DK), lambda i, c: (i, 0, 0, 0))
    st_m = pl.BlockSpec((nseq, M_HEADS, 1, 1), lambda i, c: (i, 0, 0, 0))
    return pl.pallas_call(
        functools.partial(_m_chunk_kernel, ln=ln), grid=(n // nseq, nc),
        in_specs=[rows(512), rows(512), rows(1024), rows(LANE),
                  pl.BlockSpec((nseq, 8, ln), lambda i, c: (i, 0, c)), st_c, st_n, st_m],
        out_specs=[rows(1024), st_c, st_n, st_m],
        out_shape=[_sds((n * t, 1024), F32), _sds((n, M_HEADS, M_DK, M_DV), F32),
                   _sds((n, M_HEADS, 1, M_DK), F32), _sds((n, M_HEADS, 1, 1), F32)],
        scratch_shapes=[pltpu.VMEM((nseq, M_HEADS, M_DK, M_DV), F32),
                        pltpu.VMEM((nseq, M_HEADS, 1, M_DK), F32),
                        pltpu.VMEM((nseq, M_HEADS, 1, 1), F32)],
        compiler_params=_params(2), name="mlstm_chunk",
    )(q, k, v, gcol, grow, c0, n0, m0)


def _m_out_kernel(hh_ref, o_ref, x_ref, g_ref, w_ref, y_ref, *, tps, pad):
    tm = x_ref.shape[0]
    parts = []
    for hd in range(M_HEADS):
        sl = slice(hd * M_DV, (hd + 1) * M_DV)
        hn = _rms(hh_ref[:, sl], g_ref[:, sl])
        parts.append((hn * o_ref[:, sl].astype(F32)).astype(BF16))
    y = x_ref[...] + _dot(jnp.concatenate(parts, axis=1), w_ref[...])
    if pad:
        y = jnp.where(_valid_rows(pl.program_id(0), tm, tps, pad), y, 0.0)
    y_ref[...] = y


def _m_out(hh, o, x, g, w, *, tm, tps, pad):
    r = x.shape[0]
    kern = functools.partial(_m_out_kernel, tps=tps, pad=pad)
    return pl.pallas_call(
        kern, grid=(r // tm,),
        in_specs=[_row_spec(tm, 1024), _row_spec(tm, 1024), _row_spec(tm, D_MODEL),
                  _full_spec((1, 1024)), _full_spec((1024, D_MODEL))],
        out_specs=_row_spec(tm, D_MODEL), out_shape=_sds((r, D_MODEL), F32),
        compiler_params=_params(1), name="mlstm_out",
    )(hh, o, x, g, w)


def _ffn_kernel(x_ref, g_ref, wg_ref, wu_ref, cw_ref, cb_ref, wd_ref,
                y_ref, tail_ref, carry_s, gu_s, act_s, *, tps, pad):
    i = pl.program_id(0)
    tm = x_ref.shape[0]
    x = x_ref[...]
    h = _rms(x, g_ref[...]).astype(BF16)

    @pl.when(i % tps == 0)
    def _():
        carry_s[...] = jnp.zeros_like(carry_s)

    rowid = lax.broadcasted_iota(jnp.int32, (tm, 1), 0)

    def up(c, slot):
        gu_s[slot, 0] = _dot(h, wg_ref[c])
        gu_s[slot, 1] = _dot(h, wu_ref[c])

    def gate(c, slot):
        g = gu_s[slot, 0]
        u = gu_s[slot, 1]
        prev = carry_s[c]
        g1 = jnp.where(rowid == 0, prev[7:8], pltpu.roll(g, 1, 0))
        g2 = jnp.where(rowid == 0, prev[6:7], jnp.where(rowid == 1, prev[7:8], pltpu.roll(g, 2, 0)))
        cw = cw_ref[c]
        gc = cw[0:1] * g2 + cw[1:2] * g1 + cw[2:3] * g + cb_ref[c]
        act_s[slot] = (gc * _sigmoid(gc) * u).astype(BF16)
        last = g[tm - 8:tm]
        carry_s[c] = last
        tail_ref[0, c] = last

    def down(c, slot):
        return _dot(act_s[slot], wd_ref[c])

    y_ref[...] = x
    up(0, 0)
    up(1, 1)
    gate(0, 0)
    for t in range(2, N_FF_CHUNK - 1, 2):
        up(t, 0)
        gate(t - 1, 1)
        d0 = down(t - 2, 0)
        up(t + 1, 1)
        gate(t, 0)
        d1 = down(t - 1, 1)
        y_ref[...] += d0 + d1
    last = N_FF_CHUNK - 1
    up(last, 0)
    gate(last - 1, 1)
    d0 = down(last - 2, 0)
    gate(last, 0)
    d1 = down(last - 1, 1)
    y = y_ref[...] + d0 + d1 + down(last, 0)
    if pad:
        y = jnp.where(_valid_rows(i, tm, tps, pad), y, 0.0)
    y_ref[...] = y


def _ffn(x, g, w, *, tm, tps, pad):
    r = x.shape[0]
    nt = r // tm
    kern = functools.partial(_ffn_kernel, tps=tps, pad=pad)
    return pl.pallas_call(
        kern, grid=(nt,),
        in_specs=[_row_spec(tm, D_MODEL), _full_spec((1, D_MODEL)),
                  _full_spec(w["wg"].shape), _full_spec(w["wu"].shape), _full_spec(w["cw"].shape),
                  _full_spec(w["cb"].shape), _full_spec(w["wd"].shape)],
        out_specs=[_row_spec(tm, D_MODEL),
                   pl.BlockSpec((1, N_FF_CHUNK, 8, FF_CHUNK), lambda i: (i, 0, 0, 0))],
        out_shape=[_sds((r, D_MODEL), F32), _sds((nt, N_FF_CHUNK, 8, FF_CHUNK), F32)],
        scratch_shapes=[pltpu.VMEM((N_FF_CHUNK, 8, FF_CHUNK), F32),
                        pltpu.VMEM((2, 2, tm, FF_CHUNK), F32), pltpu.VMEM((2, tm, FF_CHUNK), BF16)],
        compiler_params=_params(1), name="ffn",
    )(x, g, w["wg"], w["wu"], w["cw"], w["cb"], w["wd"])


def _ffn_s_kernel(x_ref, g_ref, hist_ref, wg_ref, wu_ref, cw_ref, cb_ref, wd_ref,
                  y_ref, tail_ref, acc_s, *, nb):
    c = pl.program_id(0)
    x = x_ref[...]
    h = _rms(x, g_ref[...]).astype(BF16)

    @pl.when(c == 0)
    def _():
        acc_s[...] = jnp.zeros_like(acc_s)

    g = _dot(h, wg_ref[0])
    u = _dot(h, wu_ref[0])
    hist = hist_ref[...]
    t = g.shape[0]
    ext0 = jnp.concatenate([hist, g[:t - 2 * nb]], axis=0)
    ext1 = jnp.concatenate([hist[nb:], g[:t - nb]], axis=0)
    cw = cw_ref[0]
    gc = cw[0:1] * ext0 + cw[1:2] * ext1 + cw[2:3] * g + cb_ref[0]
    act = (gc * _sigmoid(gc) * u).astype(BF16)
    acc_s[...] += _dot(act, wd_ref[0])
    tail_ref[...] = g[t - 2 * nb:]

    @pl.when(c == pl.num_programs(0) - 1)
    def _():
        y_ref[...] = x + acc_s[...]


def _ffn_s(x, g, hist, w, *, nb):
    r = x.shape[0]
    kern = functools.partial(_ffn_s_kernel, nb=nb)
    chunk3 = lambda a, b: pl.BlockSpec((1, a, b), lambda c: (c, 0, 0))
    return pl.pallas_call(
        kern, grid=(N_FF_CHUNK,),
        in_specs=[_full_spec((r, D_MODEL)), _full_spec((1, D_MODEL)),
                  pl.BlockSpec((2 * nb, FF_CHUNK), lambda c: (0, c)),
                  chunk3(D_MODEL, FF_CHUNK), chunk3(D_MODEL, FF_CHUNK), chunk3(8, FF_CHUNK),
                  chunk3(1, FF_CHUNK), chunk3(FF_CHUNK, D_MODEL)],
        out_specs=[_full_spec((r, D_MODEL)), pl.BlockSpec((2 * nb, FF_CHUNK), lambda c: (0, c))],
        out_shape=[_sds((r, D_MODEL), F32), _sds((2 * nb, D_FF), F32)],
        scratch_shapes=[pltpu.VMEM((r, D_MODEL), F32)],
        compiler_params=_params(1), name="ffn_sample",
    )(x, g, hist, w["wg"], w["wu"], w["cw"], w["cb"], w["wd"])


def _pool_kernel(x_ref, g_ref, pw_ref, ps_ref, y_ref, tail_ref, ext_s, *, tps, pad):
    i = pl.program_id(0)
    tm = x_ref.shape[0]
    x = x_ref[...]
    u = _rms(x, g_ref[...])
    hist = POOL_HIST + 1

    @pl.when(i % tps == 0)
    def _():
        ext_s[0:hist] = jnp.zeros((hist, D_MODEL), F32)

    @pl.when(i % tps != 0)
    def _():
        ext_s[0:hist] = ext_s[tm:tm + hist]

    ext_s[hist:hist + tm] = u
    tail_ref[0] = u[tm - hist:]
    pos = (i % tps) * tm + lax.broadcasted_iota(jnp.int32, (tm, 1), 0) - pad
    valid = pos >= 0
    for gi, w in enumerate(POOL_WINDOWS):
        sl = slice(gi * POOL_GDIM, (gi + 1) * POOL_GDIM)
        ug = u[:, sl]
        win = ug
        for j in range(1, w):
            win = win + ext_s[hist - j:hist - j + tm, sl]
        cnt = jnp.maximum(jnp.minimum(pos + 1, w), 1).astype(F32)
        d = (win / cnt - ug).astype(BF16)
        y = x[:, sl] + _dot(d, pw_ref[gi]) * ps_ref[:, sl]
        y_ref[:, sl] = jnp.where(valid, y, 0.0)


def _pool(x, g, pw, ps, *, tm, tps, pad):
    r = x.shape[0]
    nt = r // tm
    kern = functools.partial(_pool_kernel, tps=tps, pad=pad)
    return pl.pallas_call(
        kern, grid=(nt,),
        in_specs=[_row_spec(tm, D_MODEL), _full_spec((1, D_MODEL)),
                  _full_spec((4, POOL_GDIM, POOL_GDIM)), _full_spec((1, D_MODEL))],
        out_specs=[_row_spec(tm, D_MODEL), pl.BlockSpec((1, 16, D_MODEL), lambda i: (i, 0, 0))],
        out_shape=[_sds((r, D_MODEL), F32), _sds((nt, 16, D_MODEL), F32)],
        scratch_shapes=[pltpu.VMEM((tm + 16, D_MODEL), F32)],
        compiler_params=_params(1), name="pool",
    )(x, g, pw, ps)


def _pool_s_kernel(x_ref, g_ref, hist_ref, pw_ref, ps_ref, y_ref, u_ref, *, nb, past_len):
    x = x_ref[...]
    u = _rms(x, g_ref[...])
    u_ref[...] = u
    nt = x.shape[0] // nb
    for gi, w in enumerate(POOL_WINDOWS):
        sl = slice(gi * POOL_GDIM, (gi + 1) * POOL_GDIM)
        outs = []
        for t in range(nt):
            win = u[t * nb:(t + 1) * nb, sl]
            for j in range(1, w):
                src = t - j
                if src >= 0:
                    win = win + u[src * nb:(src + 1) * nb, sl]
                else:
                    hrow = POOL_HIST + src
                    win = win + hist_ref[hrow * nb:(hrow + 1) * nb, sl]
            cnt = float(min(past_len + t + 1, w))
            outs.append(win / cnt - u[t * nb:(t + 1) * nb, sl])
        d = jnp.concatenate(outs, axis=0).astype(BF16)
        y_ref[:, sl] = x[:, sl] + _dot(d, pw_ref[gi]) * ps_ref[:, sl]


def _pool_s(x, g, hist, pw, ps, *, nb, past_len):
    r = x.shape[0]
    kern = functools.partial(_pool_s_kernel, nb=nb, past_len=past_len)
    return pl.pallas_call(
        kern, grid=(1,),
        in_specs=[_full_spec((r, D_MODEL)), _full_spec((1, D_MODEL)), _full_spec(hist.shape),
                  _full_spec((4, POOL_GDIM, POOL_GDIM)), _full_spec((1, D_MODEL))],
        out_specs=[_full_spec((r, D_MODEL)), _full_spec((r, D_MODEL))],
        out_shape=[_sds((r, D_MODEL), F32), _sds((r, D_MODEL), F32)],
        compiler_params=_params(1), name="pool_sample",
    )(x, g, hist, pw, ps)


def _rope_norm(z, g, tc, ts, live):
    zn = z * lax.rsqrt(jnp.sum(z * z * live, axis=-1, keepdims=True) * (1.0 / MLA_QK) + EPS) * g
    return zn * tc + pltpu.roll(zn, LANE - 16, 1) * ts


def _mla_in_kernel(x_ref, g_ref, wdq, qlg, wuq, qng, wlat, wkpe, kvg, wuk, wuv, kng,
                   tc_ref, ts_ref, qh_o, kh_o, vp_o, lat_o, kpe_o):
    h = _rms(x_ref[...], g_ref[...]).astype(BF16)
    cqn = _rms(_dot(h, wdq[...]), qlg[...]).astype(BF16)
    latn = _rms(_dot(h, wlat[...]), kvg[...])
    lat_o[...] = latn
    latb = latn.astype(BF16)
    kpe = _dot(h, wkpe[...])
    kpe_o[...] = kpe
    tc, ts = tc_ref[...], ts_ref[...]
    live = jnp.where(lax.broadcasted_iota(jnp.int32, (1, LANE), 1) < MLA_QK, 1.0, 0.0)
    for j in range(MLA_HEADS // 2):
        q2 = _dot(cqn, wuq[j])
        k2 = _dot(latb, wuk[j])
        for half in range(2):
            sl = slice(half * LANE, (half + 1) * LANE)
            qh_o[2 * j + half] = _rope_norm(q2[:, sl], qng[...], tc, ts, live).astype(qh_o.dtype)
            kh_o[2 * j + half] = _rope_norm(k2[:, sl] + kpe, kng[...], tc, ts, live).astype(BF16)
    ones = jnp.ones((latb.shape[0], LANE), BF16)
    for j in range(MLA_HEADS // 4):
        v4 = _dot(latb, wuv[j]).astype(BF16)
        vp_o[2 * j] = jnp.concatenate([v4[:, :LANE], ones], axis=1)
        vp_o[2 * j + 1] = jnp.concatenate([v4[:, LANE:], ones], axis=1)


def _mla_in(x, g, w, tabs, *, tm, qdtype):
    r = x.shape[0]
    heads = lambda n, w=LANE: pl.BlockSpec((n, tm, w), lambda i: (0, i, 0))
    tps = tabs[0].shape[0] // tm
    tab = pl.BlockSpec((tm, LANE), lambda i: (i % tps, 0))
    return pl.pallas_call(
        _mla_in_kernel, grid=(r // tm,),
        in_specs=[_row_spec(tm, D_MODEL), _full_spec((1, D_MODEL)),
                  _full_spec(w["wdq"].shape), _full_spec((1, MLA_Q_LORA)), _full_spec(w["wuq"].shape),
                  _full_spec((1, LANE)), _full_spec(w["wlat"].shape), _full_spec(w["wkpe"].shape),
                  _full_spec((1, MLA_KV_LORA)), _full_spec(w["wuk"].shape), _full_spec(w["wuv"].shape),
                  _full_spec((1, LANE)), tab, tab],
        out_specs=[heads(MLA_HEADS), heads(MLA_HEADS), heads(MLA_HEADS // 2, 2 * LANE),
                   _row_spec(tm, MLA_KV_LORA), _row_spec(tm, LANE)],
        out_shape=[_sds((MLA_HEADS, r, LANE), qdtype), _sds((MLA_HEADS, r, LANE), BF16),
                   _sds((MLA_HEADS // 2, r, 2 * LANE), BF16), _sds((r, MLA_KV_LORA), F32),
                   _sds((r, LANE), F32)],
        compiler_params=_params(1), name="mla_in",
    )(x, g, w["wdq"], w["qlg"], w["wuq"], w["qng"], w["wlat"], w["wkpe"], w["kvg"],
      w["wuk"], w["wuv"], w["kng"], *tabs)


def _attn_kernel(q_ref, k_ref, v_ref, o_ref, acc_s, m_s, *, pad, nsub):
    qi = pl.program_id(2)
    tq = q_ref.shape[1]
    tk = tq
    tqs = tq // nsub
    for hh in range(2):
        m_s[hh] = jnp.full((tq, LANE), NEG, F32)
        acc_s[hh] = jnp.zeros((tq, 2 * LANE), F32)

    def block(ki, masked):
        ks = pl.multiple_of(ki * tk, tk)
        vblk = v_ref[0, pl.ds(ks, tk), :]
        for qs in range(nsub):
            rows = slice(qs * tqs, (qs + 1) * tqs)
            for hh in range(2):
                s = _dot_nt(q_ref[hh, rows, :], k_ref[hh, pl.ds(ks, tk), :])
                if masked:
                    qpos = qi * tq + qs * tqs + lax.broadcasted_iota(jnp.int32, (tqs, tk), 0)
                    kpos = ki * tk + lax.broadcasted_iota(jnp.int32, (tqs, tk), 1)
                    s = jnp.where(kpos <= qpos, jnp.where(kpos >= pad, s, NEG), NEG)
                m_old = m_s[hh, rows, :]
                m_new = jnp.maximum(m_old, jnp.max(s, axis=1, keepdims=True))
                p = jnp.exp2(s - jnp.tile(m_new, (1, tk // LANE)))
                alpha = jnp.exp2(m_old - m_new)
                acc_s[hh, rows, :] = (jnp.tile(alpha, (1, 2)) * acc_s[hh, rows, :]
                                      + _dot(p.astype(BF16), vblk))
                m_s[hh, rows, :] = m_new

    block(0, True)

    def body(ki, carry):
        block(ki, False)
        return carry

    lax.fori_loop(1, qi, body, 0)

    @pl.when(qi > 0)
    def _():
        block(qi, True)

    lane = lax.broadcasted_iota(jnp.int32, (tq, LANE), 1)
    a0 = acc_s[0]
    a1 = acc_s[1]
    o_ref[...] = jnp.where(lane < MLA_V, a0[:, :LANE] / a0[:, LANE:],
                           a1[:, :LANE] / a1[:, LANE:]).astype(BF16)


def _attn(qh, kh, vp, *, n, tp, tq, pad):
    nq = tp // tq
    kern = functools.partial(_attn_kernel, pad=pad, nsub=2)
    return pl.pallas_call(
        kern, grid=(n, MLA_HEADS // 2, nq),
        in_specs=[pl.BlockSpec((2, tq, LANE), lambda b, hp, qi: (hp, b * nq + qi, 0)),
                  pl.BlockSpec((2, tp, LANE), lambda b, hp, qi: (hp, b, 0)),
                  pl.BlockSpec((1, tp, 2 * LANE), lambda b, hp, qi: (hp, b, 0))],
        out_specs=pl.BlockSpec((tq, LANE), lambda b, hp, qi: (b * nq + qi, hp)),
        out_shape=_sds((n * tp, MLA_HEADS * MLA_V), BF16),
        scratch_shapes=[pltpu.VMEM((2, tq, 2 * LANE), F32), pltpu.VMEM((2, tq, LANE), F32)],
        compiler_params=_params(3), name="attn",
    )(qh, kh, vp)


def _proj_out_kernel(o_ref, x_ref, w_ref, y_ref, *, tps, pad):
    tm = x_ref.shape[0]
    y = x_ref[...] + _dot(o_ref[...], w_ref[...])
    if pad:
        y = jnp.where(_valid_rows(pl.program_id(0), tm, tps, pad), y, 0.0)
    y_ref[...] = y


def _proj_out(o, x, w, *, tm, tps, pad):
    r = x.shape[0]
    kern = functools.partial(_proj_out_kernel, tps=tps, pad=pad)
    return pl.pallas_call(
        kern, grid=(r // tm,),
        in_specs=[_row_spec(tm, 1024), _row_spec(tm, D_MODEL), _full_spec((1024, D_MODEL))],
        out_specs=_row_spec(tm, D_MODEL), out_shape=_sds((r, D_MODEL), F32),
        compiler_params=_params(1), name="mla_out",
    )(o, x, w)


def _dec_prep_kernel(qh_ref, kng, wukt, qabs_o, ab_o):
    g = kng[...]
    g1 = g[:, MLA_NOPE:MLA_NOPE + 16]
    g2 = g[:, MLA_NOPE + 16:MLA_QK]
    for hd in range(MLA_HEADS):
        qh = qh_ref[hd]
        qabs_o[hd] = _dot((qh * g).astype(BF16), wukt[hd]).astype(BF16)
        q1 = qh[:, MLA_NOPE:MLA_NOPE + 16]
        q2 = qh[:, MLA_NOPE + 16:MLA_QK]
        ab_o[hd] = jnp.concatenate([g1 * q1, g2 * q2, g1 * q2, -(g2 * q1)], axis=1).astype(BF16)


def _dec_prep(qh, kng, wukt):
    r = qh.shape[1]
    return pl.pallas_call(
        _dec_prep_kernel, grid=(1,),
        in_specs=[_full_spec(qh.shape), _full_spec((1, LANE)), _full_spec(wukt.shape)],
        out_specs=[_full_spec((MLA_HEADS, r, MLA_KV_LORA)), _full_spec((MLA_HEADS, r, 64))],
        out_shape=[_sds((MLA_HEADS, r, MLA_KV_LORA), BF16), _sds((MLA_HEADS, r, 64), BF16)],
        compiler_params=_params(1), name="decode_prep",
    )(qh, kng, wukt)


def _decode_kernel(pt_ref, new_ref, qm_ref, ab_ref, wuk_ref, cs_ref, *rest,
                   n_pages, page, pages_per_chunk):
    del pt_ref
    page_refs = rest[:n_pages]
    (o_ref,) = rest[n_pages:]
    nq = qm_ref.shape[1]
    lhs = jnp.concatenate([qm_ref[0], wuk_ref[...]], axis=0)
    ab = ab_ref[0]
    ck = pages_per_chunk * page
    n_chunk = n_pages // pages_per_chunk
    past = n_pages * page
    nn = new_ref.shape[2]

    def scores(lat, kpe, cs):
        n = lat.shape[1]
        latb = lat.astype(BF16)
        kcs = (jnp.concatenate([kpe, kpe], axis=0) * cs).astype(BF16)
        big = _dot(lhs, latb)
        raw = big[:nq] + _dot(ab, kcs)
        kn = big[nq:]
        ss = jnp.sum((kn * kn).reshape(MLA_NOPE, MLA_HEADS, n), axis=0)
        ss = ss + jnp.sum(kpe * kpe, axis=0, keepdims=True)
        r = lax.rsqrt(ss * (1.0 / MLA_QK) + EPS)
        return latb, jnp.concatenate([r] * (nq // MLA_HEADS), axis=0) * raw

    def fold(state, latb, s):
        m, l, acc = state
        m_new = jnp.maximum(m, jnp.max(s, axis=1, keepdims=True))
        p = jnp.exp2(s - m_new)
        alpha = jnp.exp2(m - m_new)
        l = alpha * l + jnp.sum(p, axis=1, keepdims=True)
        acc = alpha * acc + _dot_nt(p.astype(BF16), latb)
        return m_new, l, acc

    state = (jnp.full((nq, 1), NEG, F32), jnp.zeros((nq, 1), F32),
             jnp.zeros((nq, MLA_KV_LORA), F32))
    for c in range(n_chunk):
        blks = [page_refs[c * pages_per_chunk + p][0] for p in range(pages_per_chunk)]
        lat = jnp.concatenate([b[:MLA_KV_LORA, :] for b in blks], axis=1)
        kpe = jnp.concatenate([b[MLA_KV_LORA:, :] for b in blks], axis=1)
        latb, s = scores(lat, kpe, cs_ref[:, c * ck:(c + 1) * ck])
        state = fold(state, latb, s)
    blk = new_ref[0]
    latb, s = scores(blk[:MLA_KV_LORA, :], blk[MLA_KV_LORA:, :], cs_ref[:, past:past + nn])
    key = lax.broadcasted_iota(jnp.int32, (nq, nn), 1)
    tok = lax.broadcasted_iota(jnp.int32, (nq, nn), 0) // MLA_HEADS
    _, l, acc = fold(state, latb, jnp.where(key <= tok, s, NEG))
    o_ref[0] = acc / l


def _decode(page_table, cache_t, new_t, qm, ab, wuk, cs, *, pages_per_chunk=16):
    nb, n_pages = page_table.shape
    assert n_pages % pages_per_chunk == 0
    page = cache_t.shape[2]
    nn = new_t.shape[2]
    nq = qm.shape[1]
    kern = functools.partial(_decode_kernel, n_pages=n_pages, page=page,
                             pages_per_chunk=pages_per_chunk)
    per_seq = lambda shape: pl.BlockSpec((1,) + shape, lambda b, pt: (b, 0, 0))
    page_specs = [pl.BlockSpec((1, MLA_CACHE_DIM, page),
                               lambda b, pt, j=j: (pt[b * n_pages + j], 0, 0))
                  for j in range(n_pages)]
    grid_spec = pltpu.PrefetchScalarGridSpec(
        num_scalar_prefetch=1, grid=(nb,),
        in_specs=[per_seq((MLA_CACHE_DIM, nn)), per_seq((nq, MLA_KV_LORA)), per_seq((nq, 64)),
                  pl.BlockSpec(wuk.shape, lambda b, pt: (0, 0)),
                  pl.BlockSpec(cs.shape, lambda b, pt: (0, 0))] + page_specs,
        out_specs=per_seq((nq, MLA_KV_LORA)))
    return pl.pallas_call(
        kern, grid_spec=grid_spec, out_shape=_sds((nb, nq, MLA_KV_LORA), F32),
        compiler_params=_params(1), name="decode",
    )(page_table.reshape(-1), new_t, qm, ab, wuk, cs, *([cache_t] * n_pages))


def _dec_out_kernel(ol_ref, wuv, wo, x_ref, y_ref):
    parts = [_dot(ol_ref[hd].astype(BF16), wuv[hd]) for hd in range(MLA_HEADS)]
    o = jnp.concatenate(parts, axis=1).astype(BF16)
    y_ref[...] = x_ref[...] + _dot(o, wo[...])


def _dec_out(ol, wuv, wo, x):
    r = x.shape[0]
    return pl.pallas_call(
        _dec_out_kernel, grid=(1,),
        in_specs=[_full_spec(ol.shape), _full_spec(wuv.shape),
                  _full_spec(wo.shape), _full_spec((r, D_MODEL))],
        out_specs=_full_spec((r, D_MODEL)), out_shape=_sds((r, D_MODEL), F32),
        compiler_params=_params(1), name="decode_out",
    )(ol, wuv, wo, x)


def _conv_in_kernel(x_ref, g_ref, w_ref, b_ref, gl_ref, *, tps, pad):
    tm = x_ref.shape[0]
    h = _rms(x_ref[...], g_ref[...]).astype(BF16)
    a = _dot(h, w_ref[...]) + b_ref[...]
    gl = a[:, :D_MODEL] * _sigmoid(a[:, D_MODEL:])
    if pad:
        gl = jnp.where(_valid_rows(pl.program_id(0), tm, tps, pad), gl, 0.0)
    gl_ref[...] = gl


def _conv_in(x, g, w, b, *, tm, tps, pad):
    r = x.shape[0]
    kern = functools.partial(_conv_in_kernel, tps=tps, pad=pad)
    return pl.pallas_call(
        kern, grid=(r // tm,),
        in_specs=[_row_spec(tm, D_MODEL), _full_spec((1, D_MODEL)),
                  _full_spec((D_MODEL, 2 * D_MODEL)), _full_spec((1, 2 * D_MODEL))],
        out_specs=_row_spec(tm, D_MODEL), out_shape=_sds((r, D_MODEL), F32),
        compiler_params=_params(1), name="conv_in",
    )(x, g, w, b)


def _ln_silu_proj(c, x, lng, lnb, w2, b2):
    mu = jnp.mean(c, axis=-1, keepdims=True)
    cc = c - mu
    var = jnp.mean(cc * cc, axis=-1, keepdims=True)
    y = cc * lax.rsqrt(var + EPS) * lng + lnb
    a = (y * _sigmoid(y)).astype(BF16)
    return x + _dot(a, w2) + b2


CONV_COLS = 256
CONV_ROWS = 64


def _conv_out_kernel(gl_ref, x_ref, wdw, bdw, lng, lnb, w2, b2, y_ref, ext_s, sh_s, c_s, *, tps, pad):
    i = pl.program_id(0)
    tm = x_ref.shape[0]
    hist = 32
    n_ext = tm + hist

    @pl.when(i % tps == 0)
    def _():
        ext_s[0:hist] = jnp.zeros((hist, D_MODEL), F32)

    @pl.when(i % tps != 0)
    def _():
        ext_s[0:hist] = ext_s[tm:tm + hist]

    ext_s[hist:n_ext] = gl_ref[...]
    ext_s[n_ext:n_ext + 8] = jnp.zeros((8, D_MODEL), F32)
    off = hist - CONV_HIST
    for cb in range(D_MODEL // CONV_COLS):
        cols = slice(cb * CONV_COLS, (cb + 1) * CONV_COLS)
        for s in range(8):
            sh_s[s] = ext_s[s:s + n_ext, cols]
        wblk = wdw[:, cols]
        bias = bdw[:, cols]

        def rows(rb, carry):
            r0 = pl.multiple_of(rb * CONV_ROWS, CONV_ROWS)
            acc = jnp.zeros((CONV_ROWS, CONV_COLS), F32)
            for j in range(CONV_WIDTH):
                a, s = divmod(off + j, 8)
                acc = acc + wblk[j:j + 1] * sh_s[s, pl.ds(r0 + 8 * a, CONV_ROWS), :]
            c_s[pl.ds(r0, CONV_ROWS), cols] = acc + bias
            return carry

        lax.fori_loop(0, tm // CONV_ROWS, rows, 0)
    y = _ln_silu_proj(c_s[...], x_ref[...], lng[...], lnb[...], w2[...], b2[...])
    if pad:
        y = jnp.where(_valid_rows(i, tm, tps, pad), y, 0.0)
    y_ref[...] = y


def _conv_out(gl, x, w, *, tm, tps, pad):
    r = x.shape[0]
    kern = functools.partial(_conv_out_kernel, tps=tps, pad=pad)
    vec = _full_spec((1, D_MODEL))
    return pl.pallas_call(
        kern, grid=(r // tm,),
        in_specs=[_row_spec(tm, D_MODEL), _row_spec(tm, D_MODEL), _full_spec((32, D_MODEL)),
                  vec, vec, vec, _full_spec((D_MODEL, D_MODEL)), vec],
        out_specs=_row_spec(tm, D_MODEL), out_shape=_sds((r, D_MODEL), F32),
        scratch_shapes=[pltpu.VMEM((tm + 40, D_MODEL), F32), pltpu.VMEM((8, tm + 32, CONV_COLS), F32),
                        pltpu.VMEM((tm, D_MODEL), F32)],
        compiler_params=_params(1), name="conv_out",
    )(gl, x, w["wdw"], w["bdw"], w["lng"], w["lnb"], w["w2"], w["b2"])


def _conv_out_s_kernel(hist_ref, gl_ref, x_ref, wdw, bdw, lng, lnb, w2, b2, y_ref, *, nb):
    nt = x_ref.shape[0] // nb
    outs = []
    for t in range(nt):
        acc = None
        for j in range(CONV_WIDTH):
            src = t + j
            if src < CONV_HIST:
                slab = hist_ref[src * nb:(src + 1) * nb, :]
            else:
                slab = gl_ref[(src - CONV_HIST) * nb:(src - CONV_HIST + 1) * nb, :]
            term = wdw[j:j + 1, :] * slab
            acc = term if acc is None else acc + term
        outs.append(acc)
    c = jnp.concatenate(outs, axis=0) + bdw[...]
    y_ref[...] = _ln_silu_proj(c, x_ref[...], lng[...], lnb[...], w2[...], b2[...])


def _conv_out_s(hist, gl, x, w, *, nb):
    r = x.shape[0]
    kern = functools.partial(_conv_out_s_kernel, nb=nb)
    vec = _full_spec((1, D_MODEL))
    return pl.pallas_call(
        kern, grid=(1,),
        in_specs=[_full_spec(hist.shape), _full_spec((r, D_MODEL)), _full_spec((r, D_MODEL)),
                  _full_spec((32, D_MODEL)), vec, vec, vec, _full_spec((D_MODEL, D_MODEL)), vec],
        out_specs=_full_spec((r, D_MODEL)), out_shape=_sds((r, D_MODEL), F32),
        compiler_params=_params(1), name="conv_out_sample",
    )(hist, gl, x, w["wdw"], w["bdw"], w["lng"], w["lnb"], w["w2"], w["b2"])


def _pad_lanes(a, width=LANE):
    return jnp.pad(a, [(0, 0)] * (a.ndim - 1) + [(0, width - a.shape[-1])])


def _row(a):
    return a.reshape(1, -1).astype(F32)


def _prep_mlstm(m_w_in, m_b_gates, m_head_norm_g, m_w_out):
    i0 = M_HEADS * M_DK
    i1 = 2 * i0
    i2 = i1 + M_HEADS * M_DV
    i3 = i2 + M_HEADS * M_DV
    wgate = _pad_lanes(m_w_in[:, i3:])
    wgh = wgate.astype(BF16)
    wgl = (wgate - wgh.astype(F32)).astype(BF16)
    return dict(wq=m_w_in[:, :i0].astype(BF16), wk=m_w_in[:, i0:i1].astype(BF16),
                wv=m_w_in[:, i1:i2].astype(BF16), wo=m_w_in[:, i2:i3].astype(BF16),
                wgh=wgh, wgl=wgl, bg=_pad_lanes(_row(m_b_gates)),
                hg=_row(m_head_norm_g), wout=m_w_out.astype(BF16))


def _prep_ffn(w_up, conv_w, conv_b, w_down):
    chunks = lambda a: a.reshape(D_MODEL, N_FF_CHUNK, FF_CHUNK).transpose(1, 0, 2).astype(BF16)
    cw = jnp.pad(conv_w, ((0, 8 - conv_w.shape[0]), (0, 0)))
    return dict(wg=chunks(w_up[:, :D_FF]), wu=chunks(w_up[:, D_FF:]),
                cw=cw.reshape(8, N_FF_CHUNK, FF_CHUNK).transpose(1, 0, 2).astype(F32),
                cb=conv_b.reshape(N_FF_CHUNK, 1, FF_CHUNK).astype(F32),
                wd=w_down.reshape(N_FF_CHUNK, FF_CHUNK, D_MODEL).astype(BF16))


def _prep_mla(mla_w_dq, mla_q_lora_g, mla_w_uq, mla_w_dkv, mla_kv_lora_g, mla_w_ukv,
              mla_q_norm_g, mla_k_norm_g, mla_w_o):
    pair = lambda a: (a.reshape(a.shape[0], MLA_HEADS // 2, 2 * a.shape[2]).transpose(1, 0, 2))
    half1 = slice(MLA_NOPE, MLA_NOPE + MLA_ROPE // 2)
    lanes = lambda a: _pad_lanes(jnp.concatenate([a, a[..., half1]], axis=-1))
    wuq = pair(lanes(mla_w_uq.reshape(MLA_Q_LORA, MLA_HEADS, MLA_QK)))
    ukv = mla_w_ukv.reshape(MLA_KV_LORA, MLA_HEADS, MLA_NOPE + MLA_V)
    uk = ukv[:, :, :MLA_NOPE]
    uv = ukv[:, :, MLA_NOPE:]
    wkpe = lanes(jnp.pad(mla_w_dkv[:, MLA_KV_LORA:], ((0, 0), (MLA_NOPE, 0))))
    qscale = (MLA_QK ** -0.5) * LOG2E
    return dict(
        wdq=mla_w_dq.astype(BF16), qlg=_row(mla_q_lora_g), wuq=wuq.astype(BF16),
        qng=lanes(_row(mla_q_norm_g)) * qscale,
        wlat=mla_w_dkv[:, :MLA_KV_LORA].astype(BF16), wkpe=wkpe.astype(BF16),
        kvg=_row(mla_kv_lora_g), wuk=pair(_pad_lanes(uk)).astype(BF16),
        wuv=uv.reshape(MLA_KV_LORA, MLA_HEADS // 4, 4 * MLA_V).transpose(1, 0, 2).astype(BF16),
        kng=lanes(_row(mla_k_norm_g)),
        wukt=jnp.pad(uk.transpose(1, 2, 0), ((0, 0), (0, LANE - MLA_NOPE), (0, 0))).astype(BF16),
        wuk_rows=uk.transpose(2, 1, 0).reshape(MLA_NOPE * MLA_HEADS, MLA_KV_LORA).astype(BF16),
        wuv_h=uv.transpose(1, 0, 2).astype(BF16), wo=mla_w_o.astype(BF16))


def _rope_tables(pos):
    half = MLA_ROPE // 2
    inv = ROPE_BASE ** (-jnp.arange(half, dtype=F32) / half)
    ang = pos.astype(F32)[:, None] * inv[None, :]
    cos, sin = jnp.cos(ang), jnp.sin(ang)
    r = pos.shape[0]
    ones = jnp.ones((r, MLA_NOPE), F32)
    z32 = jnp.zeros((r, LANE - MLA_QK), F32)
    z64 = jnp.zeros((r, MLA_NOPE), F32)
    tc = jnp.concatenate([ones, cos, cos, z32], axis=1)
    ts = jnp.concatenate([z64, -sin, sin, z32], axis=1)
    cs = jnp.concatenate([cos, cos, sin, sin], axis=1)
    return (tc, ts), cs


def _prompt_trunk(x_prompt, meta_tokens, norm_mix_g, norm_ffn_g, wm, wffn, pool_w, pool_scale,
                  wmla, wconv):
    n, seq, _ = x_prompt.shape
    t_real = seq + N_META
    tm = 768
    tp = -(-t_real // tm) * tm
    pad = tp - t_real
    tps = tp // tm
    meta = jnp.broadcast_to(meta_tokens[None].astype(F32), (n, N_META, D_MODEL))
    x = jnp.concatenate([jnp.zeros((n, pad, D_MODEL), F32), meta, x_prompt], axis=1)
    x = x.reshape(n * tp, D_MODEL)
    kw = dict(tm=tm, tps=tps, pad=pad)
    st = {}
    ffn_hist = []

    def ffn(x, layer):
        y, tail = _ffn(x, _row(norm_ffn_g[layer]), wffn[layer], **kw)
        tail = tail.reshape(n, tps, N_FF_CHUNK, 8, FF_CHUNK)[:, -1, :, 8 - FFN_HIST:, :]
        ffn_hist.append(tail.transpose(0, 2, 1, 3).reshape(n, FFN_HIST, D_FF))
        return y

    q, k, v, o, gt = _m_in(x, _row(norm_mix_g[0]), wm, **kw)
    grow = gt[:, :8].reshape(n, tp, 8).transpose(0, 2, 1)
    zc = jnp.zeros((n, M_HEADS, M_DK, M_DV), F32)
    zn = jnp.zeros((n, M_HEADS, 1, M_DK), F32)
    zm = jnp.zeros((n, M_HEADS, 1, 1), F32)
    hh, c_new, n_new, m_new = _m_chunk(q, k, v, gt, grow, zc, zn, zm, ln=tm // 2)
    st["mlstm_C"] = c_new
    st["mlstm_n"] = n_new.reshape(n, M_HEADS, M_DK)
    st["mlstm_m"] = m_new.reshape(n, M_HEADS)
    x = _m_out(hh, o, x, wm["hg"], wm["wout"], **kw)
    x = ffn(x, 0)

    x, tail = _pool(x, _row(norm_mix_g[1]), pool_w.astype(BF16), _row(pool_scale), **kw)
    st["pool"] = tail.reshape(n, tps, 16, D_MODEL)[:, -1, 16 - POOL_HIST:, :]
    x = ffn(x, 1)

    pos = jnp.maximum(jnp.arange(tp, dtype=jnp.int32) - pad, 0)
    tabs, _ = _rope_tables(pos)
    qh, kh, vp, lat, kpe = _mla_in(x, _row(norm_mix_g[2]), wmla, tabs, tm=tm, qdtype=BF16)
    rows = jnp.concatenate([lat, kpe[:, MLA_NOPE:MLA_QK]], axis=1)
    st["mla"] = rows.reshape(n, tp, MLA_CACHE_DIM)[:, pad:, :]
    o = _attn(qh, kh, vp, n=n, tp=tp, tq=tm, pad=pad)
    x = _proj_out(o, x, wmla["wo"], **kw)
    x = ffn(x, 2)

    gl = _conv_in(x, _row(norm_mix_g[3]), wconv["w1"], wconv["b1"], **kw)
    st["conv"] = gl.reshape(n, tp, D_MODEL)[:, tp - CONV_HIST:, :]
    x = _conv_out(gl, x, wconv, **kw)
    x = ffn(x, 3)

    st["ffn"] = jnp.stack(ffn_hist, axis=0)
    y = x.reshape(n, tp, D_MODEL)[:, pad + N_META:, :]
    return y, st


def _sample_trunk(x_sample, state_mlstm_C, state_mlstm_n, state_mlstm_m, state_pool, cache_mla,
                  page_table, state_conv, state_ffn, norm_mix_g, norm_ffn_g, wm, wffn, pool_w,
                  pool_scale, wmla, wconv):
    nb, nt, _ = x_sample.shape
    r = nb * nt
    past_len = page_table.shape[1] * cache_mla.shape[1]
    tmaj = lambda a: a.transpose(1, 0, 2).reshape(a.shape[1] * nb, a.shape[2])
    smaj = lambda a: a.reshape(-1, nb, a.shape[-1]).transpose(1, 0, 2)
    x = tmaj(x_sample.astype(F32))
    kw = dict(tm=r, tps=1, pad=0)
    st = {}
    ffn_hist = []

    def ffn(x, layer):
        y, tail = _ffn_s(x, _row(norm_ffn_g[layer]), tmaj(state_ffn[layer]), wffn[layer], nb=nb)
        ffn_hist.append(smaj(tail))
        return y

    ln = 16
    q, k, v, o, gt = _m_in(x, _row(norm_mix_g[0]), wm, **kw)

    def seq_rows(a, fill=None):
        a = smaj(a)
        if fill is None:
            a = jnp.pad(a, ((0, 0), (0, ln - nt), (0, 0)))
        else:
            a = jnp.concatenate([a, jnp.broadcast_to(fill, (nb, ln - nt, a.shape[2]))], axis=1)
        return a.reshape(nb * ln, a.shape[2])

    lane = jnp.arange(LANE)
    neutral = jnp.where(lane < M_HEADS, NEG, 0.0).astype(F32)
    gts = seq_rows(gt, fill=neutral)
    grow = gts[:, :8].reshape(nb, ln, 8).transpose(0, 2, 1)
    hh, c_new, n_new, m_new = _m_chunk(
        seq_rows(q), seq_rows(k), seq_rows(v), gts, grow, state_mlstm_C.astype(F32),
        state_mlstm_n.astype(F32).reshape(nb, M_HEADS, 1, M_DK),
        state_mlstm_m.astype(F32).reshape(nb, M_HEADS, 1, 1), ln=ln, nseq=4)
    st["mlstm_C"] = c_new
    st["mlstm_n"] = n_new.reshape(nb, M_HEADS, M_DK)
    st["mlstm_m"] = m_new.reshape(nb, M_HEADS)
    hh = tmaj(hh.reshape(nb, ln, M_HEADS * M_DV)[:, :nt, :])
    x = _m_out(hh, o, x, wm["hg"], wm["wout"], **kw)
    x = ffn(x, 0)

    x, u = _pool_s(x, _row(norm_mix_g[1]), tmaj(state_pool.astype(F32)), pool_w.astype(BF16),
                   _row(pool_scale), nb=nb, past_len=past_len)
    st["pool"] = jnp.concatenate([state_pool.astype(F32), smaj(u)], axis=1)[:, -POOL_HIST:, :]
    x = ffn(x, 1)

    pos_new = past_len + jnp.repeat(jnp.arange(nt, dtype=jnp.int32), nb)
    tabs, _ = _rope_tables(pos_new)
    _, cs = _rope_tables(jnp.arange(past_len + LANE, dtype=jnp.int32))
    qh, _, _, lat, kpe = _mla_in(x, _row(norm_mix_g[2]), wmla, tabs, tm=r, qdtype=F32)
    rows = smaj(jnp.concatenate([lat, kpe[:, MLA_NOPE:MLA_QK]], axis=1))
    st["mla"] = rows
    qabs, ab = _dec_prep(qh, wmla["kng"], wmla["wukt"])
    seq_q = lambda a: (a.reshape(MLA_HEADS, nt, nb, a.shape[-1]).transpose(2, 1, 0, 3)
                       .reshape(nb, nt * MLA_HEADS, a.shape[-1]))
    new_t = _pad_lanes(rows.transpose(0, 2, 1))
    cache_t = cache_mla.astype(F32).transpose(0, 2, 1)
    ol = _decode(page_table, cache_t, new_t, seq_q(qabs), seq_q(ab), wmla["wuk_rows"], cs.T)
    ol = ol.reshape(nb, nt, MLA_HEADS, MLA_KV_LORA).transpose(2, 1, 0, 3).reshape(MLA_HEADS, r, MLA_KV_LORA)
    x = _dec_out(ol, wmla["wuv_h"], wmla["wo"], x)
    x = ffn(x, 2)

    gl = _conv_in(x, _row(norm_mix_g[3]), wconv["w1"], wconv["b1"], **kw)
    st["conv"] = jnp.concatenate([state_conv.astype(F32), smaj(gl)], axis=1)[:, -CONV_HIST:, :]
    x = _conv_out_s(tmaj(state_conv.astype(F32)), gl, x, wconv, nb=nb)
    x = ffn(x, 3)

    st["ffn"] = jnp.stack(ffn_hist, axis=0)
    return smaj(x), st


def kernel(x_prompt, x_sample, state_mlstm_C, state_mlstm_n, state_mlstm_m, state_pool, cache_mla, page_table, state_conv, state_ffn, meta_tokens, norm_mix_g, norm_ffn_g, m_w_in, m_b_gates, m_head_norm_g, m_w_out, pool_w, pool_scale, mla_w_dq, mla_q_lora_g, mla_w_uq, mla_w_dkv, mla_kv_lora_g, mla_w_ukv, mla_q_norm_g, mla_k_norm_g, mla_w_o, conv_w_pw1, conv_b_pw1, conv_w_dw, conv_b_dw, conv_ln_g, conv_ln_b, conv_w_pw2, conv_b_pw2, ffn_w_up, ffn_conv_w, ffn_conv_b, ffn_w_down):
    depth = ffn_w_up.shape[0]
    wm = _prep_mlstm(m_w_in, m_b_gates, m_head_norm_g, m_w_out)
    wffn = [_prep_ffn(ffn_w_up[i], ffn_conv_w[i], ffn_conv_b[i], ffn_w_down[i]) for i in range(depth)]
    wmla = _prep_mla(mla_w_dq, mla_q_lora_g, mla_w_uq, mla_w_dkv, mla_kv_lora_g, mla_w_ukv,
                     mla_q_norm_g, mla_k_norm_g, mla_w_o)
    wconv = dict(w1=conv_w_pw1.astype(BF16), b1=_row(conv_b_pw1),
                 wdw=jnp.pad(conv_w_dw, ((0, 32 - CONV_WIDTH), (0, 0))).astype(F32),
                 bdw=_row(conv_b_dw), lng=_row(conv_ln_g), lnb=_row(conv_ln_b),
                 w2=conv_w_pw2.astype(BF16), b2=_row(conv_b_pw2))
    yp, sp = _prompt_trunk(x_prompt, meta_tokens, norm_mix_g, norm_ffn_g, wm, wffn, pool_w,
                           pool_scale, wmla, wconv)
    ys, ss = _sample_trunk(x_sample, state_mlstm_C, state_mlstm_n, state_mlstm_m, state_pool,
                           cache_mla, page_table, state_conv, state_ffn, norm_mix_g, norm_ffn_g,
                           wm, wffn, pool_w, pool_scale, wmla, wconv)
    names = ("mlstm_C", "mlstm_n", "mlstm_m", "pool", "mla", "conv", "ffn")
    return (yp, ys) + tuple(sp[k] for k in names) + tuple(ss[k] for k in names)
```

```python
import functools
import math

import jax
import jax.numpy as jnp
from jax import lax
from jax.experimental import pallas as pl
from jax.experimental.pallas import tpu as pltpu

F32 = jnp.float32
BF16 = jnp.bfloat16

D_MODEL = 1024
N_META = 16
EPS = 1e-6
M_HEADS = 4
M_DK = 128
M_DV = 256
POOL_WINDOWS = (2, 4, 8, 16)
POOL_GDIM = 256
POOL_HIST = 15
MLA_HEADS = 16
MLA_NOPE = 64
MLA_ROPE = 32
MLA_QK = 96
MLA_V = 64
MLA_Q_LORA = 512
MLA_KV_LORA = 256
MLA_CACHE_DIM = 288
ROPE_BASE = 10000.0
CONV_WIDTH = 31
CONV_HIST = 30
D_FF = 2816
FFN_HIST = 2

LANE = 128
FF_CHUNK = 256
N_FF_CHUNK = D_FF // FF_CHUNK
assert N_FF_CHUNK % 2 == 1 and N_FF_CHUNK >= 3
NEG = -1e30
LOG2E = 1.4426950408889634
VMEM_LIMIT = 56 * 1024 * 1024


def _dot(a, b):
    return jnp.dot(a, b, preferred_element_type=F32)


def _dot_nt(a, b):
    return lax.dot_general(a, b, (((1,), (1,)), ((), ())), preferred_element_type=F32)


def _dot_tn(a, b):
    return lax.dot_general(a, b, (((0,), (0,)), ((), ())), preferred_element_type=F32)


def _rms(xf, g):
    return xf * lax.rsqrt(jnp.mean(xf * xf, axis=-1, keepdims=True) + EPS) * g


def _sigmoid(x):
    return 1.0 / (1.0 + jnp.exp(-x))


def _valid_rows(i, tm, tps, pad):
    r = (i % tps) * tm + lax.broadcasted_iota(jnp.int32, (tm, 1), 0)
    return r >= pad


def _params(n_axes):
    return pltpu.CompilerParams(dimension_semantics=("arbitrary",) * n_axes,
                                vmem_limit_bytes=VMEM_LIMIT)


def _row_spec(tm, c):
    return pl.BlockSpec((tm, c), lambda i: (i, 0))


def _full_spec(shape):
    zeros = (0,) * len(shape)
    return pl.BlockSpec(shape, lambda *_: zeros)


def _sds(shape, dtype):
    return jax.ShapeDtypeStruct(shape, dtype)


def _m_in_kernel(x_ref, g_ref, wq, wk, wv, wo, wgh, wgl, bg,
                 q_o, k_o, v_o, o_o, gt_o, *, tps, pad):
    tm = x_ref.shape[0]
    hf = _rms(x_ref[...], g_ref[...])
    h = hf.astype(BF16)
    q_o[...] = (_dot(h, wq[...]) * (M_DK ** -0.5)).astype(BF16)
    k_o[...] = _dot(h, wk[...]).astype(BF16)
    v_o[...] = _dot(h, wv[...]).astype(BF16)
    o_o[...] = _sigmoid(_dot(h, wo[...])).astype(BF16)
    hl = (hf - h.astype(F32)).astype(BF16)
    gts = _dot(h, wgh[...]) + _dot(hl, wgh[...]) + _dot(h, wgl[...]) + bg[...]
    lane = lax.broadcasted_iota(jnp.int32, gts.shape, 1)
    lf = jnp.minimum(gts, 0.0) - jnp.log(1.0 + jnp.exp(-jnp.abs(gts)))
    out = jnp.where(lane < M_HEADS, gts, lf)
    if pad:
        valid = _valid_rows(pl.program_id(0), tm, tps, pad)
        out = jnp.where(valid, out, jnp.where(lane < M_HEADS, NEG, 0.0))
    gt_o[...] = out


def _m_in(x, g, w, *, tm, tps, pad):
    r = x.shape[0]
    kern = functools.partial(_m_in_kernel, tps=tps, pad=pad)
    return pl.pallas_call(
        kern, grid=(r // tm,),
        in_specs=[_row_spec(tm, D_MODEL), _full_spec((1, D_MODEL)),
                  _full_spec(w["wq"].shape), _full_spec(w["wk"].shape), _full_spec(w["wv"].shape),
                  _full_spec(w["wo"].shape), _full_spec(w["wgh"].shape), _full_spec(w["wgl"].shape),
                  _full_spec((1, LANE))],
        out_specs=[_row_spec(tm, 512), _row_spec(tm, 512), _row_spec(tm, 1024),
                   _row_spec(tm, 1024), _row_spec(tm, LANE)],
        out_shape=[_sds((r, 512), BF16), _sds((r, 512), BF16), _sds((r, 1024), BF16),
                   _sds((r, 1024), BF16), _sds((r, LANE), F32)],
        compiler_params=_params(1), name="mlstm_in",
    )(x, g, w["wq"], w["wk"], w["wv"], w["wo"], w["wgh"], w["wgl"], w["bg"])


def _split3(a):
    a1 = a.astype(BF16)
    r1 = a - a1.astype(F32)
    a2 = r1.astype(BF16)
    a3 = (r1 - a2.astype(F32)).astype(BF16)
    return a1, a2, a3


def _m_chunk_kernel(q_ref, k_ref, v_ref, gc_ref, gr_ref, c0_ref, n0_ref, m0_ref,
                    h_ref, c_o, n_o, m_o, c_s, n_s, m_s, *, ln):
    c = pl.program_id(1)
    nseq = q_ref.shape[0] // ln

    @pl.when(c == 0)
    def _():
        c_s[...] = c0_ref[...]
        n_s[...] = n0_ref[...]
        m_s[...] = m0_ref[...]

    row = lax.broadcasted_iota(jnp.int32, (ln, ln), 0)
    col = lax.broadcasted_iota(jnp.int32, (ln, ln), 1)
    causal = col <= row
    tril = jnp.where(causal, 1.0, 0.0).astype(BF16)
    triu = jnp.where(row <= col, 1.0, 0.0).astype(BF16)
    for sq in range(nseq):
        rows = slice(sq * ln, (sq + 1) * ln)
        gc = gc_ref[rows, :]
        gr = gr_ref[sq]
        c1, c2, c3 = _split3(gc)
        b_col = _dot(tril, c1) + _dot(tril, c2) + _dot(tril, c3)
        r1, r2, r3 = _split3(gr)
        b_row = _dot(r1, triu) + _dot(r2, triu) + _dot(r3, triu)

        for hd in range(M_HEADS):
            ig_c = gc[:, hd:hd + 1]
            b_c = b_col[:, M_HEADS + hd:M_HEADS + hd + 1]
            ig_r = gr[hd:hd + 1, :]
            b_r = b_row[M_HEADS + hd:M_HEADS + hd + 1, :]
            m0 = m_s[sq, hd]
            c0 = c_s[sq, hd]
            n0 = n_s[sq, hd]
            dmat = jnp.where(causal, b_c - b_r + ig_r, NEG)
            inter = b_c + m0
            m_t = jnp.maximum(inter, jnp.max(dmat, axis=1, keepdims=True))
            w_intra = jnp.exp(dmat - m_t)
            w_inter = jnp.exp(inter - m_t)
            qh = q_ref[rows, hd * M_DK:(hd + 1) * M_DK]
            kh = k_ref[rows, hd * M_DK:(hd + 1) * M_DK]
            vh = v_ref[rows, hd * M_DV:(hd + 1) * M_DV]
            s = _dot_nt(qh, kh) * w_intra
            num = _dot(s.astype(BF16), vh) + w_inter * _dot(qh, c0.astype(BF16))
            den = (jnp.sum(s, axis=1, keepdims=True)
                   + w_inter * jnp.sum(qh.astype(F32) * n0, axis=1, keepdims=True))
            h_ref[rows, hd * M_DV:(hd + 1) * M_DV] = num / jnp.maximum(jnp.abs(den), jnp.exp(-m_t))
            b_last = b_c[ln - 1:ln, :]
            m_new = m_t[ln - 1:ln, :]
            wk_c = jnp.exp(b_last - b_c + ig_c - m_new)
            decay = jnp.exp(b_last + m0 - m_new)
            vw = (vh.astype(F32) * wk_c).astype(BF16)
            c_s[sq, hd] = decay * c0 + _dot_tn(kh, vw)
            n_s[sq, hd] = decay * n0 + jnp.sum(kh.astype(F32) * wk_c, axis=0, keepdims=True)
            m_s[sq, hd] = m_new

    @pl.when(c == pl.num_programs(1) - 1)
    def _():
        c_o[...] = c_s[...]
        n_o[...] = n_s[...]
        m_o[...] = m_s[...]


def _m_chunk(q, k, v, gcol, grow, c0, n0, m0, *, ln, nseq=1):
    n = c0.shape[0]
    t = q.shape[0] // n
    nc = t // ln
    assert n % nseq == 0 and (nseq == 1 or nc == 1)
    rows = lambda w: pl.BlockSpec((nseq * ln, w), lambda i, c: (i * nc + c, 0))
    st_c = pl.BlockSpec((nseq, M_HEADS, M_DK, M_DV), lambda i, c: (i, 0, 0, 0))
    st_n = pl.BlockSpec((nseq, M_HEADS, 1, M_DK), lambda i, c: (i, 0, 0, 0))
    st_m = pl.BlockSpec((nseq, M_HEADS, 1, 1), lambda i, c: (i, 0, 0, 0))
    return pl.pallas_call(
        functools.partial(_m_chunk_kernel, ln=ln), grid=(n // nseq, nc),
        in_specs=[rows(512), rows(512), rows(1024), rows(LANE),
                  pl.BlockSpec((nseq, 8, ln), lambda i, c: (i, 0, c)), st_c, st_n, st_m],
        out_specs=[rows(1024), st_c, st_n, st_m],
        out_shape=[_sds((n * t, 1024), F32), _sds((n, M_HEADS, M_DK, M_DV), F32),
                   _sds((n, M_HEADS, 1, M_DK), F32), _sds((n, M_HEADS, 1, 1), F32)],
        scratch_shapes=[pltpu.VMEM((nseq, M_HEADS, M_DK, M_DV), F32),
                        pltpu.VMEM((nseq, M_HEADS, 1, M_DK), F32),
                        pltpu.VMEM((nseq, M_HEADS, 1, 1), F32)],
        compiler_params=_params(2), name="mlstm_chunk",
    )(q, k, v, gcol, grow, c0, n0, m0)


def _m_out_kernel(hh_ref, o_ref, x_ref, g_ref, w_ref, y_ref, *, tps, pad):
    tm = x_ref.shape[0]
    parts = []
    for hd in range(M_HEADS):
        sl = slice(hd * M_DV, (hd + 1) * M_DV)
        hn = _rms(hh_ref[:, sl], g_ref[:, sl])
        parts.append((hn * o_ref[:, sl].astype(F32)).astype(BF16))
    y = x_ref[...] + _dot(jnp.concatenate(parts, axis=1), w_ref[...])
    if pad:
        y = jnp.where(_valid_rows(pl.program_id(0), tm, tps, pad), y, 0.0)
    y_ref[...] = y


def _m_out(hh, o, x, g, w, *, tm, tps, pad):
    r = x.shape[0]
    kern = functools.partial(_m_out_kernel, tps=tps, pad=pad)
    return pl.pallas_call(
        kern, grid=(r // tm,),
        in_specs=[_row_spec(tm, 1024), _row_spec(tm, 1024), _row_spec(tm, D_MODEL),
                  _full_spec((1, 1024)), _full_spec((1024, D_MODEL))],
        out_specs=_row_spec(tm, D_MODEL), out_shape=_sds((r, D_MODEL), F32),
        compiler_params=_params(1), name="mlstm_out",
    )(hh, o, x, g, w)


def _ffn_kernel(x_ref, g_ref, wg_ref, wu_ref, cw_ref, cb_ref, wd_ref,
                y_ref, tail_ref, carry_s, gu_s, act_s, *, tps, pad):
    i = pl.program_id(0)
    tm = x_ref.shape[0]
    x = x_ref[...]
    h = _rms(x, g_ref[...]).astype(BF16)

    @pl.when(i % tps == 0)
    def _():
        carry_s[...] = jnp.zeros_like(carry_s)

    rowid = lax.broadcasted_iota(jnp.int32, (tm, 1), 0)

    def up(c, slot):
        gu_s[slot, 0] = _dot(h, wg_ref[c])
        gu_s[slot, 1] = _dot(h, wu_ref[c])

    def gate(c, slot):
        g = gu_s[slot, 0]
        u = gu_s[slot, 1]
        prev = carry_s[c]
        g1 = jnp.where(rowid == 0, prev[7:8], pltpu.roll(g, 1, 0))
        g2 = jnp.where(rowid == 0, prev[6:7], jnp.where(rowid == 1, prev[7:8], pltpu.roll(g, 2, 0)))
        cw = cw_ref[c]
        gc = cw[0:1] * g2 + cw[1:2] * g1 + cw[2:3] * g + cb_ref[c]
        act_s[slot] = (gc * _sigmoid(gc) * u).astype(BF16)
        last = g[tm - 8:tm]
        carry_s[c] = last
        tail_ref[0, c] = last

    def down(c, slot):
        return _dot(act_s[slot], wd_ref[c])

    y_ref[...] = x
    up(0, 0)
    up(1, 1)
    gate(0, 0)
    for t in range(2, N_FF_CHUNK - 1, 2):
        up(t, 0)
        gate(t - 1, 1)
        d0 = down(t - 2, 0)
        up(t + 1, 1)
        gate(t, 0)
        d1 = down(t - 1, 1)
        y_ref[...] += d0 + d1
    last = N_FF_CHUNK - 1
    up(last, 0)
    gate(last - 1, 1)
    d0 = down(last - 2, 0)
    gate(last, 0)
    d1 = down(last - 1, 1)
    y = y_ref[...] + d0 + d1 + down(last, 0)
    if pad:
        y = jnp.where(_valid_rows(i, tm, tps, pad), y, 0.0)
    y_ref[...] = y


def _ffn(x, g, w, *, tm, tps, pad):
    r = x.shape[0]
    nt = r // tm
    kern = functools.partial(_ffn_kernel, tps=tps, pad=pad)
    return pl.pallas_call(
        kern, grid=(nt,),
        in_specs=[_row_spec(tm, D_MODEL), _full_spec((1, D_MODEL)),
                  _full_spec(w["wg"].shape), _full_spec(w["wu"].shape), _full_spec(w["cw"].shape),
                  _full_spec(w["cb"].shape), _full_spec(w["wd"].shape)],
        out_specs=[_row_spec(tm, D_MODEL),
                   pl.BlockSpec((1, N_FF_CHUNK, 8, FF_CHUNK), lambda i: (i, 0, 0, 0))],
        out_shape=[_sds((r, D_MODEL), F32), _sds((nt, N_FF_CHUNK, 8, FF_CHUNK), F32)],
        scratch_shapes=[pltpu.VMEM((N_FF_CHUNK, 8, FF_CHUNK), F32),
                        pltpu.VMEM((2, 2, tm, FF_CHUNK), F32), pltpu.VMEM((2, tm, FF_CHUNK), BF16)],
        compiler_params=_params(1), name="ffn",
    )(x, g, w["wg"], w["wu"], w["cw"], w["cb"], w["wd"])


def _ffn_s_kernel(x_ref, g_ref, hist_ref, wg_ref, wu_ref, cw_ref, cb_ref, wd_ref,
                  y_ref, tail_ref, acc_s, *, nb):
    c = pl.program_id(0)
    x = x_ref[...]
    h = _rms(x, g_ref[...]).astype(BF16)

    @pl.when(c == 0)
    def _():
        acc_s[...] = jnp.zeros_like(acc_s)

    g = _dot(h, wg_ref[0])
    u = _dot(h, wu_ref[0])
    hist = hist_ref[...]
    t = g.shape[0]
    ext0 = jnp.concatenate([hist, g[:t - 2 * nb]], axis=0)
    ext1 = jnp.concatenate([hist[nb:], g[:t - nb]], axis=0)
    cw = cw_ref[0]
    gc = cw[0:1] * ext0 + cw[1:2] * ext1 + cw[2:3] * g + cb_ref[0]
    act = (gc * _sigmoid(gc) * u).astype(BF16)
    acc_s[...] += _dot(act, wd_ref[0])
    tail_ref[...] = g[t - 2 * nb:]

    @pl.when(c == pl.num_programs(0) - 1)
    def _():
        y_ref[...] = x + acc_s[...]


def _ffn_s(x, g, hist, w, *, nb):
    r = x.shape[0]
    kern = functools.partial(_ffn_s_kernel, nb=nb)
    chunk3 = lambda a, b: pl.BlockSpec((1, a, b), lambda c: (c, 0, 0))
    return pl.pallas_call(
        kern, grid=(N_FF_CHUNK,),
        in_specs=[_full_spec((r, D_MODEL)), _full_spec((1, D_MODEL)),
                  pl.BlockSpec((2 * nb, FF_CHUNK), lambda c: (0, c)),
                  chunk3(D_MODEL, FF_CHUNK), chunk3(D_MODEL, FF_CHUNK), chunk3(8, FF_CHUNK),
                  chunk3(1, FF_CHUNK), chunk3(FF_CHUNK, D_MODEL)],
        out_specs=[_full_spec((r, D_MODEL)), pl.BlockSpec((2 * nb, FF_CHUNK), lambda c: (0, c))],
        out_shape=[_sds((r, D_MODEL), F32), _sds((2 * nb, D_FF), F32)],
        scratch_shapes=[pltpu.VMEM((r, D_MODEL), F32)],
        compiler_params=_params(1), name="ffn_sample",
    )(x, g, hist, w["wg"], w["wu"], w["cw"], w["cb"], w["wd"])


def _pool_kernel(x_ref, g_ref, pw_ref, ps_ref, y_ref, tail_ref, ext_s, *, tps, pad):
    i = pl.program_id(0)
    tm = x_ref.shape[0]
    x = x_ref[...]
    u = _rms(x, g_ref[...])
    hist = POOL_HIST + 1

    @pl.when(i % tps == 0)
    def _():
        ext_s[0:hist] = jnp.zeros((hist, D_MODEL), F32)

    @pl.when(i % tps != 0)
    def _():
        ext_s[0:hist] = ext_s[tm:tm + hist]

    ext_s[hist:hist + tm] = u
    tail_ref[0] = u[tm - hist:]
    pos = (i % tps) * tm + lax.broadcasted_iota(jnp.int32, (tm, 1), 0) - pad
    valid = pos >= 0
    for gi, w in enumerate(POOL_WINDOWS):
        sl = slice(gi * POOL_GDIM, (gi + 1) * POOL_GDIM)
        ug = u[:, sl]
        win = ug
        for j in range(1, w):
            win = win + ext_s[hist - j:hist - j + tm, sl]
        cnt = jnp.maximum(jnp.minimum(pos + 1, w), 1).astype(F32)
        d = (win / cnt - ug).astype(BF16)
        y = x[:, sl] + _dot(d, pw_ref[gi]) * ps_ref[:, sl]
        y_ref[:, sl] = jnp.where(valid, y, 0.0)


def _pool(x, g, pw, ps, *, tm, tps, pad):
    r = x.shape[0]
    nt = r // tm
    kern = functools.partial(_pool_kernel, tps=tps, pad=pad)
    return pl.pallas_call(
        kern, grid=(nt,),
        in_specs=[_row_spec(tm, D_MODEL), _full_spec((1, D_MODEL)),
                  _full_spec((4, POOL_GDIM, POOL_GDIM)), _full_spec((1, D_MODEL))],
        out_specs=[_row_spec(tm, D_MODEL), pl.BlockSpec((1, 16, D_MODEL), lambda i: (i, 0, 0))],
        out_shape=[_sds((r, D_MODEL), F32), _sds((nt, 16, D_MODEL), F32)],
        scratch_shapes=[pltpu.VMEM((tm + 16, D_MODEL), F32)],
        compiler_params=_params(1), name="pool",
    )(x, g, pw, ps)


def _pool_s_kernel(x_ref, g_ref, hist_ref, pw_ref, ps_ref, y_ref, u_ref, *, nb, past_len):
    x = x_ref[...]
    u = _rms(x, g_ref[...])
    u_ref[...] = u
    nt = x.shape[0] // nb
    for gi, w in enumerate(POOL_WINDOWS):
        sl = slice(gi * POOL_GDIM, (gi + 1) * POOL_GDIM)
        outs = []
        for t in range(nt):
            win = u[t * nb:(t + 1) * nb, sl]
            for j in range(1, w):
                src = t - j
                if src >= 0:
                    win = win + u[src * nb:(src + 1) * nb, sl]
                else:
                    hrow = POOL_HIST + src
                    win = win + hist_ref[hrow * nb:(hrow + 1) * nb, sl]
            cnt = float(min(past_len + t + 1, w))
            outs.append(win / cnt - u[t * nb:(t + 1) * nb, sl])
        d = jnp.concatenate(outs, axis=0).astype(BF16)
        y_ref[:, sl] = x[:, sl] + _dot(d, pw_ref[gi]) * ps_ref[:, sl]


def _pool_s(x, g, hist, pw, ps, *, nb, past_len):
    r = x.shape[0]
    kern = functools.partial(_pool_s_kernel, nb=nb, past_len=past_len)
    return pl.pallas_call(
        kern, grid=(1,),
        in_specs=[_full_spec((r, D_MODEL)), _full_spec((1, D_MODEL)), _full_spec(hist.shape),
                  _full_spec((4, POOL_GDIM, POOL_GDIM)), _full_spec((1, D_MODEL))],
        out_specs=[_full_spec((r, D_MODEL)), _full_spec((r, D_MODEL))],
        out_shape=[_sds((r, D_MODEL), F32), _sds((r, D_MODEL), F32)],
        compiler_params=_params(1), name="pool_sample",
    )(x, g, hist, pw, ps)


def _rope_norm(z, g, tc, ts, live):
    zn = z * lax.rsqrt(jnp.sum(z * z * live, axis=-1, keepdims=True) * (1.0 / MLA_QK) + EPS) * g
    return zn * tc + pltpu.roll(zn, LANE - 16, 1) * ts


MLA_SUB = 256


def _mla_in_kernel(x_ref, g_ref, wdq, qlg, wuq, qng, wlat, wkpe, kvg, wuk, wuv, kng,
                   tc_ref, ts_ref, qh_o, kh_o, vp_o, lat_o, kpe_o):
    tm = x_ref.shape[0]
    live = jnp.where(lax.broadcasted_iota(jnp.int32, (1, LANE), 1) < MLA_QK, 1.0, 0.0)

    def sub(r, carry):
        rows = pl.ds(pl.multiple_of(r * MLA_SUB, MLA_SUB), MLA_SUB)
        h = _rms(x_ref[rows, :], g_ref[...]).astype(BF16)
        cqn = _rms(_dot(h, wdq[...]), qlg[...]).astype(BF16)
        latn = _rms(_dot(h, wlat[...]), kvg[...])
        lat_o[rows, :] = latn
        latb = latn.astype(BF16)
        kpe = _dot(h, wkpe[...])
        kpe_o[rows, :] = kpe
        tc, ts = tc_ref[rows, :], ts_ref[rows, :]
        for j in range(MLA_HEADS // 2):
            q2 = _dot(cqn, wuq[j])
            k2 = _dot(latb, wuk[j])
            for half in range(2):
                sl = slice(half * LANE, (half + 1) * LANE)
                qh_o[2 * j + half, rows, :] = _rope_norm(q2[:, sl], qng[...], tc, ts,
                                                         live).astype(qh_o.dtype)
                kh_o[2 * j + half, rows, :] = _rope_norm(k2[:, sl] + kpe, kng[...], tc, ts,
                                                         live).astype(BF16)
        ones = jnp.ones((MLA_SUB, LANE), BF16)
        for j in range(MLA_HEADS // 4):
            v4 = _dot(latb, wuv[j]).astype(BF16)
            vp_o[2 * j, rows, :] = jnp.concatenate([v4[:, :LANE], ones], axis=1)
            vp_o[2 * j + 1, rows, :] = jnp.concatenate([v4[:, LANE:], ones], axis=1)
        return carry

    assert tm % MLA_SUB == 0
    lax.fori_loop(0, tm // MLA_SUB, sub, 0)


def _mla_in(x, g, w, tabs, *, tm, qdtype):
    r = x.shape[0]
    heads = lambda n, w=LANE: pl.BlockSpec((n, tm, w), lambda i: (0, i, 0))
    tps = tabs[0].shape[0] // tm
    tab = pl.BlockSpec((tm, LANE), lambda i: (i % tps, 0))
    return pl.pallas_call(
        _mla_in_kernel, grid=(r // tm,),
        in_specs=[_row_spec(tm, D_MODEL), _full_spec((1, D_MODEL)),
                  _full_spec(w["wdq"].shape), _full_spec((1, MLA_Q_LORA)), _full_spec(w["wuq"].shape),
                  _full_spec((1, LANE)), _full_spec(w["wlat"].shape), _full_spec(w["wkpe"].shape),
                  _full_spec((1, MLA_KV_LORA)), _full_spec(w["wuk"].shape), _full_spec(w["wuv"].shape),
                  _full_spec((1, LANE)), tab, tab],
        out_specs=[heads(MLA_HEADS), heads(MLA_HEADS), heads(MLA_HEADS // 2, 2 * LANE),
                   _row_spec(tm, MLA_KV_LORA), _row_spec(tm, LANE)],
        out_shape=[_sds((MLA_HEADS, r, LANE), qdtype), _sds((MLA_HEADS, r, LANE), BF16),
                   _sds((MLA_HEADS // 2, r, 2 * LANE), BF16), _sds((r, MLA_KV_LORA), F32),
                   _sds((r, LANE), F32)],
        compiler_params=_params(1), name="mla_in",
    )(x, g, w["wdq"], w["qlg"], w["wuq"], w["qng"], w["wlat"], w["wkpe"], w["kvg"],
      w["wuk"], w["wuv"], w["kng"], *tabs)


def _attn_kernel(q_ref, k_ref, v_ref, o_ref, acc_s, m_s, *, pad, nsub):
    qi = pl.program_id(2)
    tq = q_ref.shape[1]
    tk = tq
    tqs = tq // nsub
    for hh in range(2):
        m_s[hh] = jnp.full((tq, LANE), NEG, F32)
        acc_s[hh] = jnp.zeros((tq, 2 * LANE), F32)

    def block(ki, masked):
        ks = pl.multiple_of(ki * tk, tk)
        vblk = v_ref[0, pl.ds(ks, tk), :]
        for qs in range(nsub):
            rows = slice(qs * tqs, (qs + 1) * tqs)
            for hh in range(2):
                s = _dot_nt(q_ref[hh, rows, :], k_ref[hh, pl.ds(ks, tk), :])
                if masked:
                    qpos = qi * tq + qs * tqs + lax.broadcasted_iota(jnp.int32, (tqs, tk), 0)
                    kpos = ki * tk + lax.broadcasted_iota(jnp.int32, (tqs, tk), 1)
                    s = jnp.where(kpos <= qpos, jnp.where(kpos >= pad, s, NEG), NEG)
                m_old = m_s[hh, rows, :]
                m_new = jnp.maximum(m_old, jnp.max(s, axis=1, keepdims=True))
                p = jnp.exp2(s - jnp.tile(m_new, (1, tk // LANE)))
                alpha = jnp.exp2(m_old - m_new)
                acc_s[hh, rows, :] = (jnp.tile(alpha, (1, 2)) * acc_s[hh, rows, :]
                                      + _dot(p.astype(BF16), vblk))
                m_s[hh, rows, :] = m_new

    block(0, True)

    def body(ki, carry):
        block(ki, False)
        return carry

    lax.fori_loop(1, qi, body, 0)

    @pl.when(qi > 0)
    def _():
        block(qi, True)

    lane = lax.broadcasted_iota(jnp.int32, (tq, LANE), 1)
    a0 = acc_s[0]
    a1 = acc_s[1]
    o_ref[...] = jnp.where(lane < MLA_V, a0[:, :LANE] / a0[:, LANE:],
                           a1[:, :LANE] / a1[:, LANE:]).astype(BF16)


def _attn(qh, kh, vp, *, n, tp, tq, pad):
    nq = tp // tq
    kern = functools.partial(_attn_kernel, pad=pad, nsub=2)
    return pl.pallas_call(
        kern, grid=(n, MLA_HEADS // 2, nq),
        in_specs=[pl.BlockSpec((2, tq, LANE), lambda b, hp, qi: (hp, b * nq + qi, 0)),
                  pl.BlockSpec((2, tp, LANE), lambda b, hp, qi: (hp, b, 0)),
                  pl.BlockSpec((1, tp, 2 * LANE), lambda b, hp, qi: (hp, b, 0))],
        out_specs=pl.BlockSpec((tq, LANE), lambda b, hp, qi: (b * nq + qi, hp)),
        out_shape=_sds((n * tp, MLA_HEADS * MLA_V), BF16),
        scratch_shapes=[pltpu.VMEM((2, tq, 2 * LANE), F32), pltpu.VMEM((2, tq, LANE), F32)],
        compiler_params=_params(3), name="attn",
    )(qh, kh, vp)


def _proj_out_kernel(o_ref, x_ref, w_ref, y_ref, *, tps, pad):
    tm = x_ref.shape[0]
    y = x_ref[...] + _dot(o_ref[...], w_ref[...])
    if pad:
        y = jnp.where(_valid_rows(pl.program_id(0), tm, tps, pad), y, 0.0)
    y_ref[...] = y


def _proj_out(o, x, w, *, tm, tps, pad):
    r = x.shape[0]
    kern = functools.partial(_proj_out_kernel, tps=tps, pad=pad)
    return pl.pallas_call(
        kern, grid=(r // tm,),
        in_specs=[_row_spec(tm, 1024), _row_spec(tm, D_MODEL), _full_spec((1024, D_MODEL))],
        out_specs=_row_spec(tm, D_MODEL), out_shape=_sds((r, D_MODEL), F32),
        compiler_params=_params(1), name="mla_out",
    )(o, x, w)


def _dec_prep_kernel(qh_ref, kng, wukt, qabs_o, ab_o):
    g = kng[...]
    g1 = g[:, MLA_NOPE:MLA_NOPE + 16]
    g2 = g[:, MLA_NOPE + 16:MLA_QK]
    for hd in range(MLA_HEADS):
        qh = qh_ref[hd]
        qabs_o[hd] = _dot((qh * g).astype(BF16), wukt[hd]).astype(BF16)
        q1 = qh[:, MLA_NOPE:MLA_NOPE + 16]
        q2 = qh[:, MLA_NOPE + 16:MLA_QK]
        ab_o[hd] = jnp.concatenate([g1 * q1, g2 * q2, g1 * q2, -(g2 * q1)], axis=1).astype(BF16)


def _dec_prep(qh, kng, wukt):
    r = qh.shape[1]
    return pl.pallas_call(
        _dec_prep_kernel, grid=(1,),
        in_specs=[_full_spec(qh.shape), _full_spec((1, LANE)), _full_spec(wukt.shape)],
        out_specs=[_full_spec((MLA_HEADS, r, MLA_KV_LORA)), _full_spec((MLA_HEADS, r, 64))],
        out_shape=[_sds((MLA_HEADS, r, MLA_KV_LORA), BF16), _sds((MLA_HEADS, r, 64), BF16)],
        compiler_params=_params(1), name="decode_prep",
    )(qh, kng, wukt)


def _decode_kernel(pt_ref, new_ref, qm_ref, ab_ref, wuk_ref, cs_ref, *rest,
                   n_pages, page, pages_per_chunk):
    del pt_ref
    page_refs = rest[:n_pages]
    (o_ref,) = rest[n_pages:]
    nq = qm_ref.shape[1]
    lhs = jnp.concatenate([qm_ref[0], wuk_ref[...]], axis=0)
    ab = ab_ref[0]
    ck = pages_per_chunk * page
    n_chunk = n_pages // pages_per_chunk
    past = n_pages * page
    nn = new_ref.shape[2]

    def scores(lat, kpe, cs):
        n = lat.shape[1]
        latb = lat.astype(BF16)
        kcs = (jnp.concatenate([kpe, kpe], axis=0) * cs).astype(BF16)
        big = _dot(lhs, latb)
        raw = big[:nq] + _dot(ab, kcs)
        kn = big[nq:]
        ss = jnp.sum((kn * kn).reshape(MLA_NOPE, MLA_HEADS, n), axis=0)
        ss = ss + jnp.sum(kpe * kpe, axis=0, keepdims=True)
        r = lax.rsqrt(ss * (1.0 / MLA_QK) + EPS)
        return latb, jnp.concatenate([r] * (nq // MLA_HEADS), axis=0) * raw

    def fold(state, latb, s):
        m, l, acc = state
        m_new = jnp.maximum(m, jnp.max(s, axis=1, keepdims=True))
        p = jnp.exp2(s - m_new)
        alpha = jnp.exp2(m - m_new)
        l = alpha * l + jnp.sum(p, axis=1, keepdims=True)
        acc = alpha * acc + _dot_nt(p.astype(BF16), latb)
        return m_new, l, acc

    state = (jnp.full((nq, 1), NEG, F32), jnp.zeros((nq, 1), F32),
             jnp.zeros((nq, MLA_KV_LORA), F32))
    for c in range(n_chunk):
        blks = [page_refs[c * pages_per_chunk + p][0] for p in range(pages_per_chunk)]
        lat = jnp.concatenate([b[:MLA_KV_LORA, :] for b in blks], axis=1)
        kpe = jnp.concatenate([b[MLA_KV_LORA:, :] for b in blks], axis=1)
        latb, s = scores(lat, kpe, cs_ref[:, c * ck:(c + 1) * ck])
        state = fold(state, latb, s)
    blk = new_ref[0]
    latb, s = scores(blk[:MLA_KV_LORA, :], blk[MLA_KV_LORA:, :], cs_ref[:, past:past + nn])
    key = lax.broadcasted_iota(jnp.int32, (nq, nn), 1)
    tok = lax.broadcasted_iota(jnp.int32, (nq, nn), 0) // MLA_HEADS
    _, l, acc = fold(state, latb, jnp.where(key <= tok, s, NEG))
    o_ref[0] = acc / l


def _decode(page_table, cache_t, new_t, qm, ab, wuk, cs, *, pages_per_chunk=16):
    nb, n_pages = page_table.shape
    assert n_pages % pages_per_chunk == 0
    page = cache_t.shape[2]
    nn = new_t.shape[2]
    nq = qm.shape[1]
    kern = functools.partial(_decode_kernel, n_pages=n_pages, page=page,
                             pages_per_chunk=pages_per_chunk)
    per_seq = lambda shape: pl.BlockSpec((1,) + shape, lambda b, pt: (b, 0, 0))
    page_specs = [pl.BlockSpec((1, MLA_CACHE_DIM, page),
                               lambda b, pt, j=j: (pt[b * n_pages + j], 0, 0))
                  for j in range(n_pages)]
    grid_spec = pltpu.PrefetchScalarGridSpec(
        num_scalar_prefetch=1, grid=(nb,),
        in_specs=[per_seq((MLA_CACHE_DIM, nn)), per_seq((nq, MLA_KV_LORA)), per_seq((nq, 64)),
                  pl.BlockSpec(wuk.shape, lambda b, pt: (0, 0)),
                  pl.BlockSpec(cs.shape, lambda b, pt: (0, 0))] + page_specs,
        out_specs=per_seq((nq, MLA_KV_LORA)))
    return pl.pallas_call(
        kern, grid_spec=grid_spec, out_shape=_sds((nb, nq, MLA_KV_LORA), F32),
        compiler_params=_params(1), name="decode",
    )(page_table.reshape(-1), new_t, qm, ab, wuk, cs, *([cache_t] * n_pages))


def _dec_out_kernel(ol_ref, wuv, wo, x_ref, y_ref):
    parts = [_dot(ol_ref[hd].astype(BF16), wuv[hd]) for hd in range(MLA_HEADS)]
    o = jnp.concatenate(parts, axis=1).astype(BF16)
    y_ref[...] = x_ref[...] + _dot(o, wo[...])


def _dec_out(ol, wuv, wo, x):
    r = x.shape[0]
    return pl.pallas_call(
        _dec_out_kernel, grid=(1,),
        in_specs=[_full_spec(ol.shape), _full_spec(wuv.shape),
                  _full_spec(wo.shape), _full_spec((r, D_MODEL))],
        out_specs=_full_spec((r, D_MODEL)), out_shape=_sds((r, D_MODEL), F32),
        compiler_params=_params(1), name="decode_out",
    )(ol, wuv, wo, x)


def _conv_in_kernel(x_ref, g_ref, w_ref, b_ref, gl_ref, *, tps, pad):
    tm = x_ref.shape[0]
    h = _rms(x_ref[...], g_ref[...]).astype(BF16)
    a = _dot(h, w_ref[...]) + b_ref[...]
    gl = a[:, :D_MODEL] * _sigmoid(a[:, D_MODEL:])
    if pad:
        gl = jnp.where(_valid_rows(pl.program_id(0), tm, tps, pad), gl, 0.0)
    gl_ref[...] = gl


def _conv_in(x, g, w, b, *, tm, tps, pad):
    r = x.shape[0]
    kern = functools.partial(_conv_in_kernel, tps=tps, pad=pad)
    return pl.pallas_call(
        kern, grid=(r // tm,),
        in_specs=[_row_spec(tm, D_MODEL), _full_spec((1, D_MODEL)),
                  _full_spec((D_MODEL, 2 * D_MODEL)), _full_spec((1, 2 * D_MODEL))],
        out_specs=_row_spec(tm, D_MODEL), out_shape=_sds((r, D_MODEL), F32),
        compiler_params=_params(1), name="conv_in",
    )(x, g, w, b)


def _ln_silu_proj(c, x, lng, lnb, w2, b2):
    mu = jnp.mean(c, axis=-1, keepdims=True)
    cc = c - mu
    var = jnp.mean(cc * cc, axis=-1, keepdims=True)
    y = cc * lax.rsqrt(var + EPS) * lng + lnb
    a = (y * _sigmoid(y)).astype(BF16)
    return x + _dot(a, w2) + b2


CONV_COLS = 256
CONV_ROWS = 64


def _conv_out_kernel(gl_ref, x_ref, wdw, bdw, lng, lnb, w2, b2, y_ref, ext_s, sh_s, c_s, *, tps, pad):
    i = pl.program_id(0)
    tm = x_ref.shape[0]
    hist = 32
    n_ext = tm + hist

    @pl.when(i % tps == 0)
    def _():
        ext_s[0:hist] = jnp.zeros((hist, D_MODEL), F32)

    @pl.when(i % tps != 0)
    def _():
        ext_s[0:hist] = ext_s[tm:tm + hist]

    ext_s[hist:n_ext] = gl_ref[...]
    ext_s[n_ext:n_ext + 8] = jnp.zeros((8, D_MODEL), F32)
    off = hist - CONV_HIST
    for cb in range(D_MODEL // CONV_COLS):
        cols = slice(cb * CONV_COLS, (cb + 1) * CONV_COLS)
        for s in range(8):
            sh_s[s] = ext_s[s:s + n_ext, cols]
        wblk = wdw[:, cols]
        bias = bdw[:, cols]

        def rows(rb, carry):
            r0 = pl.multiple_of(rb * CONV_ROWS, CONV_ROWS)
            acc = jnp.zeros((CONV_ROWS, CONV_COLS), F32)
            for j in range(CONV_WIDTH):
                a, s = divmod(off + j, 8)
                acc = acc + wblk[j:j + 1] * sh_s[s, pl.ds(r0 + 8 * a, CONV_ROWS), :]
            c_s[pl.ds(r0, CONV_ROWS), cols] = acc + bias
            return carry

        lax.fori_loop(0, tm // CONV_ROWS, rows, 0)
    y = _ln_silu_proj(c_s[...], x_ref[...], lng[...], lnb[...], w2[...], b2[...])
    if pad:
        y = jnp.where(_valid_rows(i, tm, tps, pad), y, 0.0)
    y_ref[...] = y


def _conv_out(gl, x, w, *, tm, tps, pad):
    r = x.shape[0]
    kern = functools.partial(_conv_out_kernel, tps=tps, pad=pad)
    vec = _full_spec((1, D_MODEL))
    return pl.pallas_call(
        kern, grid=(r // tm,),
        in_specs=[_row_spec(tm, D_MODEL), _row_spec(tm, D_MODEL), _full_spec((32, D_MODEL)),
                  vec, vec, vec, _full_spec((D_MODEL, D_MODEL)), vec],
        out_specs=_row_spec(tm, D_MODEL), out_shape=_sds((r, D_MODEL), F32),
        scratch_shapes=[pltpu.VMEM((tm + 40, D_MODEL), F32), pltpu.VMEM((8, tm + 32, CONV_COLS), F32),
                        pltpu.VMEM((tm, D_MODEL), F32)],
        compiler_params=_params(1), name="conv_out",
    )(gl, x, w["wdw"], w["bdw"], w["lng"], w["lnb"], w["w2"], w["b2"])


def _conv_out_s_kernel(hist_ref, gl_ref, x_ref, wdw, bdw, lng, lnb, w2, b2, y_ref, *, nb):
    nt = x_ref.shape[0] // nb
    outs = []
    for t in range(nt):
        acc = None
        for j in range(CONV_WIDTH):
            src = t + j
            if src < CONV_HIST:
                slab = hist_ref[src * nb:(src + 1) * nb, :]
            else:
                slab = gl_ref[(src - CONV_HIST) * nb:(src - CONV_HIST + 1) * nb, :]
            term = wdw[j:j + 1, :] * slab
            acc = term if acc is None else acc + term
        outs.append(acc)
    c = jnp.concatenate(outs, axis=0) + bdw[...]
    y_ref[...] = _ln_silu_proj(c, x_ref[...], lng[...], lnb[...], w2[...], b2[...])


def _conv_out_s(hist, gl, x, w, *, nb):
    r = x.shape[0]
    kern = functools.partial(_conv_out_s_kernel, nb=nb)
    vec = _full_spec((1, D_MODEL))
    return pl.pallas_call(
        kern, grid=(1,),
        in_specs=[_full_spec(hist.shape), _full_spec((r, D_MODEL)), _full_spec((r, D_MODEL)),
                  _full_spec((32, D_MODEL)), vec, vec, vec, _full_spec((D_MODEL, D_MODEL)), vec],
        out_specs=_full_spec((r, D_MODEL)), out_shape=_sds((r, D_MODEL), F32),
        compiler_params=_params(1), name="conv_out_sample",
    )(hist, gl, x, w["wdw"], w["bdw"], w["lng"], w["lnb"], w["w2"], w["b2"])


def _pad_lanes(a, width=LANE):
    return jnp.pad(a, [(0, 0)] * (a.ndim - 1) + [(0, width - a.shape[-1])])


def _row(a):
    return a.reshape(1, -1).astype(F32)


def _prep_mlstm(m_w_in, m_b_gates, m_head_norm_g, m_w_out):
    i0 = M_HEADS * M_DK
    i1 = 2 * i0
    i2 = i1 + M_HEADS * M_DV
    i3 = i2 + M_HEADS * M_DV
    wgate = _pad_lanes(m_w_in[:, i3:])
    wgh = wgate.astype(BF16)
    wgl = (wgate - wgh.astype(F32)).astype(BF16)
    return dict(wq=m_w_in[:, :i0].astype(BF16), wk=m_w_in[:, i0:i1].astype(BF16),
                wv=m_w_in[:, i1:i2].astype(BF16), wo=m_w_in[:, i2:i3].astype(BF16),
                wgh=wgh, wgl=wgl, bg=_pad_lanes(_row(m_b_gates)),
                hg=_row(m_head_norm_g), wout=m_w_out.astype(BF16))


def _prep_ffn(w_up, conv_w, conv_b, w_down):
    chunks = lambda a: a.reshape(D_MODEL, N_FF_CHUNK, FF_CHUNK).transpose(1, 0, 2).astype(BF16)
    cw = jnp.pad(conv_w, ((0, 8 - conv_w.shape[0]), (0, 0)))
    return dict(wg=chunks(w_up[:, :D_FF]), wu=chunks(w_up[:, D_FF:]),
                cw=cw.reshape(8, N_FF_CHUNK, FF_CHUNK).transpose(1, 0, 2).astype(F32),
                cb=conv_b.reshape(N_FF_CHUNK, 1, FF_CHUNK).astype(F32),
                wd=w_down.reshape(N_FF_CHUNK, FF_CHUNK, D_MODEL).astype(BF16))


def _prep_mla(mla_w_dq, mla_q_lora_g, mla_w_uq, mla_w_dkv, mla_kv_lora_g, mla_w_ukv,
              mla_q_norm_g, mla_k_norm_g, mla_w_o):
    pair = lambda a: (a.reshape(a.shape[0], MLA_HEADS // 2, 2 * a.shape[2]).transpose(1, 0, 2))
    half1 = slice(MLA_NOPE, MLA_NOPE + MLA_ROPE // 2)
    lanes = lambda a: _pad_lanes(jnp.concatenate([a, a[..., half1]], axis=-1))
    wuq = pair(lanes(mla_w_uq.reshape(MLA_Q_LORA, MLA_HEADS, MLA_QK)))
    ukv = mla_w_ukv.reshape(MLA_KV_LORA, MLA_HEADS, MLA_NOPE + MLA_V)
    uk = ukv[:, :, :MLA_NOPE]
    uv = ukv[:, :, MLA_NOPE:]
    wkpe = lanes(jnp.pad(mla_w_dkv[:, MLA_KV_LORA:], ((0, 0), (MLA_NOPE, 0))))
    qscale = (MLA_QK ** -0.5) * LOG2E
    return dict(
        wdq=mla_w_dq.astype(BF16), qlg=_row(mla_q_lora_g), wuq=wuq.astype(BF16),
        qng=lanes(_row(mla_q_norm_g)) * qscale,
        wlat=mla_w_dkv[:, :MLA_KV_LORA].astype(BF16), wkpe=wkpe.astype(BF16),
        kvg=_row(mla_kv_lora_g), wuk=pair(_pad_lanes(uk)).astype(BF16),
        wuv=uv.reshape(MLA_KV_LORA, MLA_HEADS // 4, 4 * MLA_V).transpose(1, 0, 2).astype(BF16),
        kng=lanes(_row(mla_k_norm_g)),
        wukt=jnp.pad(uk.transpose(1, 2, 0), ((0, 0), (0, LANE - MLA_NOPE), (0, 0))).astype(BF16),
        wuk_rows=uk.transpose(2, 1, 0).reshape(MLA_NOPE * MLA_HEADS, MLA_KV_LORA).astype(BF16),
        wuv_h=uv.transpose(1, 0, 2).astype(BF16), wo=mla_w_o.astype(BF16))


def _rope_tables(pos):
    half = MLA_ROPE // 2
    inv = ROPE_BASE ** (-jnp.arange(half, dtype=F32) / half)
    ang = pos.astype(F32)[:, None] * inv[None, :]
    cos, sin = jnp.cos(ang), jnp.sin(ang)
    r = pos.shape[0]
    ones = jnp.ones((r, MLA_NOPE), F32)
    z32 = jnp.zeros((r, LANE - MLA_QK), F32)
    z64 = jnp.zeros((r, MLA_NOPE), F32)
    tc = jnp.concatenate([ones, cos, cos, z32], axis=1)
    ts = jnp.concatenate([z64, -sin, sin, z32], axis=1)
    cs = jnp.concatenate([cos, cos, sin, sin], axis=1)
    return (tc, ts), cs


def _prompt_trunk(x_prompt, meta_tokens, norm_mix_g, norm_ffn_g, wm, wffn, pool_w, pool_scale,
                  wmla, wconv):
    n, seq, _ = x_prompt.shape
    t_real = seq + N_META
    tm = 768
    tp = -(-t_real // tm) * tm
    pad = tp - t_real
    tps = tp // tm
    meta = jnp.broadcast_to(meta_tokens[None].astype(F32), (n, N_META, D_MODEL))
    x = jnp.concatenate([jnp.zeros((n, pad, D_MODEL), F32), meta, x_prompt], axis=1)
    x = x.reshape(n * tp, D_MODEL)
    kw = dict(tm=tm, tps=tps, pad=pad)
    st = {}
    ffn_hist = []

    def ffn(x, layer):
        y, tail = _ffn(x, _row(norm_ffn_g[layer]), wffn[layer], **kw)
        tail = tail.reshape(n, tps, N_FF_CHUNK, 8, FF_CHUNK)[:, -1, :, 8 - FFN_HIST:, :]
        ffn_hist.append(tail.transpose(0, 2, 1, 3).reshape(n, FFN_HIST, D_FF))
        return y

    q, k, v, o, gt = _m_in(x, _row(norm_mix_g[0]), wm, **kw)
    grow = gt[:, :8].reshape(n, tp, 8).transpose(0, 2, 1)
    zc = jnp.zeros((n, M_HEADS, M_DK, M_DV), F32)
    zn = jnp.zeros((n, M_HEADS, 1, M_DK), F32)
    zm = jnp.zeros((n, M_HEADS, 1, 1), F32)
    hh, c_new, n_new, m_new = _m_chunk(q, k, v, gt, grow, zc, zn, zm, ln=tm // 2)
    st["mlstm_C"] = c_new
    st["mlstm_n"] = n_new.reshape(n, M_HEADS, M_DK)
    st["mlstm_m"] = m_new.reshape(n, M_HEADS)
    x = _m_out(hh, o, x, wm["hg"], wm["wout"], **kw)
    x = ffn(x, 0)

    x, tail = _pool(x, _row(norm_mix_g[1]), pool_w.astype(BF16), _row(pool_scale), **kw)
    st["pool"] = tail.reshape(n, tps, 16, D_MODEL)[:, -1, 16 - POOL_HIST:, :]
    x = ffn(x, 1)

    pos = jnp.maximum(jnp.arange(tp, dtype=jnp.int32) - pad, 0)
    tabs, _ = _rope_tables(pos)
    qh, kh, vp, lat, kpe = _mla_in(x, _row(norm_mix_g[2]), wmla, tabs, tm=tm, qdtype=BF16)
    rows = jnp.concatenate([lat, kpe[:, MLA_NOPE:MLA_QK]], axis=1)
    st["mla"] = rows.reshape(n, tp, MLA_CACHE_DIM)[:, pad:, :]
    o = _attn(qh, kh, vp, n=n, tp=tp, tq=tm, pad=pad)
    x = _proj_out(o, x, wmla["wo"], **kw)
    x = ffn(x, 2)

    gl = _conv_in(x, _row(norm_mix_g[3]), wconv["w1"], wconv["b1"], **kw)
    st["conv"] = gl.reshape(n, tp, D_MODEL)[:, tp - CONV_HIST:, :]
    x = _conv_out(gl, x, wconv, **kw)
    x = ffn(x, 3)

    st["ffn"] = jnp.stack(ffn_hist, axis=0)
    y = x.reshape(n, tp, D_MODEL)[:, pad + N_META:, :]
    return y, st


def _sample_trunk(x_sample, state_mlstm_C, state_mlstm_n, state_mlstm_m, state_pool, cache_mla,
                  page_table, state_conv, state_ffn, norm_mix_g, norm_ffn_g, wm, wffn, pool_w,
                  pool_scale, wmla, wconv):
    nb, nt, _ = x_sample.shape
    r = nb * nt
    past_len = page_table.shape[1] * cache_mla.shape[1]
    tmaj = lambda a: a.transpose(1, 0, 2).reshape(a.shape[1] * nb, a.shape[2])
    smaj = lambda a: a.reshape(-1, nb, a.shape[-1]).transpose(1, 0, 2)
    x = tmaj(x_sample.astype(F32))
    kw = dict(tm=r, tps=1, pad=0)
    st = {}
    ffn_hist = []

    def ffn(x, layer):
        y, tail = _ffn_s(x, _row(norm_ffn_g[layer]), tmaj(state_ffn[layer]), wffn[layer], nb=nb)
        ffn_hist.append(smaj(tail))
        return y

    ln = 16
    q, k, v, o, gt = _m_in(x, _row(norm_mix_g[0]), wm, **kw)

    def seq_rows(a, fill=None):
        a = smaj(a)
        if fill is None:
            a = jnp.pad(a, ((0, 0), (0, ln - nt), (0, 0)))
        else:
            a = jnp.concatenate([a, jnp.broadcast_to(fill, (nb, ln - nt, a.shape[2]))], axis=1)
        return a.reshape(nb * ln, a.shape[2])

    lane = jnp.arange(LANE)
    neutral = jnp.where(lane < M_HEADS, NEG, 0.0).astype(F32)
    gts = seq_rows(gt, fill=neutral)
    grow = gts[:, :8].reshape(nb, ln, 8).transpose(0, 2, 1)
    hh, c_new, n_new, m_new = _m_chunk(
        seq_rows(q), seq_rows(k), seq_rows(v), gts, grow, state_mlstm_C.astype(F32),
        state_mlstm_n.astype(F32).reshape(nb, M_HEADS, 1, M_DK),
        state_mlstm_m.astype(F32).reshape(nb, M_HEADS, 1, 1), ln=ln, nseq=4)
    st["mlstm_C"] = c_new
    st["mlstm_n"] = n_new.reshape(nb, M_HEADS, M_DK)
    st["mlstm_m"] = m_new.reshape(nb, M_HEADS)
    hh = tmaj(hh.reshape(nb, ln, M_HEADS * M_DV)[:, :nt, :])
    x = _m_out(hh, o, x, wm["hg"], wm["wout"], **kw)
    x = ffn(x, 0)

    x, u = _pool_s(x, _row(norm_mix_g[1]), tmaj(state_pool.astype(F32)), pool_w.astype(BF16),
                   _row(pool_scale), nb=nb, past_len=past_len)
    st["pool"] = jnp.concatenate([state_pool.astype(F32), smaj(u)], axis=1)[:, -POOL_HIST:, :]
    x = ffn(x, 1)

    pos_new = past_len + jnp.repeat(jnp.arange(nt, dtype=jnp.int32), nb)
    tabs, _ = _rope_tables(pos_new)
    _, cs = _rope_tables(jnp.arange(past_len + LANE, dtype=jnp.int32))
    qh, _, _, lat, kpe = _mla_in(x, _row(norm_mix_g[2]), wmla, tabs, tm=r, qdtype=F32)
    rows = smaj(jnp.concatenate([lat, kpe[:, MLA_NOPE:MLA_QK]], axis=1))
    st["mla"] = rows
    qabs, ab = _dec_prep(qh, wmla["kng"], wmla["wukt"])
    seq_q = lambda a: (a.reshape(MLA_HEADS, nt, nb, a.shape[-1]).transpose(2, 1, 0, 3)
                       .reshape(nb, nt * MLA_HEADS, a.shape[-1]))
    new_t = _pad_lanes(rows.transpose(0, 2, 1))
    cache_t = cache_mla.astype(F32).transpose(0, 2, 1)
    ol = _decode(page_table, cache_t, new_t, seq_q(qabs), seq_q(ab), wmla["wuk_rows"], cs.T)
    ol = ol.reshape(nb, nt, MLA_HEADS, MLA_KV_LORA).transpose(2, 1, 0, 3).reshape(MLA_HEADS, r, MLA_KV_LORA)
    x = _dec_out(ol, wmla["wuv_h"], wmla["wo"], x)
    x = ffn(x, 2)

    gl = _conv_in(x, _row(norm_mix_g[3]), wconv["w1"], wconv["b1"], **kw)
    st["conv"] = jnp.concatenate([state_conv.astype(F32), smaj(gl)], axis=1)[:, -CONV_HIST:, :]
    x = _conv_out_s(tmaj(state_conv.astype(F32)), gl, x, wconv, nb=nb)
    x = ffn(x, 3)

    st["ffn"] = jnp.stack(ffn_hist, axis=0)
    return smaj(x), st


def kernel(x_prompt, x_sample, state_mlstm_C, state_mlstm_n, state_mlstm_m, state_pool, cache_mla, page_table, state_conv, state_ffn, meta_tokens, norm_mix_g, norm_ffn_g, m_w_in, m_b_gates, m_head_norm_g, m_w_out, pool_w, pool_scale, mla_w_dq, mla_q_lora_g, mla_w_uq, mla_w_dkv, mla_kv_lora_g, mla_w_ukv, mla_q_norm_g, mla_k_norm_g, mla_w_o, conv_w_pw1, conv_b_pw1, conv_w_dw, conv_b_dw, conv_ln_g, conv_ln_b, conv_w_pw2, conv_b_pw2, ffn_w_up, ffn_conv_w, ffn_conv_b, ffn_w_down):
    depth = ffn_w_up.shape[0]
    wm = _prep_mlstm(m_w_in, m_b_gates, m_head_norm_g, m_w_out)
    wffn = [_prep_ffn(ffn_w_up[i], ffn_conv_w[i], ffn_conv_b[i], ffn_w_down[i]) for i in range(depth)]
    wmla = _prep_mla(mla_w_dq, mla_q_lora_g, mla_w_uq, mla_w_dkv, mla_kv_lora_g, mla_w_ukv,
                     mla_q_norm_g, mla_k_norm_g, mla_w_o)
    wconv = dict(w1=conv_w_pw1.astype(BF16), b1=_row(conv_b_pw1),
                 wdw=jnp.pad(conv_w_dw, ((0, 32 - CONV_WIDTH), (0, 0))).astype(F32),
                 bdw=_row(conv_b_dw), lng=_row(conv_ln_g), lnb=_row(conv_ln_b),
                 w2=conv_w_pw2.astype(BF16), b2=_row(conv_b_pw2))
    yp, sp = _prompt_trunk(x_prompt, meta_tokens, norm_mix_g, norm_ffn_g, wm, wffn, pool_w,
                           pool_scale, wmla, wconv)
    ys, ss = _sample_trunk(x_sample, state_mlstm_C, state_mlstm_n, state_mlstm_m, state_pool,
                           cache_mla, page_table, state_conv, state_ffn, norm_mix_g, norm_ffn_g,
                           wm, wffn, pool_w, pool_scale, wmla, wconv)
    names = ("mlstm_C", "mlstm_n", "mlstm_m", "pool", "mla", "conv", "ffn")
    return (yp, ys) + tuple(sp[k] for k in names) + tuple(ss[k] for k in names)
```
